```python
import math
import jax, jax.numpy as jnp
from jax import lax
import numpy as np

D_MODEL = 1024
BATCH = 8
SEQ = 2048
DEPTH = 1
DEC_BATCH = 32
DEC_SEQ = 4
PAST_LEN = 8192
PAGE_SIZE = 128

MIX_WIDTH = D_MODEL
ATTN_WIDTH = MIX_WIDTH // 2
POOL_WIDTH = MIX_WIDTH - ATTN_WIDTH
N_HEADS = 4
HEAD_DIM = ATTN_WIDTH // (2 * N_HEADS)
V_DIM = 2 * HEAD_DIM
IN_WIDTH = 3 * ATTN_WIDTH + POOL_WIDTH
POOL_WINDOWS = (2, 4, 8, 16)
N_POOL_GROUPS = len(POOL_WINDOWS)
POOL_GROUP = POOL_WIDTH // N_POOL_GROUPS
POOL_HIST = max(POOL_WINDOWS) - 1
N_BUCKETS = 32
MAX_DISTANCE = 128
N_EXP_GROUPS = 4
EXP_PER_GROUP = 4
EXP_TOP_K = 2
D_EXPERT = D_MODEL // 2
Q_BLOCK = 128
ALPHA = (2 * DEPTH) ** 0.25
BETA = (8 * DEPTH) ** -0.25
LN_EPS = 1e-5
RMS_EPS = 1e-5

kernel_name = 'hybrid_diffattn_pool_hmoe_step'


def layer_norm(x, g, b):
    xf = x.astype(jnp.float32)
    mu = jnp.mean(xf, -1, keepdims=True)
    var = jnp.mean(jnp.square(xf - mu), -1, keepdims=True)
    return ((xf - mu) * lax.rsqrt(var + LN_EPS) * g.astype(jnp.float32) + b.astype(jnp.float32)).astype(x.dtype)


def rel_bucket(rel):
    n = jnp.maximum(rel, 0)
    max_exact = N_BUCKETS // 2
    nf = jnp.maximum(n, 1).astype(jnp.float32)
    large = max_exact + (jnp.log(nf / max_exact) / math.log(MAX_DISTANCE / max_exact)
                         * (N_BUCKETS - max_exact)).astype(jnp.int32)
    large = jnp.minimum(large, N_BUCKETS - 1)
    return jnp.where(n < max_exact, n, large)


def rel_bias_for(q_pos, k_pos, rel_bias):
    b = rel_bucket(q_pos[:, None] - k_pos[None, :])
    return jnp.transpose(rel_bias[b], (2, 0, 1)).astype(jnp.float32)


def diff_attend(q, k, v, q_pos, k_pos, lam, rel_bias):
    s = jnp.einsum('nqhcd,nkhcd->nhcqk', q, k, preferred_element_type=jnp.float32) * (HEAD_DIM ** -0.5)
    s = s + rel_bias_for(q_pos, k_pos, rel_bias)[None, :, None]
    causal = k_pos[None, :] <= q_pos[:, None]
    s = jnp.where(causal, s, -jnp.inf)
    p = jax.nn.softmax(s, axis=-1)
    a = p[:, :, 0] - lam * p[:, :, 1]
    return jnp.einsum('nhqk,nkhe->nqhe', a.astype(v.dtype), v)


def prompt_attention(q, k, v, lam, rel_bias):
    n, s = q.shape[0], q.shape[1]
    k_pos = jnp.arange(s)

    def one_block(i):
        qb = lax.dynamic_slice_in_dim(q, i * Q_BLOCK, Q_BLOCK, axis=1)
        q_pos = i * Q_BLOCK + jnp.arange(Q_BLOCK)
        return diff_attend(qb, k, v, q_pos, k_pos, lam, rel_bias)

    o = lax.map(one_block, jnp.arange(s // Q_BLOCK))
    return jnp.moveaxis(o, 0, 1).reshape(n, s, N_HEADS, V_DIM)


def sample_attention(q, k_new, v_new, k_pages, v_pages, page_table, lam, rel_bias):
    n, t = q.shape[0], q.shape[1]
    k_past = k_pages[page_table].reshape(n, -1, N_HEADS, 2, HEAD_DIM)
    v_past = v_pages[page_table].reshape(n, -1, N_HEADS, V_DIM)
    past = k_past.shape[1]
    k_all = jnp.concatenate([k_past, k_new], axis=1)
    v_all = jnp.concatenate([v_past, v_new], axis=1)
    q_pos = past + jnp.arange(t)
    k_pos = jnp.arange(past + t)
    return diff_attend(q, k_all, v_all, q_pos, k_pos, lam, rel_bias)


def diff_head_norm(o, g, lam_init):
    of = o.astype(jnp.float32)
    of = of * lax.rsqrt(jnp.mean(of * of, -1, keepdims=True) + RMS_EPS) * g.astype(jnp.float32)
    return (of * (1.0 - lam_init)).astype(o.dtype)


def pool_mix(hist, xnew, pos_new, w_pool, pool_scale):
    n_hist, t = hist.shape[1], xnew.shape[1]
    xs = jnp.concatenate([hist, xnew], axis=1).astype(jnp.float32)
    cs = jnp.pad(jnp.cumsum(xs, axis=1), ((0, 0), (1, 0), (0, 0)))
    end = n_hist + jnp.arange(t) + 1
    outs = []
    for g, w in enumerate(POOL_WINDOWS):
        sl = slice(g * POOL_GROUP, (g + 1) * POOL_GROUP)
        csg = cs[..., sl]
        start = jnp.maximum(end - w, 0)
        win_sum = csg[:, end] - csg[:, start]
        count = jnp.minimum(w, pos_new + 1).astype(jnp.float32)[None, :, None]
        mixed = win_sum / count - xnew[..., sl].astype(jnp.float32)
        outs.append(jnp.einsum('ntc,cd->ntd', mixed.astype(xnew.dtype), w_pool[g]))
    return jnp.concatenate(outs, axis=-1) * pool_scale


def modulation(c, w_ada, b_ada):
    m = jnp.einsum('nd,de->ne', jax.nn.silu(c), w_ada) + b_ada
    return jnp.split(m[:, None, :], 6, axis=-1)


def mixer_sublayer(x, shift, scale, gate, attend, pool_hist, pos, w_in, subln_g, lam_init,
                   w_pool, pool_scale, w_o, ln_g, ln_b):
    n, t, _ = x.shape
    u = x * (1 + scale) + shift
    qkvp = jnp.einsum('ntd,de->nte', u, w_in)
    q = qkvp[..., :ATTN_WIDTH].reshape(n, t, N_HEADS, 2, HEAD_DIM)
    k = qkvp[..., ATTN_WIDTH:2 * ATTN_WIDTH].reshape(n, t, N_HEADS, 2, HEAD_DIM)
    v = qkvp[..., 2 * ATTN_WIDTH:3 * ATTN_WIDTH].reshape(n, t, N_HEADS, V_DIM)
    p = qkvp[..., 3 * ATTN_WIDTH:]
    o = diff_head_norm(attend(q, k, v), subln_g, lam_init).reshape(n, t, ATTN_WIDTH)
    pm = pool_mix(pool_hist, p, pos, w_pool, pool_scale)
    h = jnp.einsum('nte,ed->ntd', jnp.concatenate([o, pm], axis=-1), w_o)
    y = layer_norm(ALPHA * x + gate * h, ln_g, ln_b)
    new_pool = jnp.concatenate([pool_hist, p], axis=1)[:, -POOL_HIST:]
    return y, k.reshape(n, t, N_HEADS, 2 * HEAD_DIM), v, new_pool


def moe_sublayer(x, shift, scale, gate, w_rg, b_rg, w_re, b_re, w_gate, w_up, w_down, ln_g, ln_b):
    n, t, d = x.shape
    tok = (x * (1 + scale) + shift).reshape(n * t, d)
    g_logits = jnp.einsum('md,dg->mg', tok, w_rg).astype(jnp.float32) + b_rg.astype(jnp.float32)
    g_prob = jax.nn.softmax(g_logits, axis=-1)
    g_w, g_idx = lax.top_k(g_prob, 1)
    e_logits = jnp.einsum('md,gde->mge', tok, w_re).astype(jnp.float32) + b_re.astype(jnp.float32)
    e_sel = jnp.take_along_axis(e_logits, g_idx[:, :, None], axis=1)[:, 0]
    e_val, e_idx = lax.top_k(e_sel, EXP_TOP_K)
    e_w = jax.nn.softmax(e_val, axis=-1)
    within = jnp.sum(jax.nn.one_hot(e_idx, EXP_PER_GROUP) * e_w[..., None], axis=1)
    combine = (jax.nn.one_hot(g_idx[:, 0], N_EXP_GROUPS)[:, :, None] * within[:, None, :]
               * g_w[:, :, None]).astype(tok.dtype)
    out = jnp.zeros_like(tok)
    for g in range(N_EXP_GROUPS):
        hg = jax.nn.silu(jnp.einsum('md,edf->mef', tok, w_gate[g])) * jnp.einsum('md,edf->mef', tok, w_up[g])
        out = out + jnp.einsum('mef,efd->md', hg * combine[:, g, :, None], w_down[g])
    return layer_norm(ALPHA * x + gate * out.reshape(n, t, d), ln_g, ln_b)


def setup_inputs(seed: int = 0) -> dict:
    key = jax.random.key(seed)
    ks = jax.random.split(key, 32)
    f32 = jnp.float32
    n_pages = PAST_LEN // PAGE_SIZE
    n_used = DEC_BATCH * n_pages
    n_phys = n_used + max(n_used // 4, 1)

    def nrm(k, shape, s):
        return jax.random.normal(k, shape, f32) * s

    page_table = jax.random.permutation(ks[7], n_phys)[:n_used].reshape(DEC_BATCH, n_pages).astype(jnp.int32)
    cols = jnp.arange(IN_WIDTH)
    v_cols = (cols >= 2 * ATTN_WIDTH) & (cols < 3 * ATTN_WIDTH)
    w_in = nrm(ks[10], (DEPTH, D_MODEL, IN_WIDTH), D_MODEL ** -0.5) * jnp.where(v_cols, BETA, 1.0).astype(f32)
    return {
        'x_prompt': nrm(ks[0], (BATCH, SEQ, D_MODEL), 1.0),
        'x_sample': nrm(ks[1], (DEC_BATCH, DEC_SEQ, D_MODEL), 1.0),
        'c_prompt': nrm(ks[2], (BATCH, D_MODEL), 1.0),
        'c_sample': nrm(ks[3], (DEC_BATCH, D_MODEL), 1.0),
        'cache_k': nrm(ks[4], (DEPTH, n_phys, PAGE_SIZE, N_HEADS, 2 * HEAD_DIM), 1.0),
        'cache_v': nrm(ks[5], (DEPTH, n_phys, PAGE_SIZE, N_HEADS, V_DIM), 1.0),
        'state_pool': nrm(ks[6], (DEPTH, DEC_BATCH, POOL_HIST, POOL_WIDTH), 1.0),
        'page_table': page_table,
        'rel_bias': nrm(ks[8], (N_BUCKETS, N_HEADS), 0.5),
        'w_ada': nrm(ks[9], (DEPTH, D_MODEL, 6 * D_MODEL), 0.5 * D_MODEL ** -0.5),
        'b_ada': nrm(ks[11], (DEPTH, 6 * D_MODEL), 0.02),
        'w_in': w_in,
        'lambda_q1': nrm(ks[12], (DEPTH, HEAD_DIM), 0.1),
        'lambda_k1': nrm(ks[13], (DEPTH, HEAD_DIM), 0.1),
        'lambda_q2': nrm(ks[14], (DEPTH, HEAD_DIM), 0.1),
        'lambda_k2': nrm(ks[15], (DEPTH, HEAD_DIM), 0.1),
        'subln_g': 1.0 + nrm(ks[16], (DEPTH, V_DIM), 0.05),
        'w_pool': nrm(ks[17], (DEPTH, N_POOL_GROUPS, POOL_GROUP, POOL_GROUP), POOL_GROUP ** -0.5),
        'pool_scale': 1.0 + nrm(ks[18], (DEPTH, POOL_WIDTH), 0.1),
        'w_o': nrm(ks[19], (DEPTH, MIX_WIDTH, D_MODEL), BETA * MIX_WIDTH ** -0.5),
        'ln1_g': 1.0 + nrm(ks[20], (DEPTH, D_MODEL), 0.05),
        'ln1_b': nrm(ks[21], (DEPTH, D_MODEL), 0.02),
        'w_router_group': nrm(ks[22], (DEPTH, D_MODEL, N_EXP_GROUPS), D_MODEL ** -0.5),
        'b_router_group': nrm(ks[23], (DEPTH, N_EXP_GROUPS), 0.01),
        'w_router_expert': nrm(ks[24], (DEPTH, N_EXP_GROUPS, D_MODEL, EXP_PER_GROUP), D_MODEL ** -0.5),
        'b_router_expert': nrm(ks[25], (DEPTH, N_EXP_GROUPS, EXP_PER_GROUP), 0.01),
        'w_gate': nrm(ks[26], (DEPTH, N_EXP_GROUPS, EXP_PER_GROUP, D_MODEL, D_EXPERT), BETA * D_MODEL ** -0.5),
        'w_up': nrm(ks[27], (DEPTH, N_EXP_GROUPS, EXP_PER_GROUP, D_MODEL, D_EXPERT), BETA * D_MODEL ** -0.5),
        'w_down': nrm(ks[28], (DEPTH, N_EXP_GROUPS, EXP_PER_GROUP, D_EXPERT, D_MODEL), BETA * D_EXPERT ** -0.5),
        'ln2_g': 1.0 + nrm(ks[29], (DEPTH, D_MODEL), 0.05),
        'ln2_b': nrm(ks[30], (DEPTH, D_MODEL), 0.02),
    }


def reference(x_prompt, x_sample, c_prompt, c_sample, cache_k, cache_v, state_pool, page_table,
              rel_bias, w_ada, b_ada, w_in, lambda_q1, lambda_k1, lambda_q2, lambda_k2, subln_g,
              w_pool, pool_scale, w_o, ln1_g, ln1_b, w_router_group, b_router_group,
              w_router_expert, b_router_expert, w_gate, w_up, w_down, ln2_g, ln2_b):
    n_p, s_p = x_prompt.shape[0], x_prompt.shape[1]
    t_s = x_sample.shape[1]
    past_len = page_table.shape[1] * cache_k.shape[2]
    pos_prompt = jnp.arange(s_p)
    pos_sample = past_len + jnp.arange(t_s)
    empty_hist = jnp.zeros((n_p, 0, POOL_WIDTH), x_prompt.dtype)

    xp, xs = x_prompt, x_sample
    kp_l, vp_l, pp_l, ks_l, vs_l, ps_l = [], [], [], [], [], []
    for l in range(DEPTH):
        lam_init = 0.8 - 0.6 * math.exp(-0.3 * l)
        lam = (jnp.exp(jnp.sum(lambda_q1[l].astype(jnp.float32) * lambda_k1[l].astype(jnp.float32)))
               - jnp.exp(jnp.sum(lambda_q2[l].astype(jnp.float32) * lambda_k2[l].astype(jnp.float32)))
               + lam_init)
        mp = modulation(c_prompt, w_ada[l], b_ada[l])
        ms = modulation(c_sample, w_ada[l], b_ada[l])

        def attend_prompt(q, k, v):
            return prompt_attention(q, k, v, lam, rel_bias)

        def attend_sample(q, k, v, l=l):
            return sample_attention(q, k, v, cache_k[l], cache_v[l], page_table, lam, rel_bias)

        xp, kp, vp, pp = mixer_sublayer(xp, mp[0], mp[1], mp[2], attend_prompt, empty_hist, pos_prompt,
                                        w_in[l], subln_g[l], lam_init, w_pool[l], pool_scale[l], w_o[l],
                                        ln1_g[l], ln1_b[l])
        xs, ks_, vs_, ps_ = mixer_sublayer(xs, ms[0], ms[1], ms[2], attend_sample, state_pool[l], pos_sample,
                                           w_in[l], subln_g[l], lam_init, w_pool[l], pool_scale[l], w_o[l],
                                           ln1_g[l], ln1_b[l])
        xp = moe_sublayer(xp, mp[3], mp[4], mp[5], w_router_group[l], b_router_group[l], w_router_expert[l],
                          b_router_expert[l], w_gate[l], w_up[l], w_down[l], ln2_g[l], ln2_b[l])
        xs = moe_sublayer(xs, ms[3], ms[4], ms[5], w_router_group[l], b_router_group[l], w_router_expert[l],
                          b_router_expert[l], w_gate[l], w_up[l], w_down[l], ln2_g[l], ln2_b[l])
        kp_l.append(kp); vp_l.append(vp); pp_l.append(pp)
        ks_l.append(ks_); vs_l.append(vs_); ps_l.append(ps_)

    y_prompt, y_sample = xp, xs
    k_prompt, v_prompt, pool_prompt = jnp.stack(kp_l), jnp.stack(vp_l), jnp.stack(pp_l)
    k_sample, v_sample, pool_sample = jnp.stack(ks_l), jnp.stack(vs_l), jnp.stack(ps_l)
    return (y_prompt, y_sample, k_prompt, v_prompt, pool_prompt, k_sample, v_sample, pool_sample)
```

```python
import functools
import math

import numpy as np
import jax
import jax.numpy as jnp
from jax import lax
from jax.experimental import pallas as pl
from jax.experimental.pallas import tpu as pltpu

F32 = jnp.float32
BF16 = jnp.bfloat16

N_HEADS = 4
HEAD_DIM = 64
V_DIM = 2 * HEAD_DIM
POOL_WINDOWS = (2, 4, 8, 16)
POOL_GROUP = 128
POOL_HIST = max(POOL_WINDOWS) - 1
N_BUCKETS = 32
MAX_DISTANCE = 128
N_EXP_GROUPS = 4
EXP_PER_GROUP = 4
N_EXPERTS = N_EXP_GROUPS * EXP_PER_GROUP
EXPERT_PAIRS = ((0, 1), (0, 2), (0, 3), (1, 2), (1, 3), (2, 3))
N_PAIR_BUCKETS = N_EXP_GROUPS * len(EXPERT_PAIRS)
LN_EPS = 1e-5
RMS_EPS = 1e-5

LANES = 128
POOL_HALO = 16
VMEM_LIMIT_BYTES = 48 * 1024 * 1024
SEQ_TILE = 512
ATTN_TILE = 256
EXPERT_TILE = 256
PAGES_PER_STEP = 8
ROUTE_EXTRA = LANES


def _params(*sem):
    return pltpu.CompilerParams(dimension_semantics=sem, vmem_limit_bytes=VMEM_LIMIT_BYTES)


def _dot(a, b):
    return jnp.dot(a, b, preferred_element_type=F32)


def _dot_nt(a, b):
    return lax.dot_general(a, b, (((1,), (1,)), ((), ())), preferred_element_type=F32)


def _sigmoid(x):
    return 1.0 / (1.0 + jnp.exp(-x))


def _mod_kernel(c_ref, w_ref, b_ref, o_ref):
    c = c_ref[...]
    s = c * _sigmoid(c)
    s_hi = s.astype(BF16)
    s_lo = (s - s_hi.astype(F32)).astype(BF16)
    w = w_ref[...]
    w_hi = w.astype(BF16)
    w_lo = (w - w_hi.astype(F32)).astype(BF16)
    o_ref[...] = _dot(s_hi, w_hi) + _dot(s_lo, w_hi) + _dot(s_hi, w_lo) + b_ref[...]


def _modulation(c_all, w_ada, b_ada):
    n, d = c_all.shape
    e = w_ada.shape[1]
    bn = 1024
    return pl.pallas_call(
        _mod_kernel,
        grid=(e // bn,),
        in_specs=[pl.BlockSpec((n, d), lambda j: (0, 0)),
                  pl.BlockSpec((d, bn), lambda j: (0, j)),
                  pl.BlockSpec((1, bn), lambda j: (0, j))],
        out_specs=pl.BlockSpec((n, bn), lambda j: (0, j)),
        out_shape=jax.ShapeDtypeStruct((n, e), F32),
        compiler_params=_params("arbitrary"),
        name="modulation",
    )(c_all, w_ada, b_ada.reshape(1, e))


def _pool_mixed(ext, g, w, inv_cnt, rows):
    eg = ext[:, g * POOL_GROUP:(g + 1) * POOL_GROUP]
    s = eg
    step = 1
    while step < w:
        s = s + pltpu.roll(s, step, 0)
        step *= 2
    return s[POOL_HALO:POOL_HALO + rows] * inv_cnt - eg[POOL_HALO:POOL_HALO + rows]


def _inproj_kernel(x_ref, mod_ref, win_ref, wpool_ref, pscale_ref,
                   q_ref, k_ref, v_ref, kb_ref, vb_ref, pm_ref, ph_ref, ext_ref, *, ts, aw):
    i = pl.program_id(1)

    @pl.when(i == 0)
    def _():
        ext_ref[0:POOL_HALO, :] = jnp.zeros((POOL_HALO, ext_ref.shape[1]), F32)

    @pl.when(i > 0)
    def _():
        ext_ref[0:POOL_HALO, :] = ext_ref[ts:ts + POOL_HALO, :]

    shift = mod_ref[0, 0:1, :]
    scale = mod_ref[0, 1:2, :]
    u = (x_ref[0] * (1.0 + scale) + shift).astype(BF16)
    acc = _dot(u, win_ref[...])
    q_ref[0] = (acc[:, 0:aw] * (HEAD_DIM ** -0.5)).astype(BF16)
    k = acc[:, aw:2 * aw]
    v = acc[:, 2 * aw:3 * aw]
    k_ref[0] = k
    v_ref[0] = v
    kb_ref[0] = k.astype(BF16)
    vb_ref[0] = v.astype(BF16)
    p = acc[:, 3 * aw:]
    ext_ref[POOL_HALO:POOL_HALO + ts, :] = p
    ext = ext_ref[...]
    pos = i * ts + lax.broadcasted_iota(jnp.int32, (ts, 1), 0)
    for g, w in enumerate(POOL_WINDOWS):
        inv_cnt = 1.0 / jnp.minimum(w, pos + 1).astype(F32)
        mixed = _pool_mixed(ext, g, w, inv_cnt, ts)
        cols = slice(g * POOL_GROUP, (g + 1) * POOL_GROUP)
        pm_ref[0, :, cols] = (_dot(mixed.astype(BF16), wpool_ref[g]) * pscale_ref[:, cols]).astype(BF16)

    @pl.when(i == pl.num_programs(1) - 1)
    def _():
        ph_ref[0] = p[ts - POOL_HIST:ts, :]


def _prompt_inproj(x, mod, win_b, wpool_b, pscale):
    nb, s, d = x.shape
    ew = win_b.shape[1]
    aw = (ew - 4 * POOL_GROUP) // 3
    pw = ew - 3 * aw
    ts = min(SEQ_TILE, s)
    nt = s // ts
    tile = lambda b, i: (b, i, 0)
    const2 = lambda b, i: (0, 0)
    return pl.pallas_call(
        functools.partial(_inproj_kernel, ts=ts, aw=aw),
        grid=(nb, nt),
        in_specs=[pl.BlockSpec((1, ts, d), tile),
                  pl.BlockSpec((1, 6, d), lambda b, i: (b, 0, 0)),
                  pl.BlockSpec((d, ew), const2),
                  pl.BlockSpec(wpool_b.shape, lambda b, i: (0, 0, 0)),
                  pl.BlockSpec((1, pw), const2)],
        out_specs=[pl.BlockSpec((1, ts, aw), tile)] * 5 + [
            pl.BlockSpec((1, ts, pw), tile),
            pl.BlockSpec((1, POOL_HIST, pw), lambda b, i: (b, 0, 0))],
        out_shape=[jax.ShapeDtypeStruct((nb, s, aw), BF16),
                   jax.ShapeDtypeStruct((nb, s, aw), F32),
                   jax.ShapeDtypeStruct((nb, s, aw), F32),
                   jax.ShapeDtypeStruct((nb, s, aw), BF16),
                   jax.ShapeDtypeStruct((nb, s, aw), BF16),
                   jax.ShapeDtypeStruct((nb, s, pw), BF16),
                   jax.ShapeDtypeStruct((nb, POOL_HIST, pw), F32)],
        scratch_shapes=[pltpu.VMEM((ts + POOL_HALO, pw), F32)],
        compiler_params=_params("arbitrary", "arbitrary"),
        name="prompt_inproj",
    )(x, mod, win_b, wpool_b, pscale)


def _split_branches(qh):
    lane = lax.broadcasted_iota(jnp.int32, qh.shape, 1)
    zero = jnp.zeros_like(qh)
    return jnp.concatenate([jnp.where(lane < HEAD_DIM, qh, zero), jnp.where(lane >= HEAD_DIM, qh, zero)], axis=0)


def _online_update(s, v_tile, m_ref, l_ref, acc_ref, first):
    m_new = jnp.max(s, axis=1, keepdims=True)
    if not first:
        m_old = m_ref[...]
        m_new = jnp.maximum(m_old, m_new)
    p = jnp.exp(s - m_new)
    pv = _dot(p.astype(BF16), v_tile)
    l_new = jnp.sum(p, axis=1, keepdims=True)
    if first:
        acc_ref[...] = pv
        l_ref[...] = l_new
    else:
        alpha = jnp.exp(m_old - m_new)
        acc_ref[...] = alpha * acc_ref[...] + pv
        l_ref[...] = alpha * l_ref[...] + l_new
    m_ref[...] = m_new


def _diff_head_out(acc, l, t, lam, g, lam_init):
    o = acc[0:t] / l[0:t] - lam * (acc[t:2 * t] / l[t:2 * t])
    o = o * lax.rsqrt(jnp.mean(o * o, axis=-1, keepdims=True) + RMS_EPS) * g
    return o * (1.0 - lam_init)


def _attn_kernel(cfar_ref, lam_ref, q_ref, k_ref, v_ref, bd_ref, be_ref, g_ref, o_ref,
                 m_sc, l_sc, acc_sc, *, tq, lam_init):
    qi = pl.program_id(1)
    lam = lam_ref[0]
    for h in range(N_HEADS):
        cols = slice(h * V_DIM, (h + 1) * V_DIM)
        qs = _split_branches(q_ref[0, :, cols])

        def tile_scores(j, qs=qs, cols=cols):
            return _dot_nt(qs, k_ref[0, pl.ds(pl.multiple_of(j * tq, tq), tq), cols])

        def v_tile(j, cols=cols):
            return v_ref[0, pl.ds(pl.multiple_of(j * tq, tq), tq), cols]

        _online_update(tile_scores(qi) + bd_ref[h], v_tile(qi), m_sc, l_sc, acc_sc, first=True)
        js = jnp.maximum(qi - 1, 0)
        s = jnp.where(qi >= 1, tile_scores(js) + be_ref[h], -jnp.inf)
        _online_update(s, v_tile(js), m_sc, l_sc, acc_sc, first=False)

        def far(j, carry, h=h, tile_scores=tile_scores, v_tile=v_tile):
            _online_update(tile_scores(j) + cfar_ref[h], v_tile(j), m_sc, l_sc, acc_sc, first=False)
            return carry

        lax.fori_loop(0, jnp.maximum(qi - 1, 0), far, 0)
        o = _diff_head_out(acc_sc[...], l_sc[...], tq, lam, g_ref[...], lam_init)
        o_ref[0, :, cols] = o.astype(BF16)


def _prompt_attention(q, kb, vb, bias_diag, bias_sub, c_far, lam, subln_g, lam_init):
    nb, s, aw = q.shape
    tq = bias_diag.shape[2]
    smem = pl.BlockSpec(memory_space=pltpu.SMEM)
    const3 = lambda b, i: (0, 0, 0)
    return pl.pallas_call(
        functools.partial(_attn_kernel, tq=tq, lam_init=lam_init),
        grid=(nb, s // tq),
        in_specs=[smem, smem,
                  pl.BlockSpec((1, tq, aw), lambda b, i: (b, i, 0)),
                  pl.BlockSpec((1, s, aw), lambda b, i: (b, 0, 0)),
                  pl.BlockSpec((1, s, aw), lambda b, i: (b, 0, 0)),
                  pl.BlockSpec(bias_diag.shape, const3),
                  pl.BlockSpec(bias_sub.shape, const3),
                  pl.BlockSpec((1, V_DIM), lambda b, i: (0, 0))],
        out_specs=pl.BlockSpec((1, tq, aw), lambda b, i: (b, i, 0)),
        out_shape=jax.ShapeDtypeStruct((nb, s, aw), BF16),
        scratch_shapes=[pltpu.VMEM((2 * tq, 1), F32), pltpu.VMEM((2 * tq, 1), F32),
                        pltpu.VMEM((2 * tq, V_DIM), F32)],
        compiler_params=_params("arbitrary", "arbitrary"),
        name="prompt_attention",
    )(c_far, lam, q, kb, vb, bias_diag, bias_sub, subln_g)


def _layer_norm(z, g, b):
    mu = jnp.mean(z, axis=-1, keepdims=True)
    zc = z - mu
    var = jnp.mean(zc * zc, axis=-1, keepdims=True)
    return zc * lax.rsqrt(var + LN_EPS) * g + b


def _mod_get(mod_ref, j, per_row):
    return mod_ref[j] if per_row else mod_ref[0, j:j + 1, :]


def _outproj_router_kernel(o_ref, pm_ref, x_ref, mod_ref, wo_ref, lng_ref, lnb_ref, wr_ref, br_ref,
                           y_ref, tokx_ref, route_ref, cnt_ref, carry_sc, *, alpha, per_row):
    t = x_ref.shape[1]
    d = x_ref.shape[2]
    aw = o_ref.shape[2]

    @pl.when((pl.program_id(0) == 0) & (pl.program_id(1) == 0))
    def _():
        carry_sc[...] = jnp.zeros(carry_sc.shape, F32)

    h = _dot(o_ref[0], wo_ref[0:aw, :]) + _dot(pm_ref[0], wo_ref[aw:, :])
    gate1 = _mod_get(mod_ref, 2, per_row)
    y1 = _layer_norm(alpha * x_ref[0] + gate1 * h, lng_ref[...], lnb_ref[...])
    y_ref[0] = y1
    tok = y1 * (1.0 + _mod_get(mod_ref, 4, per_row)) + _mod_get(mod_ref, 3, per_row)
    tokx_ref[0, :, 0:d] = tok

    logits = _dot(tok.astype(BF16), wr_ref[...]) + br_ref[...]
    lane = lax.broadcasted_iota(jnp.int32, logits.shape, 1)
    lane_f = lane.astype(F32)
    neg = jnp.full_like(logits, -jnp.inf)

    def first_argmax(vals, vmax):
        return jnp.min(jnp.where(vals == vmax, lane_f, float(LANES)), axis=1, keepdims=True).astype(jnp.int32)

    gl = jnp.where(lane < N_EXP_GROUPS, logits, neg)
    gmax = jnp.max(gl, axis=1, keepdims=True)
    gidx = first_argmax(gl, gmax)
    g_w = 1.0 / jnp.sum(jnp.exp(gl - gmax), axis=1, keepdims=True)
    lo_lane = N_EXP_GROUPS + EXP_PER_GROUP * gidx
    el = jnp.where((lane >= lo_lane) & (lane < lo_lane + EXP_PER_GROUP), logits, neg)
    v1 = jnp.max(el, axis=1, keepdims=True)
    i1 = first_argmax(el, v1)
    el2 = jnp.where(lane == i1, neg, el)
    v2 = jnp.max(el2, axis=1, keepdims=True)
    i2 = first_argmax(el2, v2)
    e21 = jnp.exp(v2 - v1)
    w1 = g_w / (1.0 + e21)
    w2 = g_w * e21 / (1.0 + e21)
    first_lo = i1 < i2
    cw_lo = jnp.where(first_lo, w1, w2)
    cw_hi = jnp.where(first_lo, w2, w1)
    a = jnp.minimum(i1, i2) - lo_lane
    b = jnp.maximum(i1, i2) - lo_lane
    pair_base = jnp.where(a == 0, 0, jnp.where(a == 1, 3, 5))
    bucket = gidx * len(EXPERT_PAIRS) + pair_base + (b - a - 1)

    xlane = lax.broadcasted_iota(jnp.int32, (t, ROUTE_EXTRA), 1)
    tokx_ref[0, :, d:] = jnp.where(xlane == 0, cw_lo, jnp.where(xlane == 1, cw_hi, 0.0))

    onehot = lane == bucket
    row = lax.broadcasted_iota(jnp.int32, (t, t), 0)
    col = lax.broadcasted_iota(jnp.int32, (t, t), 1)
    ltri = jnp.where(col < row, 1.0, 0.0).astype(BF16)
    prefix = _dot(ltri, jnp.where(onehot, 1.0, 0.0).astype(BF16)) + carry_sc[...]
    rank = jnp.sum(jnp.where(onehot, prefix, 0.0), axis=1, keepdims=True)
    carry_sc[...] = carry_sc[...] + jnp.sum(jnp.where(onehot, 1.0, 0.0), axis=0, keepdims=True)
    cnt_ref[...] = carry_sc[...]
    route_ref[0] = jnp.where(lane == 0, bucket.astype(F32), jnp.where(lane == 1, rank, 0.0))


def _outproj_router(o, pm, x, mod, wo_b, ln_g, ln_b, wr_b, br, *, alpha, per_row):
    nb, s, d = x.shape
    aw = o.shape[2]
    pw = pm.shape[2]
    t = min(SEQ_TILE, s)
    tile = lambda b, i: (b, i, 0)
    const2 = lambda b, i: (0, 0)
    mod_spec = (pl.BlockSpec(mod.shape, lambda b, i: (0, 0, 0)) if per_row
                else pl.BlockSpec((1, 6, d), lambda b, i: (b, 0, 0)))
    return pl.pallas_call(
        functools.partial(_outproj_router_kernel, alpha=alpha, per_row=per_row),
        grid=(nb, s // t),
        in_specs=[pl.BlockSpec((1, t, aw), tile), pl.BlockSpec((1, t, pw), tile), pl.BlockSpec((1, t, d), tile),
                  mod_spec,
                  pl.BlockSpec(wo_b.shape, const2), pl.BlockSpec((1, d), const2), pl.BlockSpec((1, d), const2),
                  pl.BlockSpec(wr_b.shape, const2), pl.BlockSpec((1, LANES), const2)],
        out_specs=[pl.BlockSpec((1, t, d), tile), pl.BlockSpec((1, t, d + ROUTE_EXTRA), tile),
                   pl.BlockSpec((1, t, LANES), tile), pl.BlockSpec((1, LANES), const2)],
        out_shape=[jax.ShapeDtypeStruct((nb, s, d), F32), jax.ShapeDtypeStruct((nb, s, d + ROUTE_EXTRA), F32),
                   jax.ShapeDtypeStruct((nb, s, LANES), F32), jax.ShapeDtypeStruct((1, LANES), F32)],
        scratch_shapes=[pltpu.VMEM((1, LANES), F32)],
        compiler_params=_params("arbitrary", "arbitrary"),
        name="outproj_router",
    )(o, pm, x, mod, wo_b, ln_g, ln_b, wr_b, br)


def _row_copy(src, dst, sem):
    return pltpu.make_async_copy(src, dst, sem)


def _dispatch_kernel(pos_ref, tok_ref, xs_in_ref, xs_ref, sem, *, t):
    del xs_in_ref
    base = pl.program_id(0) * t

    def issue(r, carry):
        _row_copy(tok_ref.at[pl.ds(r, 1)], xs_ref.at[pl.ds(pos_ref[base + r], 1)], sem).start()
        return carry

    lax.fori_loop(0, t, issue, 0, unroll=8)

    def drain(r, carry):
        _row_copy(tok_ref.at[pl.ds(0, 1)], xs_ref.at[pl.ds(0, 1)], sem).wait()
        return carry

    lax.fori_loop(0, t, drain, 0, unroll=8)


def _dispatch(pos, tokx, xs):
    n, w = tokx.shape
    t = min(SEQ_TILE, n)
    return pl.pallas_call(
        functools.partial(_dispatch_kernel, t=t),
        grid_spec=pltpu.PrefetchScalarGridSpec(
            num_scalar_prefetch=1,
            grid=(n // t,),
            in_specs=[pl.BlockSpec((t, w), lambda i, pos: (i, 0)),
                      pl.BlockSpec(memory_space=pl.ANY)],
            out_specs=pl.BlockSpec(memory_space=pl.ANY),
            scratch_shapes=[pltpu.SemaphoreType.DMA(())]),
        out_shape=jax.ShapeDtypeStruct(xs.shape, xs.dtype),
        input_output_aliases={2: 0},
        compiler_params=_params("arbitrary"),
        name="dispatch_rows",
    )(pos, tokx, xs)


def _expert_kernel(ea_ref, eb_ref, valid_ref, xs_ref, wga_ref, wua_ref, wda_ref, wgb_ref, wub_ref, wdb_ref,
                   o_ref, *, d):
    del ea_ref, eb_ref
    ti = pl.program_id(0)

    @pl.when(valid_ref[ti] == 1)
    def _():
        xs = xs_ref[...]
        x = xs[:, 0:d].astype(BF16)

        def ffn(wg_ref, wu_ref, wd_ref, cw):
            g = _dot(x, wg_ref[0])
            u = _dot(x, wu_ref[0])
            hid = (g * _sigmoid(g)) * u * cw
            return _dot(hid.astype(BF16), wd_ref[0])

        o_ref[...] = (ffn(wga_ref, wua_ref, wda_ref, xs[:, d:d + 1])
                      + ffn(wgb_ref, wub_ref, wdb_ref, xs[:, d + 1:d + 2]))

    @pl.when(valid_ref[ti] == 0)
    def _():
        o_ref[...] = jnp.zeros(o_ref.shape, F32)


def _experts(tile_ea, tile_eb, tile_valid, xs, wg_b, wu_b, wd_b):
    r, w = xs.shape
    d = w - ROUTE_EXTRA
    f = wg_b.shape[2]
    tm = EXPERT_TILE
    wa = lambda blk: pl.BlockSpec(blk, lambda i, ea, eb, va: (ea[i], 0, 0))
    wb = lambda blk: pl.BlockSpec(blk, lambda i, ea, eb, va: (eb[i], 0, 0))
    return pl.pallas_call(
        functools.partial(_expert_kernel, d=d),
        grid_spec=pltpu.PrefetchScalarGridSpec(
            num_scalar_prefetch=3,
            grid=(r // tm,),
            in_specs=[pl.BlockSpec((tm, w), lambda i, ea, eb, va: (i, 0)),
                      wa((1, d, f)), wa((1, d, f)), wa((1, f, d)),
                      wb((1, d, f)), wb((1, d, f)), wb((1, f, d))],
            out_specs=pl.BlockSpec((tm, d), lambda i, ea, eb, va: (i, 0))),
        out_shape=jax.ShapeDtypeStruct((r, d), F32),
        compiler_params=_params("arbitrary"),
        name="expert_ffn",
    )(tile_ea, tile_eb, tile_valid, xs, wg_b, wu_b, wd_b, wg_b, wu_b, wd_b)


def _combine_kernel(pos_ref, y_ref, mod_ref, lng_ref, lnb_ref, os_ref, out_ref, buf, sem, *, alpha, per_row):
    t = y_ref.shape[1]
    base = (pl.program_id(0) * pl.num_programs(1) + pl.program_id(1)) * t

    def issue(r, carry):
        _row_copy(os_ref.at[pl.ds(pos_ref[base + r], 1)], buf.at[pl.ds(r, 1)], sem).start()
        return carry

    lax.fori_loop(0, t, issue, 0, unroll=8)

    def drain(r, carry):
        _row_copy(os_ref.at[pl.ds(0, 1)], buf.at[pl.ds(0, 1)], sem).wait()
        return carry

    lax.fori_loop(0, t, drain, 0, unroll=8)
    gate2 = _mod_get(mod_ref, 5, per_row)
    out_ref[0] = _layer_norm(alpha * y_ref[0] + gate2 * buf[...], lng_ref[...], lnb_ref[...])


def _combine(pos, y1, mod, ln_g, ln_b, o_sorted, *, alpha, per_row):
    nb, s, d = y1.shape
    t = min(SEQ_TILE, s)
    tile = lambda b, i, pos: (b, i, 0)
    const2 = lambda b, i, pos: (0, 0)
    mod_spec = (pl.BlockSpec(mod.shape, lambda b, i, pos: (0, 0, 0)) if per_row
                else pl.BlockSpec((1, 6, d), lambda b, i, pos: (b, 0, 0)))
    return pl.pallas_call(
        functools.partial(_combine_kernel, alpha=alpha, per_row=per_row),
        grid_spec=pltpu.PrefetchScalarGridSpec(
            num_scalar_prefetch=1,
            grid=(nb, s // t),
            in_specs=[pl.BlockSpec((1, t, d), tile), mod_spec,
                      pl.BlockSpec((1, d), const2), pl.BlockSpec((1, d), const2),
                      pl.BlockSpec(memory_space=pl.ANY)],
            out_specs=pl.BlockSpec((1, t, d), tile),
            scratch_shapes=[pltpu.VMEM((t, d), F32), pltpu.SemaphoreType.DMA(())]),
        out_shape=jax.ShapeDtypeStruct((nb, s, d), F32),
        compiler_params=_params("arbitrary", "arbitrary"),
        name="combine_rows",
    )(pos, y1, mod, ln_g, ln_b, o_sorted)


def _sample_inproj_kernel(x_ref, shift_ref, scale_ref, win_ref, hist_ref, wpool_ref, pscale_ref,
                          q_ref, k_ref, v_ref, p_ref, pm_ref, *, aw, n_seq, n_new, past_len):
    u = (x_ref[...] * (1.0 + scale_ref[...]) + shift_ref[...]).astype(BF16)
    acc = _dot(u, win_ref[...])
    q_ref[...] = (acc[:, 0:aw] * (HEAD_DIM ** -0.5)).astype(BF16)
    k_ref[...] = acc[:, aw:2 * aw]
    v_ref[...] = acc[:, 2 * aw:3 * aw]
    p = acc[:, 3 * aw:]
    p_ref[...] = p
    n_hist = hist_ref.shape[0]
    rows = [hist_ref[j] for j in range(n_hist)] + [p[t * n_seq:(t + 1) * n_seq] for t in range(n_new)]
    for g, w in enumerate(POOL_WINDOWS):
        cols = slice(g * POOL_GROUP, (g + 1) * POOL_GROUP)
        mixed = []
        for t in range(n_new):
            end = n_hist + t + 1
            start = max(end - w, 0)
            win_sum = rows[start][:, cols]
            for j in range(start + 1, end):
                win_sum = win_sum + rows[j][:, cols]
            count = float(min(w, past_len + t + 1))
            mixed.append(win_sum / count - rows[n_hist + t][:, cols])
        mixed = jnp.concatenate(mixed, axis=0)
        pm_ref[:, cols] = (_dot(mixed.astype(BF16), wpool_ref[g]) * pscale_ref[:, cols]).astype(BF16)


def _sample_inproj(x, shift, scale, win_b, hist_t, wpool_b, pscale, *, n_seq, n_new, past_len):
    n, d = x.shape
    ew = win_b.shape[1]
    aw = (ew - 4 * POOL_GROUP) // 3
    pw = ew - 3 * aw
    return pl.pallas_call(
        functools.partial(_sample_inproj_kernel, aw=aw, n_seq=n_seq, n_new=n_new, past_len=past_len),
        out_shape=[jax.ShapeDtypeStruct((n, aw), BF16), jax.ShapeDtypeStruct((n, aw), F32),
                   jax.ShapeDtypeStruct((n, aw), F32), jax.ShapeDtypeStruct((n, pw), F32),
                   jax.ShapeDtypeStruct((n, pw), BF16)],
        compiler_params=pltpu.CompilerParams(vmem_limit_bytes=VMEM_LIMIT_BYTES),
        name="sample_inproj",
    )(x, shift, scale, win_b, hist_t, wpool_b, pscale)


def _paged_attn_kernel(pt_ref, lam_ref, q_ref, kn_ref, vn_ref, bp_ref, bn_ref, g_ref, *rest,
                       pps, page, n_new, lam_init):
    del pt_ref
    ck = rest[0:pps]
    cv = rest[pps:2 * pps]
    o_ref = rest[2 * pps]
    kcat, vcat, knp, vnp, m_sc, l_sc, acc_sc = rest[2 * pps + 1:]
    step = pl.program_id(1)
    last = pl.num_programs(1) - 1
    for pg in range(pps):
        kcat[pg * page:(pg + 1) * page, :] = ck[pg][0].astype(BF16)
        vcat[pg * page:(pg + 1) * page, :] = cv[pg][0].astype(BF16)

    @pl.when(step == last)
    def _():
        knp[...] = jnp.zeros(knp.shape, F32)
        vnp[...] = jnp.zeros(vnp.shape, F32)
        knp[0:kn_ref.shape[1], :] = kn_ref[0]
        vnp[0:vn_ref.shape[1], :] = vn_ref[0]

    lam = lam_ref[0]
    qn = q_ref[0]
    for h in range(N_HEADS):
        cols = slice(h * V_DIM, (h + 1) * V_DIM)
        qs = _split_branches(qn[:, cols])
        s = _dot_nt(qs, kcat[:, cols]) + bp_ref[0, h]

        @pl.when(step == 0)
        def _(s=s, cols=cols, h=h):
            _online_update(s, vcat[:, cols], m_sc.at[h], l_sc.at[h], acc_sc.at[h], first=True)

        @pl.when(step > 0)
        def _(s=s, cols=cols, h=h):
            _online_update(s, vcat[:, cols], m_sc.at[h], l_sc.at[h], acc_sc.at[h], first=False)

        @pl.when(step == last)
        def _(qs=qs, cols=cols, h=h):
            sn = _dot_nt(qs, knp[:, cols].astype(BF16)) + bn_ref[h]
            _online_update(sn, vnp[:, cols].astype(BF16), m_sc.at[h], l_sc.at[h], acc_sc.at[h], first=False)
            o = _diff_head_out(acc_sc[h], l_sc[h], n_new, lam, g_ref[...], lam_init)
            o_ref[0, :, cols] = o.astype(BF16)


def _page_index(n, s, pt_ref, *, pg, pps, n_pages):
    return (pt_ref[n * n_pages + s * pps + pg], 0, 0)


def _paged_attention(page_table, lam, q, k_new, v_new, bias_past, bias_new, subln_g, ck, cv, *, lam_init):
    n_seq, n_pages = page_table.shape
    n_new, aw = q.shape[1], q.shape[2]
    page = ck.shape[1]
    pps = math.gcd(PAGES_PER_STEP, n_pages)
    n_steps = n_pages // pps
    pad_new = k_new.shape[1]
    page_specs = [pl.BlockSpec((1, page, aw), functools.partial(_page_index, pg=pg, pps=pps, n_pages=n_pages))
                  for pg in range(pps)]
    per_seq = lambda n, s, pt: (n, 0, 0)
    return pl.pallas_call(
        functools.partial(_paged_attn_kernel, pps=pps, page=page, n_new=n_new, lam_init=lam_init),
        grid_spec=pltpu.PrefetchScalarGridSpec(
            num_scalar_prefetch=1,
            grid=(n_seq, n_steps),
            in_specs=[pl.BlockSpec(memory_space=pltpu.SMEM),
                      pl.BlockSpec((1, n_new, aw), per_seq),
                      pl.BlockSpec((1, pad_new, aw), per_seq),
                      pl.BlockSpec((1, pad_new, aw), per_seq),
                      pl.BlockSpec((1,) + bias_past.shape[1:], lambda n, s, pt: (s, 0, 0, 0)),
                      pl.BlockSpec(bias_new.shape, lambda n, s, pt: (0, 0, 0)),
                      pl.BlockSpec((1, V_DIM), lambda n, s, pt: (0, 0))] + page_specs + page_specs,
            out_specs=pl.BlockSpec((1, n_new, aw), per_seq),
            scratch_shapes=[pltpu.VMEM((pps * page, aw), BF16), pltpu.VMEM((pps * page, aw), BF16),
                            pltpu.VMEM((LANES, aw), F32), pltpu.VMEM((LANES, aw), F32),
                            pltpu.VMEM((N_HEADS, 2 * n_new, 1), F32), pltpu.VMEM((N_HEADS, 2 * n_new, 1), F32),
                            pltpu.VMEM((N_HEADS, 2 * n_new, V_DIM), F32)]),
        out_shape=jax.ShapeDtypeStruct((n_seq, n_new, aw), BF16),
        compiler_params=_params("arbitrary", "arbitrary"),
        name="paged_attention",
    )(page_table.reshape(-1), lam, q, k_new, v_new, bias_past, bias_new, subln_g, *([ck] * pps), *([cv] * pps))


def _rel_bias_lookup(rel_bias, dist):
    n = jnp.maximum(dist, 0)
    max_exact = N_BUCKETS // 2
    nf = jnp.maximum(n, 1).astype(F32)
    large = max_exact + (jnp.log(nf / max_exact) / math.log(MAX_DISTANCE / max_exact)
                         * (N_BUCKETS - max_exact)).astype(jnp.int32)
    large = jnp.minimum(large, N_BUCKETS - 1)
    return rel_bias[jnp.where(n < max_exact, n, large)].astype(F32)


def _prompt_bias_tables(rel_bias, tq):
    i = jnp.arange(tq)[:, None]
    j = jnp.arange(tq)[None, :]
    diag = jnp.where((j <= i)[None], jnp.transpose(_rel_bias_lookup(rel_bias, i - j), (2, 0, 1)), -jnp.inf)
    sub = jnp.transpose(_rel_bias_lookup(rel_bias, tq + i - j), (2, 0, 1))
    both = lambda b: jnp.concatenate([b, b], axis=1)
    return both(diag), both(sub), rel_bias[N_BUCKETS - 1].astype(F32)


def _sample_bias_tables(rel_bias, past_len, n_new, pad_new, n_steps):
    t = jnp.arange(n_new)
    past = jnp.transpose(_rel_bias_lookup(rel_bias, past_len + t[:, None] - jnp.arange(past_len)[None, :]), (2, 0, 1))
    past = jnp.concatenate([past, past], axis=1)
    past = past.reshape(N_HEADS, 2 * n_new, n_steps, past_len // n_steps).transpose(2, 0, 1, 3)
    tn = jnp.arange(LANES)
    new = jnp.transpose(_rel_bias_lookup(rel_bias, t[:, None] - tn[None, :]), (2, 0, 1))
    new = jnp.where((tn[None, :] <= t[:, None])[None], new, -jnp.inf)
    del pad_new
    return past, jnp.concatenate([new, new], axis=1)


def _routing_plan(route_p, cnt_p, route_s, cnt_s, n_rows):
    tm = EXPERT_TILE
    bucket_p = route_p[..., 0].reshape(-1).astype(jnp.int32)
    rank_p = route_p[..., 1].reshape(-1).astype(jnp.int32)
    bucket_s = route_s[..., 0].reshape(-1).astype(jnp.int32)
    rank_s = route_s[..., 1].reshape(-1).astype(jnp.int32)
    cp = cnt_p[0, :N_PAIR_BUCKETS].astype(jnp.int32)
    cs = cnt_s[0, :N_PAIR_BUCKETS].astype(jnp.int32)
    tiles = (cp + cs + tm - 1) // tm
    tile_end = jnp.cumsum(tiles)
    off = (tile_end - tiles) * tm
    pos_p = off[bucket_p] + rank_p
    pos_s = off[bucket_s] + cp[bucket_s] + rank_s
    n_tiles = n_rows // tm
    n_used = tile_end[-1]
    ti = jnp.minimum(jnp.arange(n_tiles), n_used - 1)
    tile_bucket = jnp.sum((ti[:, None] >= tile_end[None, :]).astype(jnp.int32), axis=1)
    pair = np.array(EXPERT_PAIRS, np.int32)
    grp = tile_bucket // len(EXPERT_PAIRS)
    pidx = tile_bucket % len(EXPERT_PAIRS)
    tile_ea = grp * EXP_PER_GROUP + jnp.asarray(pair[:, 0])[pidx]
    tile_eb = grp * EXP_PER_GROUP + jnp.asarray(pair[:, 1])[pidx]
    tile_valid = (jnp.arange(n_tiles) < n_used).astype(jnp.int32)
    return pos_p, pos_s, tile_ea.astype(jnp.int32), tile_eb.astype(jnp.int32), tile_valid


def kernel(x_prompt, x_sample, c_prompt, c_sample, cache_k, cache_v, state_pool, page_table, rel_bias, w_ada, b_ada, w_in, lambda_q1, lambda_k1, lambda_q2, lambda_k2, subln_g, w_pool, pool_scale, w_o, ln1_g, ln1_b, w_router_group, b_router_group, w_router_expert, b_router_expert, w_gate, w_up, w_down, ln2_g, ln2_b):
    depth = w_in.shape[0]
    nb, seq, d = x_prompt.shape
    n_seq, n_new, _ = x_sample.shape
    n_pages = page_table.shape[1]
    page = cache_k.shape[2]
    past_len = n_pages * page
    aw = N_HEADS * V_DIM
    pw = pool_scale.shape[1]
    alpha = (2 * depth) ** 0.25
    tq = min(ATTN_TILE, seq)
    assert tq >= MAX_DISTANCE and seq % tq == 0 and seq % min(SEQ_TILE, seq) == 0
    assert past_len >= POOL_HIST and n_new <= 8
    n_tok_p = nb * seq
    n_tok_s = n_seq * n_new
    assert n_tok_p % min(SEQ_TILE, n_tok_p) == 0
    n_rows = ((n_tok_p + n_tok_s) // EXPERT_TILE + N_PAIR_BUCKETS) * EXPERT_TILE
    pps = math.gcd(PAGES_PER_STEP, n_pages)
    n_steps = n_pages // pps

    bias_diag, bias_sub, c_far = _prompt_bias_tables(rel_bias, tq)
    bias_past, bias_new = _sample_bias_tables(rel_bias, past_len, n_new, 8, n_steps)
    c_all = jnp.concatenate([c_prompt, c_sample], axis=0)
    xs_tm = jnp.transpose(x_sample, (1, 0, 2)).reshape(n_tok_s, d)

    xp, xs_cur = x_prompt, xs_tm
    kp_l, vp_l, pp_l, ks_l, vs_l, ps_l = [], [], [], [], [], []
    for l in range(depth):
        lam_init = 0.8 - 0.6 * math.exp(-0.3 * l)
        lam = (jnp.exp(jnp.sum(lambda_q1[l].astype(F32) * lambda_k1[l].astype(F32)))
               - jnp.exp(jnp.sum(lambda_q2[l].astype(F32) * lambda_k2[l].astype(F32))) + lam_init).reshape(1)
        win_b = w_in[l].astype(BF16)
        wpool_b = w_pool[l].astype(BF16)
        wo_b = w_o[l].astype(BF16)
        wr = jnp.concatenate([w_router_group[l], jnp.transpose(w_router_expert[l], (1, 0, 2)).reshape(d, N_EXPERTS)], axis=1)
        wr_b = jnp.pad(wr, ((0, 0), (0, LANES - wr.shape[1]))).astype(BF16)
        br = jnp.pad(jnp.concatenate([b_router_group[l], b_router_expert[l].reshape(-1)]).astype(F32),
                     (0, LANES - N_EXP_GROUPS - N_EXPERTS)).reshape(1, LANES)
        wg_b = w_gate[l].reshape(N_EXPERTS, d, -1).astype(BF16)
        wu_b = w_up[l].reshape(N_EXPERTS, d, -1).astype(BF16)
        wd_b = w_down[l].reshape(N_EXPERTS, -1, d).astype(BF16)
        pscale = pool_scale[l].reshape(1, pw)
        g_sub = subln_g[l].reshape(1, V_DIM)
        ln1g, ln1b = ln1_g[l].reshape(1, d), ln1_b[l].reshape(1, d)
        ln2g, ln2b = ln2_g[l].reshape(1, d), ln2_b[l].reshape(1, d)

        m_all = _modulation(c_all, w_ada[l], b_ada[l]).reshape(nb + n_seq, 6, d)
        mod_p = m_all[:nb]
        mod_s = jnp.tile(jnp.transpose(m_all[nb:], (1, 0, 2)), (1, n_new, 1))

        q_p, k_p, v_p, kb_p, vb_p, pm_p, ph_p = _prompt_inproj(xp, mod_p, win_b, wpool_b, pscale)
        o_p = _prompt_attention(q_p, kb_p, vb_p, bias_diag, bias_sub, c_far, lam, g_sub, lam_init)
        y1_p, tokx_p, route_p, cnt_p = _outproj_router(o_p, pm_p, xp, mod_p, wo_b, ln1g, ln1b, wr_b, br,
                                                       alpha=alpha, per_row=False)

        hist = state_pool[l]
        q_s, k_s, v_s, p_s, pm_s = _sample_inproj(xs_cur, mod_s[0], mod_s[1], win_b, jnp.transpose(hist, (1, 0, 2)),
                                                  wpool_b, pscale, n_seq=n_seq, n_new=n_new, past_len=past_len)
        seq_major = lambda a: jnp.transpose(a.reshape(n_new, n_seq, -1), (1, 0, 2))
        pad8 = lambda a: jnp.pad(seq_major(a), ((0, 0), (0, 8 - n_new), (0, 0)))
        o_s = _paged_attention(page_table, lam, seq_major(q_s), pad8(k_s), pad8(v_s), bias_past, bias_new, g_sub,
                               cache_k[l].reshape(-1, page, aw), cache_v[l].reshape(-1, page, aw), lam_init=lam_init)
        o_s_tm = jnp.transpose(o_s, (1, 0, 2)).reshape(1, n_tok_s, aw)
        y1_s, tokx_s, route_s, cnt_s = _outproj_router(o_s_tm, pm_s[None], xs_cur[None], mod_s, wo_b, ln1g, ln1b,
                                                       wr_b, br, alpha=alpha, per_row=True)

        pos_p, pos_s, tile_ea, tile_eb, tile_valid = _routing_plan(route_p, cnt_p, route_s, cnt_s, n_rows)
        xs_sorted = jnp.zeros((n_rows, d + ROUTE_EXTRA), F32)
        xs_sorted = _dispatch(pos_p, tokx_p.reshape(n_tok_p, -1), xs_sorted)
        xs_sorted = _dispatch(pos_s, tokx_s.reshape(n_tok_s, -1), xs_sorted)
        o_sorted = _experts(tile_ea, tile_eb, tile_valid, xs_sorted, wg_b, wu_b, wd_b)
        xp = _combine(pos_p, y1_p, mod_p, ln2g, ln2b, o_sorted, alpha=alpha, per_row=False)
        xs_cur = _combine(pos_s, y1_s, mod_s, ln2g, ln2b, o_sorted, alpha=alpha, per_row=True)[0]

        kp_l.append(k_p.reshape(nb, seq, N_HEADS, V_DIM))
        vp_l.append(v_p.reshape(nb, seq, N_HEADS, V_DIM))
        pp_l.append(ph_p)
        ks_l.append(seq_major(k_s).reshape(n_seq, n_new, N_HEADS, V_DIM))
        vs_l.append(seq_major(v_s).reshape(n_seq, n_new, N_HEADS, V_DIM))
        ps_l.append(jnp.concatenate([hist, seq_major(p_s)], axis=1)[:, -POOL_HIST:])

    y_sample = jnp.transpose(xs_cur.reshape(n_new, n_seq, d), (1, 0, 2))
    return (xp, y_sample, jnp.stack(kp_l), jnp.stack(vp_l), jnp.stack(pp_l),
            jnp.stack(ks_l), jnp.stack(vs_l), jnp.stack(ps_l))
```

```python
import functools
import math

import numpy as np
import jax
import jax.numpy as jnp
from jax import lax
from jax.experimental import pallas as pl
from jax.experimental.pallas import tpu as pltpu

F32 = jnp.float32
BF16 = jnp.bfloat16

N_HEADS = 4
HEAD_DIM = 64
V_DIM = 2 * HEAD_DIM
POOL_WINDOWS = (2, 4, 8, 16)
POOL_GROUP = 128
POOL_HIST = max(POOL_WINDOWS) - 1
N_BUCKETS = 32
MAX_DISTANCE = 128
N_EXP_GROUPS = 4
EXP_PER_GROUP = 4
N_EXPERTS = N_EXP_GROUPS * EXP_PER_GROUP
EXPERT_PAIRS = ((0, 1), (0, 2), (0, 3), (1, 2), (1, 3), (2, 3))
N_PAIR_BUCKETS = N_EXP_GROUPS * len(EXPERT_PAIRS)
LN_EPS = 1e-5
RMS_EPS = 1e-5

LANES = 128
POOL_HALO = 16
VMEM_LIMIT_BYTES = 48 * 1024 * 1024
SEQ_TILE = 512
ATTN_TILE = 256
EXPERT_TILE = 256
PAGES_PER_STEP = 8
ROUTE_EXTRA = LANES
LOG2E = math.log2(math.e)
Q_SCALE = HEAD_DIM ** -0.5 * LOG2E


def _params(*sem):
    return pltpu.CompilerParams(dimension_semantics=sem, vmem_limit_bytes=VMEM_LIMIT_BYTES)


def _dot(a, b):
    return jnp.dot(a, b, preferred_element_type=F32)


def _dot_nt(a, b):
    return lax.dot_general(a, b, (((1,), (1,)), ((), ())), preferred_element_type=F32)


def _sigmoid(x):
    return 1.0 / (1.0 + jnp.exp(-x))


def _mod_kernel(c_ref, w_ref, b_ref, o_ref):
    c = c_ref[...]
    s = c * _sigmoid(c)
    s_hi = s.astype(BF16)
    s_lo = (s - s_hi.astype(F32)).astype(BF16)
    w = w_ref[...]
    w_hi = w.astype(BF16)
    w_lo = (w - w_hi.astype(F32)).astype(BF16)
    o_ref[...] = _dot(s_hi, w_hi) + _dot(s_lo, w_hi) + _dot(s_hi, w_lo) + b_ref[...]


def _modulation(c_all, w_ada, b_ada):
    n, d = c_all.shape
    e = w_ada.shape[1]
    bn = 1024
    return pl.pallas_call(
        _mod_kernel,
        grid=(e // bn,),
        in_specs=[pl.BlockSpec((n, d), lambda j: (0, 0)),
                  pl.BlockSpec((d, bn), lambda j: (0, j)),
                  pl.BlockSpec((1, bn), lambda j: (0, j))],
        out_specs=pl.BlockSpec((n, bn), lambda j: (0, j)),
        out_shape=jax.ShapeDtypeStruct((n, e), F32),
        compiler_params=_params("arbitrary"),
        name="modulation",
    )(c_all, w_ada, b_ada.reshape(1, e))


def _pool_mixed(ext, g, w, inv_cnt, rows):
    eg = ext[:, g * POOL_GROUP:(g + 1) * POOL_GROUP]
    s = eg
    step = 1
    while step < w:
        s = s + pltpu.roll(s, step, 0)
        step *= 2
    return s[POOL_HALO:POOL_HALO + rows] * inv_cnt - eg[POOL_HALO:POOL_HALO + rows]


def _inproj_kernel(x_ref, mod_ref, win_ref, wpool_ref, pscale_ref,
                   q_ref, k_ref, v_ref, kb_ref, vb_ref, pm_ref, ph_ref, ext_ref, *, ts, aw):
    i = pl.program_id(1)

    @pl.when(i == 0)
    def _():
        ext_ref[0:POOL_HALO, :] = jnp.zeros((POOL_HALO, ext_ref.shape[1]), F32)

    @pl.when(i > 0)
    def _():
        ext_ref[0:POOL_HALO, :] = ext_ref[ts:ts + POOL_HALO, :]

    shift = mod_ref[0, 0:1, :]
    scale = mod_ref[0, 1:2, :]
    u = (x_ref[0] * (1.0 + scale) + shift).astype(BF16)
    acc = _dot(u, win_ref[...])
    q_ref[0] = (acc[:, 0:aw] * Q_SCALE).astype(BF16)
    k = acc[:, aw:2 * aw]
    v = acc[:, 2 * aw:3 * aw]
    for h in range(N_HEADS):
        k_ref[0, :, h, :] = k[:, h * V_DIM:(h + 1) * V_DIM]
        v_ref[0, :, h, :] = v[:, h * V_DIM:(h + 1) * V_DIM]
    kb_ref[0] = k.astype(BF16)
    vb_ref[0] = v.astype(BF16)
    p = acc[:, 3 * aw:]
    ext_ref[POOL_HALO:POOL_HALO + ts, :] = p
    ext = ext_ref[...]
    pos = i * ts + lax.broadcasted_iota(jnp.int32, (ts, 1), 0)
    for g, w in enumerate(POOL_WINDOWS):
        inv_cnt = 1.0 / jnp.minimum(w, pos + 1).astype(F32)
        mixed = _pool_mixed(ext, g, w, inv_cnt, ts)
        cols = slice(g * POOL_GROUP, (g + 1) * POOL_GROUP)
        pm_ref[0, :, cols] = (_dot(mixed.astype(BF16), wpool_ref[g]) * pscale_ref[:, cols]).astype(BF16)

    @pl.when(i == pl.num_programs(1) - 1)
    def _():
        ph_ref[0] = p[ts - POOL_HIST:ts, :]


def _prompt_inproj(x, mod, win_b, wpool_b, pscale):
    nb, s, d = x.shape
    ew = win_b.shape[1]
    aw = (ew - 4 * POOL_GROUP) // 3
    pw = ew - 3 * aw
    ts = min(SEQ_TILE, s)
    nt = s // ts
    tile = lambda b, i: (b, i, 0)
    const2 = lambda b, i: (0, 0)
    return pl.pallas_call(
        functools.partial(_inproj_kernel, ts=ts, aw=aw),
        grid=(nb, nt),
        in_specs=[pl.BlockSpec((1, ts, d), tile),
                  pl.BlockSpec((1, 6, d), lambda b, i: (b, 0, 0)),
                  pl.BlockSpec((d, ew), const2),
                  pl.BlockSpec(wpool_b.shape, lambda b, i: (0, 0, 0)),
                  pl.BlockSpec((1, pw), const2)],
        out_specs=[pl.BlockSpec((1, ts, aw), tile),
                   pl.BlockSpec((1, ts, N_HEADS, V_DIM), lambda b, i: (b, i, 0, 0)),
                   pl.BlockSpec((1, ts, N_HEADS, V_DIM), lambda b, i: (b, i, 0, 0)),
                   pl.BlockSpec((1, ts, aw), tile), pl.BlockSpec((1, ts, aw), tile),
                   pl.BlockSpec((1, ts, pw), tile),
                   pl.BlockSpec((1, POOL_HIST, pw), lambda b, i: (b, 0, 0))],
        out_shape=[jax.ShapeDtypeStruct((nb, s, aw), BF16),
                   jax.ShapeDtypeStruct((nb, s, N_HEADS, V_DIM), F32),
                   jax.ShapeDtypeStruct((nb, s, N_HEADS, V_DIM), F32),
                   jax.ShapeDtypeStruct((nb, s, aw), BF16),
                   jax.ShapeDtypeStruct((nb, s, aw), BF16),
                   jax.ShapeDtypeStruct((nb, s, pw), BF16),
                   jax.ShapeDtypeStruct((nb, POOL_HIST, pw), F32)],
        scratch_shapes=[pltpu.VMEM((ts + POOL_HALO, pw), F32)],
        compiler_params=_params("arbitrary", "arbitrary"),
        name="prompt_inproj",
    )(x, mod, win_b, wpool_b, pscale)


def _split_branches(qh):
    lane = lax.broadcasted_iota(jnp.int32, qh.shape, 1)
    zero = jnp.zeros_like(qh)
    return jnp.concatenate([jnp.where(lane < HEAD_DIM, qh, zero), jnp.where(lane >= HEAD_DIM, qh, zero)], axis=0)


def _online_update(s, v_tile, m_ref, l_ref, acc_ref, first):
    rows, width = s.shape
    rep = lambda a: jnp.concatenate([a] * (width // LANES), axis=1) if width > LANES else a
    m_new = jnp.broadcast_to(jnp.max(s, axis=1, keepdims=True), (rows, LANES))
    if not first:
        m_old = m_ref[...]
        m_new = jnp.maximum(m_old, m_new)
    p = jnp.exp2(s - rep(m_new))
    pv = _dot(p.astype(BF16), v_tile)
    l_new = jnp.broadcast_to(jnp.sum(p, axis=1, keepdims=True), (rows, LANES))
    if first:
        acc_ref[...] = pv
        l_ref[...] = l_new
    else:
        alpha = jnp.exp2(m_old - m_new)
        acc_ref[...] = alpha * acc_ref[...] + pv
        l_ref[...] = alpha * l_ref[...] + l_new
    m_ref[...] = m_new


def _diff_head_out(acc, l, t, lam, g, lam_init):
    o = acc[0:t] / l[0:t] - lam * (acc[t:2 * t] / l[t:2 * t])
    o = o * lax.rsqrt(jnp.mean(o * o, axis=-1, keepdims=True) + RMS_EPS) * g
    return o * (1.0 - lam_init)


def _attn_kernel(cfar_ref, lam_ref, q_ref, k_ref, v_ref, bd_ref, be_ref, g_ref, o_ref,
                 qs_sc, m_sc, l_sc, acc_sc, *, tq, lam_init):
    qi = pl.program_id(1)
    heads = [slice(h * V_DIM, (h + 1) * V_DIM) for h in range(N_HEADS)]
    for h, cols in enumerate(heads):
        qs_sc[h] = _split_branches(q_ref[0, :, cols])

    def step(j, bias, first):
        rows = pl.ds(pl.multiple_of(j * tq, tq), tq)
        for h, cols in enumerate(heads):
            s = _dot_nt(qs_sc[h], k_ref[0, rows, cols]) + bias(h)
            _online_update(s, v_ref[0, rows, cols], m_sc.at[h], l_sc.at[h], acc_sc.at[h], first)

    step(qi, lambda h: bd_ref[h], True)

    @pl.when(qi >= 1)
    def _():
        step(qi - 1, lambda h: be_ref[h], False)

    def far(j, carry):
        step(j, lambda h: cfar_ref[h], False)
        return carry

    lax.fori_loop(0, jnp.maximum(qi - 1, 0), far, 0)
    lam = lam_ref[0]
    for h, cols in enumerate(heads):
        o = _diff_head_out(acc_sc[h], l_sc[h], tq, lam, g_ref[...], lam_init)
        o_ref[0, :, cols] = o.astype(BF16)


def _prompt_attention(q, kb, vb, bias_diag, bias_sub, c_far, lam, subln_g, lam_init):
    nb, s, aw = q.shape
    tq = bias_diag.shape[2]
    smem = pl.BlockSpec(memory_space=pltpu.SMEM)
    const3 = lambda b, i: (0, 0, 0)
    return pl.pallas_call(
        functools.partial(_attn_kernel, tq=tq, lam_init=lam_init),
        grid=(nb, s // tq),
        in_specs=[smem, smem,
                  pl.BlockSpec((1, tq, aw), lambda b, i: (b, i, 0)),
                  pl.BlockSpec((1, s, aw), lambda b, i: (b, 0, 0)),
                  pl.BlockSpec((1, s, aw), lambda b, i: (b, 0, 0)),
                  pl.BlockSpec(bias_diag.shape, const3),
                  pl.BlockSpec(bias_sub.shape, const3),
                  pl.BlockSpec((1, V_DIM), lambda b, i: (0, 0))],
        out_specs=pl.BlockSpec((1, tq, aw), lambda b, i: (b, i, 0)),
        out_shape=jax.ShapeDtypeStruct((nb, s, aw), BF16),
        scratch_shapes=[pltpu.VMEM((N_HEADS, 2 * tq, V_DIM), BF16),
                        pltpu.VMEM((N_HEADS, 2 * tq, LANES), F32), pltpu.VMEM((N_HEADS, 2 * tq, LANES), F32),
                        pltpu.VMEM((N_HEADS, 2 * tq, V_DIM), F32)],
        compiler_params=_params("arbitrary", "arbitrary"),
        name="prompt_attention",
    )(c_far, lam, q, kb, vb, bias_diag, bias_sub, subln_g)


def _layer_norm(z, g, b):
    mu = jnp.mean(z, axis=-1, keepdims=True)
    zc = z - mu
    var = jnp.mean(zc * zc, axis=-1, keepdims=True)
    return zc * lax.rsqrt(var + LN_EPS) * g + b


def _mod_get(mod_ref, j, per_row):
    return mod_ref[j] if per_row else mod_ref[0, j:j + 1, :]


def _outproj_router_kernel(o_ref, pm_ref, x_ref, mod_ref, wo_ref, lng_ref, lnb_ref, wr_ref, br_ref,
                           y_ref, tokx_ref, route_ref, cnt_ref, carry_sc, *, alpha, per_row):
    t = x_ref.shape[1]
    d = x_ref.shape[2]
    aw = o_ref.shape[2]

    @pl.when((pl.program_id(0) == 0) & (pl.program_id(1) == 0))
    def _():
        carry_sc[...] = jnp.zeros(carry_sc.shape, F32)

    h = _dot(o_ref[0], wo_ref[0:aw, :]) + _dot(pm_ref[0], wo_ref[aw:, :])
    gate1 = _mod_get(mod_ref, 2, per_row)
    y1 = _layer_norm(alpha * x_ref[0] + gate1 * h, lng_ref[...], lnb_ref[...])
    y_ref[0] = y1
    tok = y1 * (1.0 + _mod_get(mod_ref, 4, per_row)) + _mod_get(mod_ref, 3, per_row)
    tokx_ref[0, :, 0:d] = tok

    logits = _dot(tok.astype(BF16), wr_ref[...]) + br_ref[...]
    lane = lax.broadcasted_iota(jnp.int32, logits.shape, 1)
    lane_f = lane.astype(F32)
    neg = jnp.full_like(logits, -jnp.inf)

    def first_argmax(vals, vmax):
        return jnp.min(jnp.where(vals == vmax, lane_f, float(LANES)), axis=1, keepdims=True).astype(jnp.int32)

    gl = jnp.where(lane < N_EXP_GROUPS, logits, neg)
    gmax = jnp.max(gl, axis=1, keepdims=True)
    gidx = first_argmax(gl, gmax)
    g_w = 1.0 / jnp.sum(jnp.exp(gl - gmax), axis=1, keepdims=True)
    lo_lane = N_EXP_GROUPS + EXP_PER_GROUP * gidx
    el = jnp.where((lane >= lo_lane) & (lane < lo_lane + EXP_PER_GROUP), logits, neg)
    v1 = jnp.max(el, axis=1, keepdims=True)
    i1 = first_argmax(el, v1)
    el2 = jnp.where(lane == i1, neg, el)
    v2 = jnp.max(el2, axis=1, keepdims=True)
    i2 = first_argmax(el2, v2)
    e21 = jnp.exp(v2 - v1)
    w1 = g_w / (1.0 + e21)
    w2 = g_w * e21 / (1.0 + e21)
    first_lo = i1 < i2
    cw_lo = jnp.where(first_lo, w1, w2)
    cw_hi = jnp.where(first_lo, w2, w1)
    a = jnp.minimum(i1, i2) - lo_lane
    b = jnp.maximum(i1, i2) - lo_lane
    pair_base = jnp.where(a == 0, 0, jnp.where(a == 1, 3, 5))
    bucket = gidx * len(EXPERT_PAIRS) + pair_base + (b - a - 1)

    xlane = lax.broadcasted_iota(jnp.int32, (t, ROUTE_EXTRA), 1)
    tokx_ref[0, :, d:] = jnp.where(xlane == 0, cw_lo, jnp.where(xlane == 1, cw_hi, 0.0))

    onehot = lane == bucket
    row = lax.broadcasted_iota(jnp.int32, (t, t), 0)
    col = lax.broadcasted_iota(jnp.int32, (t, t), 1)
    ltri = jnp.where(col < row, 1.0, 0.0).astype(BF16)
    prefix = _dot(ltri, jnp.where(onehot, 1.0, 0.0).astype(BF16)) + carry_sc[...]
    rank = jnp.sum(jnp.where(onehot, prefix, 0.0), axis=1, keepdims=True)
    carry_sc[...] = carry_sc[...] + jnp.sum(jnp.where(onehot, 1.0, 0.0), axis=0, keepdims=True)
    cnt_ref[...] = carry_sc[...]
    route_ref[0] = jnp.where(lane == 0, bucket.astype(F32), jnp.where(lane == 1, rank, 0.0))


def _outproj_router(o, pm, x, mod, wo_b, ln_g, ln_b, wr_b, br, *, alpha, per_row):
    nb, s, d = x.shape
    aw = o.shape[2]
    pw = pm.shape[2]
    t = min(SEQ_TILE, s)
    tile = lambda b, i: (b, i, 0)
    const2 = lambda b, i: (0, 0)
    mod_spec = (pl.BlockSpec(mod.shape, lambda b, i: (0, 0, 0)) if per_row
                else pl.BlockSpec((1, 6, d), lambda b, i: (b, 0, 0)))
    return pl.pallas_call(
        functools.partial(_outproj_router_kernel, alpha=alpha, per_row=per_row),
        grid=(nb, s // t),
        in_specs=[pl.BlockSpec((1, t, aw), tile), pl.BlockSpec((1, t, pw), tile), pl.BlockSpec((1, t, d), tile),
                  mod_spec,
                  pl.BlockSpec(wo_b.shape, const2), pl.BlockSpec((1, d), const2), pl.BlockSpec((1, d), const2),
                  pl.BlockSpec(wr_b.shape, const2), pl.BlockSpec((1, LANES), const2)],
        out_specs=[pl.BlockSpec((1, t, d), tile), pl.BlockSpec((1, t, d + ROUTE_EXTRA), tile),
                   pl.BlockSpec((1, t, LANES), tile), pl.BlockSpec((1, LANES), const2)],
        out_shape=[jax.ShapeDtypeStruct((nb, s, d), F32), jax.ShapeDtypeStruct((nb, s, d + ROUTE_EXTRA), F32),
                   jax.ShapeDtypeStruct((nb, s, LANES), F32), jax.ShapeDtypeStruct((1, LANES), F32)],
        scratch_shapes=[pltpu.VMEM((1, LANES), F32)],
        compiler_params=_params("arbitrary", "arbitrary"),
        name="outproj_router",
    )(o, pm, x, mod, wo_b, ln_g, ln_b, wr_b, br)


def _row_copy(src, dst, sem):
    return pltpu.make_async_copy(src, dst, sem)


def _dispatch_kernel(pos_ref, tok_ref, xs_in_ref, xs_ref, sem, *, t):
    del xs_in_ref
    base = pl.program_id(0) * t

    def issue(r, carry):
        _row_copy(tok_ref.at[pl.ds(r, 1)], xs_ref.at[pl.ds(pos_ref[base + r], 1)], sem).start()
        return carry

    lax.fori_loop(0, t, issue, 0, unroll=8)

    def drain(r, carry):
        _row_copy(tok_ref.at[pl.ds(0, 1)], xs_ref.at[pl.ds(0, 1)], sem).wait()
        return carry

    lax.fori_loop(0, t, drain, 0, unroll=8)


def _dispatch(pos, tokx, xs):
    n, w = tokx.shape
    t = min(SEQ_TILE, n)
    return pl.pallas_call(
        functools.partial(_dispatch_kernel, t=t),
        grid_spec=pltpu.PrefetchScalarGridSpec(
            num_scalar_prefetch=1,
            grid=(n // t,),
            in_specs=[pl.BlockSpec((t, w), lambda i, pos: (i, 0)),
                      pl.BlockSpec(memory_space=pl.ANY)],
            out_specs=pl.BlockSpec(memory_space=pl.ANY),
            scratch_shapes=[pltpu.SemaphoreType.DMA(())]),
        out_shape=jax.ShapeDtypeStruct(xs.shape, xs.dtype),
        input_output_aliases={2: 0},
        compiler_params=_params("arbitrary"),
        name="dispatch_rows",
    )(pos, tokx, xs)


def _expert_kernel(ea_ref, eb_ref, valid_ref, xs_ref, wga_ref, wua_ref, wda_ref, wgb_ref, wub_ref, wdb_ref,
                   o_ref, *, d):
    del ea_ref, eb_ref
    ti = pl.program_id(0)

    @pl.when(valid_ref[ti] == 1)
    def _():
        xs = xs_ref[...]
        x = xs[:, 0:d].astype(BF16)

        def ffn(wg_ref, wu_ref, wd_ref, cw):
            g = _dot(x, wg_ref[0])
            u = _dot(x, wu_ref[0])
            hid = (g * _sigmoid(g)) * u * cw
            return _dot(hid.astype(BF16), wd_ref[0])

        o_ref[...] = (ffn(wga_ref, wua_ref, wda_ref, xs[:, d:d + 1])
                      + ffn(wgb_ref, wub_ref, wdb_ref, xs[:, d + 1:d + 2]))

    @pl.when(valid_ref[ti] == 0)
    def _():
        o_ref[...] = jnp.zeros(o_ref.shape, F32)


def _experts(tile_ea, tile_eb, tile_valid, xs, wg_b, wu_b, wd_b):
    r, w = xs.shape
    d = w - ROUTE_EXTRA
    f = wg_b.shape[2]
    tm = EXPERT_TILE
    wa = lambda blk: pl.BlockSpec(blk, lambda i, ea, eb, va: (ea[i], 0, 0))
    wb = lambda blk: pl.BlockSpec(blk, lambda i, ea, eb, va: (eb[i], 0, 0))
    return pl.pallas_call(
        functools.partial(_expert_kernel, d=d),
        grid_spec=pltpu.PrefetchScalarGridSpec(
            num_scalar_prefetch=3,
            grid=(r // tm,),
            in_specs=[pl.BlockSpec((tm, w), lambda i, ea, eb, va: (i, 0)),
                      wa((1, d, f)), wa((1, d, f)), wa((1, f, d)),
                      wb((1, d, f)), wb((1, d, f)), wb((1, f, d))],
            out_specs=pl.BlockSpec((tm, d), lambda i, ea, eb, va: (i, 0))),
        out_shape=jax.ShapeDtypeStruct((r, d), F32),
        compiler_params=_params("arbitrary"),
        name="expert_ffn",
    )(tile_ea, tile_eb, tile_valid, xs, wg_b, wu_b, wd_b, wg_b, wu_b, wd_b)


def _combine_kernel(pos_ref, y_ref, mod_ref, lng_ref, lnb_ref, os_ref, out_ref, buf, sem, *, alpha, per_row):
    t = y_ref.shape[1]
    base = (pl.program_id(0) * pl.num_programs(1) + pl.program_id(1)) * t

    def issue(r, carry):
        _row_copy(os_ref.at[pl.ds(pos_ref[base + r], 1)], buf.at[pl.ds(r, 1)], sem).start()
        return carry

    lax.fori_loop(0, t, issue, 0, unroll=8)

    def drain(r, carry):
        _row_copy(os_ref.at[pl.ds(0, 1)], buf.at[pl.ds(0, 1)], sem).wait()
        return carry

    lax.fori_loop(0, t, drain, 0, unroll=8)
    gate2 = _mod_get(mod_ref, 5, per_row)
    out_ref[0] = _layer_norm(alpha * y_ref[0] + gate2 * buf[...], lng_ref[...], lnb_ref[...])


def _combine(pos, y1, mod, ln_g, ln_b, o_sorted, *, alpha, per_row):
    nb, s, d = y1.shape
    t = min(SEQ_TILE, s)
    tile = lambda b, i, pos: (b, i, 0)
    const2 = lambda b, i, pos: (0, 0)
    mod_spec = (pl.BlockSpec(mod.shape, lambda b, i, pos: (0, 0, 0)) if per_row
                else pl.BlockSpec((1, 6, d), lambda b, i, pos: (b, 0, 0)))
    return pl.pallas_call(
        functools.partial(_combine_kernel, alpha=alpha, per_row=per_row),
        grid_spec=pltpu.PrefetchScalarGridSpec(
            num_scalar_prefetch=1,
            grid=(nb, s // t),
            in_specs=[pl.BlockSpec((1, t, d), tile), mod_spec,
                      pl.BlockSpec((1, d), const2), pl.BlockSpec((1, d), const2),
                      pl.BlockSpec(memory_space=pl.ANY)],
            out_specs=pl.BlockSpec((1, t, d), tile),
            scratch_shapes=[pltpu.VMEM((t, d), F32), pltpu.SemaphoreType.DMA(())]),
        out_shape=jax.ShapeDtypeStruct((nb, s, d), F32),
        compiler_params=_params("arbitrary", "arbitrary"),
        name="combine_rows",
    )(pos, y1, mod, ln_g, ln_b, o_sorted)


def _sample_inproj_kernel(x_ref, shift_ref, scale_ref, win_ref, hist_ref, wpool_ref, pscale_ref,
                          q_ref, k_ref, v_ref, p_ref, pm_ref, *, aw, n_seq, n_new, past_len):
    u = (x_ref[...] * (1.0 + scale_ref[...]) + shift_ref[...]).astype(BF16)
    acc = _dot(u, win_ref[...])
    q_ref[...] = (acc[:, 0:aw] * Q_SCALE).astype(BF16)
    k_ref[...] = acc[:, aw:2 * aw]
    v_ref[...] = acc[:, 2 * aw:3 * aw]
    p = acc[:, 3 * aw:]
    p_ref[...] = p
    n_hist = hist_ref.shape[0]
    rows = [hist_ref[j] for j in range(n_hist)] + [p[t * n_seq:(t + 1) * n_seq] for t in range(n_new)]
    for g, w in enumerate(POOL_WINDOWS):
        cols = slice(g * POOL_GROUP, (g + 1) * POOL_GROUP)
        mixed = []
        for t in range(n_new):
            end = n_hist + t + 1
            start = max(end - w, 0)
            win_sum = rows[start][:, cols]
            for j in range(start + 1, end):
                win_sum = win_sum + rows[j][:, cols]
            count = float(min(w, past_len + t + 1))
            mixed.append(win_sum / count - rows[n_hist + t][:, cols])
        mixed = jnp.concatenate(mixed, axis=0)
        pm_ref[:, cols] = (_dot(mixed.astype(BF16), wpool_ref[g]) * pscale_ref[:, cols]).astype(BF16)


def _sample_inproj(x, shift, scale, win_b, hist_t, wpool_b, pscale, *, n_seq, n_new, past_len):
    n, d = x.shape
    ew = win_b.shape[1]
    aw = (ew - 4 * POOL_GROUP) // 3
    pw = ew - 3 * aw
    return pl.pallas_call(
        functools.partial(_sample_inproj_kernel, aw=aw, n_seq=n_seq, n_new=n_new, past_len=past_len),
        out_shape=[jax.ShapeDtypeStruct((n, aw), BF16), jax.ShapeDtypeStruct((n, aw), F32),
                   jax.ShapeDtypeStruct((n, aw), F32), jax.ShapeDtypeStruct((n, pw), F32),
                   jax.ShapeDtypeStruct((n, pw), BF16)],
        compiler_params=pltpu.CompilerParams(vmem_limit_bytes=VMEM_LIMIT_BYTES),
        name="sample_inproj",
    )(x, shift, scale, win_b, hist_t, wpool_b, pscale)


def _paged_attn_kernel(pt_ref, lam_ref, q_ref, kn_ref, vn_ref, bp_ref, bn_ref, g_ref, *rest,
                       pps, page, n_new, lam_init):
    del pt_ref
    ck = rest[0:pps]
    cv = rest[pps:2 * pps]
    o_ref = rest[2 * pps]
    kcat, vcat, knp, vnp, m_sc, l_sc, acc_sc = rest[2 * pps + 1:]
    step = pl.program_id(1)
    last = pl.num_programs(1) - 1
    heads = [slice(h * V_DIM, (h + 1) * V_DIM) for h in range(N_HEADS)]
    for pg in range(pps):
        for h, cols in enumerate(heads):
            kcat[pg * page:(pg + 1) * page, cols] = ck[pg][pl.ds(h, page, stride=N_HEADS), :].astype(BF16)
            vcat[pg * page:(pg + 1) * page, cols] = cv[pg][pl.ds(h, page, stride=N_HEADS), :].astype(BF16)

    @pl.when(step == 0)
    def _():
        m_sc[...] = jnp.full(m_sc.shape, -jnp.inf, F32)
        l_sc[...] = jnp.zeros(l_sc.shape, F32)
        acc_sc[...] = jnp.zeros(acc_sc.shape, F32)

    qn = q_ref[0]
    qs = [_split_branches(qn[:, cols]) for cols in heads]
    for h, cols in enumerate(heads):
        s = _dot_nt(qs[h], kcat[:, cols]) + bp_ref[0, h]
        _online_update(s, vcat[:, cols], m_sc.at[h], l_sc.at[h], acc_sc.at[h], first=False)

    @pl.when(step == last)
    def _():
        knp[...] = jnp.zeros(knp.shape, F32)
        vnp[...] = jnp.zeros(vnp.shape, F32)
        knp[0:kn_ref.shape[1], :] = kn_ref[0]
        vnp[0:vn_ref.shape[1], :] = vn_ref[0]
        lam = lam_ref[0]
        for h, cols in enumerate(heads):
            sn = _dot_nt(qs[h], knp[:, cols].astype(BF16)) + bn_ref[h]
            _online_update(sn, vnp[:, cols].astype(BF16), m_sc.at[h], l_sc.at[h], acc_sc.at[h], first=False)
            o = _diff_head_out(acc_sc[h], l_sc[h], n_new, lam, g_ref[...], lam_init)
            o_ref[0, :, cols] = o.astype(BF16)


def _page_index(n, s, pt_ref, *, first_page, pg, pps, n_pages):
    return (first_page + pt_ref[n * n_pages + s * pps + pg], 0)


def _paged_attention(page_table, lam, q, k_new, v_new, bias_past, bias_new, subln_g, cache_k, cache_v, *,
                     layer, lam_init):
    n_seq, n_pages = page_table.shape
    n_new, aw = q.shape[1], q.shape[2]
    n_phys, page = cache_k.shape[1], cache_k.shape[2]
    pps = math.gcd(PAGES_PER_STEP, n_pages)
    n_steps = n_pages // pps
    pad_new = k_new.shape[1]
    ck = cache_k.reshape(-1, V_DIM)
    cv = cache_v.reshape(-1, V_DIM)
    page_specs = [pl.BlockSpec((page * N_HEADS, V_DIM),
                               functools.partial(_page_index, first_page=layer * n_phys, pg=pg, pps=pps,
                                                 n_pages=n_pages))
                  for pg in range(pps)]
    per_seq = lambda n, s, pt: (n, 0, 0)
    return pl.pallas_call(
        functools.partial(_paged_attn_kernel, pps=pps, page=page, n_new=n_new, lam_init=lam_init),
        grid_spec=pltpu.PrefetchScalarGridSpec(
            num_scalar_prefetch=1,
            grid=(n_seq, n_steps),
            in_specs=[pl.BlockSpec(memory_space=pltpu.SMEM),
                      pl.BlockSpec((1, n_new, aw), per_seq),
                      pl.BlockSpec((1, pad_new, aw), per_seq),
                      pl.BlockSpec((1, pad_new, aw), per_seq),
                      pl.BlockSpec((1,) + bias_past.shape[1:], lambda n, s, pt: (s, 0, 0, 0)),
                      pl.BlockSpec(bias_new.shape, lambda n, s, pt: (0, 0, 0)),
                      pl.BlockSpec((1, V_DIM), lambda n, s, pt: (0, 0))] + page_specs + page_specs,
            out_specs=pl.BlockSpec((1, n_new, aw), per_seq),
            scratch_shapes=[pltpu.VMEM((pps * page, aw), BF16), pltpu.VMEM((pps * page, aw), BF16),
                            pltpu.VMEM((LANES, aw), F32), pltpu.VMEM((LANES, aw), F32),
                            pltpu.VMEM((N_HEADS, 2 * n_new, LANES), F32), pltpu.VMEM((N_HEADS, 2 * n_new, LANES), F32),
                            pltpu.VMEM((N_HEADS, 2 * n_new, V_DIM), F32)]),
        out_shape=jax.ShapeDtypeStruct((n_seq, n_new, aw), BF16),
        compiler_params=_params("arbitrary", "arbitrary"),
        name="paged_attention",
    )(page_table.reshape(-1), lam, q, k_new, v_new, bias_past, bias_new, subln_g, *([ck] * pps), *([cv] * pps))


def _rel_bias_lookup(rel_bias, dist):
    n = jnp.maximum(dist, 0)
    max_exact = N_BUCKETS // 2
    nf = jnp.maximum(n, 1).astype(F32)
    large = max_exact + (jnp.log(nf / max_exact) / math.log(MAX_DISTANCE / max_exact)
                         * (N_BUCKETS - max_exact)).astype(jnp.int32)
    large = jnp.minimum(large, N_BUCKETS - 1)
    bucket = jnp.where(n < max_exact, n, large)
    onehot = (bucket[..., None] == jnp.arange(N_BUCKETS)).astype(F32)
    return jnp.einsum('...b,bh->...h', onehot, rel_bias.astype(F32), precision=lax.Precision.HIGHEST) * LOG2E


def _prompt_bias_tables(rel_bias, tq):
    i = jnp.arange(tq)[:, None]
    j = jnp.arange(tq)[None, :]
    diag = jnp.where((j <= i)[None], jnp.transpose(_rel_bias_lookup(rel_bias, i - j), (2, 0, 1)), -jnp.inf)
    sub = jnp.transpose(_rel_bias_lookup(rel_bias, tq + i - j), (2, 0, 1))
    both = lambda b: jnp.concatenate([b, b], axis=1)
    return both(diag), both(sub), rel_bias[N_BUCKETS - 1].astype(F32) * LOG2E


def _sample_bias_tables(rel_bias, past_len, n_new, pad_new, n_steps):
    t = jnp.arange(n_new)
    past = jnp.transpose(_rel_bias_lookup(rel_bias, past_len + t[:, None] - jnp.arange(past_len)[None, :]), (2, 0, 1))
    past = jnp.concatenate([past, past], axis=1)
    past = past.reshape(N_HEADS, 2 * n_new, n_steps, past_len // n_steps).transpose(2, 0, 1, 3)
    tn = jnp.arange(LANES)
    new = jnp.transpose(_rel_bias_lookup(rel_bias, t[:, None] - tn[None, :]), (2, 0, 1))
    new = jnp.where((tn[None, :] <= t[:, None])[None], new, -jnp.inf)
    del pad_new
    return past, jnp.concatenate([new, new], axis=1)


def _routing_plan(route_p, cnt_p, route_s, cnt_s, n_rows):
    tm = EXPERT_TILE
    bucket_p = route_p[..., 0].reshape(-1).astype(jnp.int32)
    rank_p = route_p[..., 1].reshape(-1).astype(jnp.int32)
    bucket_s = route_s[..., 0].reshape(-1).astype(jnp.int32)
    rank_s = route_s[..., 1].reshape(-1).astype(jnp.int32)
    cp = cnt_p[0, :N_PAIR_BUCKETS].astype(jnp.int32)
    cs = cnt_s[0, :N_PAIR_BUCKETS].astype(jnp.int32)
    tiles = (cp + cs + tm - 1) // tm
    tile_end = jnp.cumsum(tiles)
    off = (tile_end - tiles) * tm
    pos_p = off[bucket_p] + rank_p
    pos_s = off[bucket_s] + cp[bucket_s] + rank_s
    n_tiles = n_rows // tm
    n_used = tile_end[-1]
    ti = jnp.minimum(jnp.arange(n_tiles), n_used - 1)
    tile_bucket = jnp.sum((ti[:, None] >= tile_end[None, :]).astype(jnp.int32), axis=1)
    pair = np.array(EXPERT_PAIRS, np.int32)
    grp = tile_bucket // len(EXPERT_PAIRS)
    pidx = tile_bucket % len(EXPERT_PAIRS)
    tile_ea = grp * EXP_PER_GROUP + jnp.asarray(pair[:, 0])[pidx]
    tile_eb = grp * EXP_PER_GROUP + jnp.asarray(pair[:, 1])[pidx]
    tile_valid = (jnp.arange(n_tiles) < n_used).astype(jnp.int32)
    return pos_p, pos_s, tile_ea.astype(jnp.int32), tile_eb.astype(jnp.int32), tile_valid


def kernel(x_prompt, x_sample, c_prompt, c_sample, cache_k, cache_v, state_pool, page_table, rel_bias, w_ada, b_ada, w_in, lambda_q1, lambda_k1, lambda_q2, lambda_k2, subln_g, w_pool, pool_scale, w_o, ln1_g, ln1_b, w_router_group, b_router_group, w_router_expert, b_router_expert, w_gate, w_up, w_down, ln2_g, ln2_b):
    depth = w_in.shape[0]
    nb, seq, d = x_prompt.shape
    n_seq, n_new, _ = x_sample.shape
    n_pages = page_table.shape[1]
    page = cache_k.shape[2]
    past_len = n_pages * page
    aw = N_HEADS * V_DIM
    pw = pool_scale.shape[1]
    alpha = (2 * depth) ** 0.25
    tq = min(ATTN_TILE, seq)
    assert tq >= MAX_DISTANCE and seq % tq == 0 and seq % min(SEQ_TILE, seq) == 0
    assert past_len >= POOL_HIST and n_new <= 8
    n_tok_p = nb * seq
    n_tok_s = n_seq * n_new
    assert n_tok_p % min(SEQ_TILE, n_tok_p) == 0
    n_rows = ((n_tok_p + n_tok_s) // EXPERT_TILE + N_PAIR_BUCKETS) * EXPERT_TILE
    pps = math.gcd(PAGES_PER_STEP, n_pages)
    n_steps = n_pages // pps

    bias_diag, bias_sub, c_far = _prompt_bias_tables(rel_bias, tq)
    bias_past, bias_new = _sample_bias_tables(rel_bias, past_len, n_new, 8, n_steps)
    c_all = jnp.concatenate([c_prompt, c_sample], axis=0)
    xs_tm = jnp.transpose(x_sample, (1, 0, 2)).reshape(n_tok_s, d)

    xp, xs_cur = x_prompt, xs_tm
    kp_l, vp_l, pp_l, ks_l, vs_l, ps_l = [], [], [], [], [], []
    for l in range(depth):
        lam_init = 0.8 - 0.6 * math.exp(-0.3 * l)
        lam = (jnp.exp(jnp.sum(lambda_q1[l].astype(F32) * lambda_k1[l].astype(F32)))
               - jnp.exp(jnp.sum(lambda_q2[l].astype(F32) * lambda_k2[l].astype(F32))) + lam_init).reshape(1)
        win_b = w_in[l].astype(BF16)
        wpool_b = w_pool[l].astype(BF16)
        wo_b = w_o[l].astype(BF16)
        wr = jnp.concatenate([w_router_group[l], jnp.transpose(w_router_expert[l], (1, 0, 2)).reshape(d, N_EXPERTS)], axis=1)
        wr_b = jnp.pad(wr, ((0, 0), (0, LANES - wr.shape[1]))).astype(BF16)
        br = jnp.pad(jnp.concatenate([b_router_group[l], b_router_expert[l].reshape(-1)]).astype(F32),
                     (0, LANES - N_EXP_GROUPS - N_EXPERTS)).reshape(1, LANES)
        wg_b = w_gate[l].reshape(N_EXPERTS, d, -1).astype(BF16)
        wu_b = w_up[l].reshape(N_EXPERTS, d, -1).astype(BF16)
        wd_b = w_down[l].reshape(N_EXPERTS, -1, d).astype(BF16)
        pscale = pool_scale[l].reshape(1, pw)
        g_sub = subln_g[l].reshape(1, V_DIM)
        ln1g, ln1b = ln1_g[l].reshape(1, d), ln1_b[l].reshape(1, d)
        ln2g, ln2b = ln2_g[l].reshape(1, d), ln2_b[l].reshape(1, d)

        m_all = _modulation(c_all, w_ada[l], b_ada[l]).reshape(nb + n_seq, 6, d)
        mod_p = m_all[:nb]
        mod_s = jnp.tile(jnp.transpose(m_all[nb:], (1, 0, 2)), (1, n_new, 1))

        q_p, k_p, v_p, kb_p, vb_p, pm_p, ph_p = _prompt_inproj(xp, mod_p, win_b, wpool_b, pscale)
        o_p = _prompt_attention(q_p, kb_p, vb_p, bias_diag, bias_sub, c_far, lam, g_sub, lam_init)
        y1_p, tokx_p, route_p, cnt_p = _outproj_router(o_p, pm_p, xp, mod_p, wo_b, ln1g, ln1b, wr_b, br,
                                                       alpha=alpha, per_row=False)

        hist = state_pool[l]
        q_s, k_s, v_s, p_s, pm_s = _sample_inproj(xs_cur, mod_s[0], mod_s[1], win_b, jnp.transpose(hist, (1, 0, 2)),
                                                  wpool_b, pscale, n_seq=n_seq, n_new=n_new, past_len=past_len)
        seq_major = lambda a: jnp.transpose(a.reshape(n_new, n_seq, -1), (1, 0, 2))
        pad8 = lambda a: jnp.pad(seq_major(a), ((0, 0), (0, 8 - n_new), (0, 0)))
        o_s = _paged_attention(page_table, lam, seq_major(q_s), pad8(k_s), pad8(v_s), bias_past, bias_new, g_sub,
                               cache_k, cache_v, layer=l, lam_init=lam_init)
        o_s_tm = jnp.transpose(o_s, (1, 0, 2)).reshape(1, n_tok_s, aw)
        y1_s, tokx_s, route_s, cnt_s = _outproj_router(o_s_tm, pm_s[None], xs_cur[None], mod_s, wo_b, ln1g, ln1b,
                                                       wr_b, br, alpha=alpha, per_row=True)

        pos_p, pos_s, tile_ea, tile_eb, tile_valid = _routing_plan(route_p, cnt_p, route_s, cnt_s, n_rows)
        xs_sorted = jnp.zeros((n_rows, d + ROUTE_EXTRA), F32)
        xs_sorted = _dispatch(pos_p, tokx_p.reshape(n_tok_p, -1), xs_sorted)
        xs_sorted = _dispatch(pos_s, tokx_s.reshape(n_tok_s, -1), xs_sorted)
        o_sorted = _experts(tile_ea, tile_eb, tile_valid, xs_sorted, wg_b, wu_b, wd_b)
        xp = _combine(pos_p, y1_p, mod_p, ln2g, ln2b, o_sorted, alpha=alpha, per_row=False)
        xs_cur = _combine(pos_s, y1_s, mod_s, ln2g, ln2b, o_sorted, alpha=alpha, per_row=True)[0]

        kp_l.append(k_p)
        vp_l.append(v_p)
        pp_l.append(ph_p)
        ks_l.append(seq_major(k_s).reshape(n_seq, n_new, N_HEADS, V_DIM))
        vs_l.append(seq_major(v_s).reshape(n_seq, n_new, N_HEADS, V_DIM))
        ps_l.append(jnp.concatenate([hist, seq_major(p_s)], axis=1)[:, -POOL_HIST:])

    y_sample = jnp.transpose(xs_cur.reshape(n_new, n_seq, d), (1, 0, 2))
    return (xp, y_sample, jnp.stack(kp_l), jnp.stack(vp_l), jnp.stack(pp_l),
            jnp.stack(ks_l), jnp.stack(vs_l), jnp.stack(ps_l))
```

```python
import functools
import math

import numpy as np
import jax
import jax.numpy as jnp
from jax import lax
from jax.experimental import pallas as pl
from jax.experimental.pallas import tpu as pltpu

F32 = jnp.float32
BF16 = jnp.bfloat16

N_HEADS = 4
HEAD_DIM = 64
V_DIM = 2 * HEAD_DIM
POOL_WINDOWS = (2, 4, 8, 16)
POOL_GROUP = 128
POOL_HIST = max(POOL_WINDOWS) - 1
N_BUCKETS = 32
MAX_DISTANCE = 128
N_EXP_GROUPS = 4
EXP_PER_GROUP = 4
N_EXPERTS = N_EXP_GROUPS * EXP_PER_GROUP
EXPERT_PAIRS = ((0, 1), (0, 2), (0, 3), (1, 2), (1, 3), (2, 3))
N_PAIR_BUCKETS = N_EXP_GROUPS * len(EXPERT_PAIRS)
LN_EPS = 1e-5
RMS_EPS = 1e-5

LANES = 128
POOL_HALO = 16
VMEM_LIMIT_BYTES = 48 * 1024 * 1024
SEQ_TILE = 512
ATTN_TILE = 256
EXPERT_TILE = 256
PAGES_PER_STEP = 8
ROUTE_EXTRA = LANES
LOG2E = math.log2(math.e)
Q_SCALE = HEAD_DIM ** -0.5 * LOG2E


def _params(*sem):
    return pltpu.CompilerParams(dimension_semantics=sem, vmem_limit_bytes=VMEM_LIMIT_BYTES)


def _dot(a, b):
    return jnp.dot(a, b, preferred_element_type=F32)


def _dot_nt(a, b):
    return lax.dot_general(a, b, (((1,), (1,)), ((), ())), preferred_element_type=F32)


def _sigmoid(x):
    return 1.0 / (1.0 + jnp.exp(-x))


def _mod_kernel(c_ref, w_ref, b_ref, o_ref):
    c = c_ref[...]
    s = c * _sigmoid(c)
    s_hi = s.astype(BF16)
    s_lo = (s - s_hi.astype(F32)).astype(BF16)
    w = w_ref[...]
    w_hi = w.astype(BF16)
    w_lo = (w - w_hi.astype(F32)).astype(BF16)
    o_ref[...] = _dot(s_hi, w_hi) + _dot(s_lo, w_hi) + _dot(s_hi, w_lo) + b_ref[...]


def _modulation(c_all, w_ada, b_ada):
    n, d = c_all.shape
    e = w_ada.shape[1]
    bn = 1024
    return pl.pallas_call(
        _mod_kernel,
        grid=(e // bn,),
        in_specs=[pl.BlockSpec((n, d), lambda j: (0, 0)),
                  pl.BlockSpec((d, bn), lambda j: (0, j)),
                  pl.BlockSpec((1, bn), lambda j: (0, j))],
        out_specs=pl.BlockSpec((n, bn), lambda j: (0, j)),
        out_shape=jax.ShapeDtypeStruct((n, e), F32),
        compiler_params=_params("arbitrary"),
        name="modulation",
    )(c_all, w_ada, b_ada.reshape(1, e))


def _pool_mixed(ext, g, w, inv_cnt, rows):
    eg = ext[:, g * POOL_GROUP:(g + 1) * POOL_GROUP]
    s = eg
    step = 1
    while step < w:
        s = s + pltpu.roll(s, step, 0)
        step *= 2
    return s[POOL_HALO:POOL_HALO + rows] * inv_cnt - eg[POOL_HALO:POOL_HALO + rows]


def _inproj_kernel(x_ref, mod_ref, win_ref, wpool_ref, pscale_ref,
                   q_ref, k_ref, v_ref, kb_ref, vb_ref, pm_ref, ph_ref, ext_ref, *, ts, aw):
    i = pl.program_id(1)

    @pl.when(i == 0)
    def _():
        ext_ref[0:POOL_HALO, :] = jnp.zeros((POOL_HALO, ext_ref.shape[1]), F32)

    @pl.when(i > 0)
    def _():
        ext_ref[0:POOL_HALO, :] = ext_ref[ts:ts + POOL_HALO, :]

    shift = mod_ref[0, 0:1, :]
    scale = mod_ref[0, 1:2, :]
    u = (x_ref[0] * (1.0 + scale) + shift).astype(BF16)
    acc = _dot(u, win_ref[...])
    q_ref[0] = (acc[:, 0:aw] * Q_SCALE).astype(BF16)
    k = acc[:, aw:2 * aw]
    v = acc[:, 2 * aw:3 * aw]
    for h in range(N_HEADS):
        k_ref[0, :, h, :] = k[:, h * V_DIM:(h + 1) * V_DIM]
        v_ref[0, :, h, :] = v[:, h * V_DIM:(h + 1) * V_DIM]
    kb_ref[0] = k.astype(BF16)
    vb_ref[0] = v.astype(BF16)
    p = acc[:, 3 * aw:]
    ext_ref[POOL_HALO:POOL_HALO + ts, :] = p
    ext = ext_ref[...]
    pos = i * ts + lax.broadcasted_iota(jnp.int32, (ts, 1), 0)
    for g, w in enumerate(POOL_WINDOWS):
        inv_cnt = 1.0 / jnp.minimum(w, pos + 1).astype(F32)
        mixed = _pool_mixed(ext, g, w, inv_cnt, ts)
        cols = slice(g * POOL_GROUP, (g + 1) * POOL_GROUP)
        pm_ref[0, :, cols] = (_dot(mixed.astype(BF16), wpool_ref[g]) * pscale_ref[:, cols]).astype(BF16)

    @pl.when(i == pl.num_programs(1) - 1)
    def _():
        ph_ref[0] = p[ts - POOL_HIST:ts, :]


def _prompt_inproj(x, mod, win_b, wpool_b, pscale):
    nb, s, d = x.shape
    ew = win_b.shape[1]
    aw = (ew - 4 * POOL_GROUP) // 3
    pw = ew - 3 * aw
    ts = min(SEQ_TILE, s)
    nt = s // ts
    tile = lambda b, i: (b, i, 0)
    const2 = lambda b, i: (0, 0)
    return pl.pallas_call(
        functools.partial(_inproj_kernel, ts=ts, aw=aw),
        grid=(nb, nt),
        in_specs=[pl.BlockSpec((1, ts, d), tile),
                  pl.BlockSpec((1, 6, d), lambda b, i: (b, 0, 0)),
                  pl.BlockSpec((d, ew), const2),
                  pl.BlockSpec(wpool_b.shape, lambda b, i: (0, 0, 0)),
                  pl.BlockSpec((1, pw), const2)],
        out_specs=[pl.BlockSpec((1, ts, aw), tile),
                   pl.BlockSpec((1, ts, N_HEADS, V_DIM), lambda b, i: (b, i, 0, 0)),
                   pl.BlockSpec((1, ts, N_HEADS, V_DIM), lambda b, i: (b, i, 0, 0)),
                   pl.BlockSpec((1, ts, aw), tile), pl.BlockSpec((1, ts, aw), tile),
                   pl.BlockSpec((1, ts, pw), tile),
                   pl.BlockSpec((1, POOL_HIST, pw), lambda b, i: (b, 0, 0))],
        out_shape=[jax.ShapeDtypeStruct((nb, s, aw), BF16),
                   jax.ShapeDtypeStruct((nb, s, N_HEADS, V_DIM), F32),
                   jax.ShapeDtypeStruct((nb, s, N_HEADS, V_DIM), F32),
                   jax.ShapeDtypeStruct((nb, s, aw), BF16),
                   jax.ShapeDtypeStruct((nb, s, aw), BF16),
                   jax.ShapeDtypeStruct((nb, s, pw), BF16),
                   jax.ShapeDtypeStruct((nb, POOL_HIST, pw), F32)],
        scratch_shapes=[pltpu.VMEM((ts + POOL_HALO, pw), F32)],
        compiler_params=_params("arbitrary", "arbitrary"),
        name="prompt_inproj",
    )(x, mod, win_b, wpool_b, pscale)


def _split_branches(qh):
    lane = lax.broadcasted_iota(jnp.int32, qh.shape, 1)
    zero = jnp.zeros_like(qh)
    return jnp.concatenate([jnp.where(lane < HEAD_DIM, qh, zero), jnp.where(lane >= HEAD_DIM, qh, zero)], axis=0)


def _lane_tile(a, width):
    return jnp.concatenate([a] * (width // LANES), axis=1) if width > LANES else a


def _online_update(s, v_tile, m_ref, l_ref, acc_ref, first=False):
    rows, width = s.shape
    m_new = jnp.broadcast_to(jnp.max(s, axis=1, keepdims=True), (rows, LANES))
    if not first:
        m_old = m_ref[...]
        m_new = jnp.maximum(m_old, m_new)
    p = jnp.exp2(s - _lane_tile(m_new, width))
    pv = _dot(p.astype(BF16), v_tile)
    l_new = jnp.broadcast_to(jnp.sum(p, axis=1, keepdims=True), (rows, LANES))
    if first:
        acc_ref[...] = pv
        l_ref[...] = l_new
    else:
        alpha = jnp.exp2(m_old - m_new)
        acc_ref[...] = _lane_tile(alpha, pv.shape[1]) * acc_ref[...] + pv
        l_ref[...] = alpha * l_ref[...] + l_new
    m_ref[...] = m_new


def _diff_head_out(acc, l, t, lam, g, lam_init):
    o = acc[0:t] / l[0:t] - lam * (acc[t:2 * t] / l[t:2 * t])
    o = o * lax.rsqrt(jnp.mean(o * o, axis=-1, keepdims=True) + RMS_EPS) * g
    return o * (1.0 - lam_init)


def _attn_kernel(cfar_ref, lam_ref, q_ref, k_ref, v_ref, bd_ref, be_ref, g_ref, o_ref,
                 qs_sc, m_sc, l_sc, acc_sc, *, tq, lam_init):
    qi = pl.program_id(1)
    heads = [slice(h * V_DIM, (h + 1) * V_DIM) for h in range(N_HEADS)]
    for h, cols in enumerate(heads):
        qs_sc[h] = _split_branches(q_ref[0, :, cols])

    def step(j, bias, first):
        rows = pl.ds(pl.multiple_of(j * tq, tq), tq)
        for h, cols in enumerate(heads):
            s = _dot_nt(qs_sc[h], k_ref[0, rows, cols]) + bias(h)
            _online_update(s, v_ref[0, rows, cols], m_sc.at[h], l_sc.at[h], acc_sc.at[h], first=first)

    step(qi, lambda h: bd_ref[h], True)

    @pl.when(qi >= 1)
    def _():
        step(qi - 1, lambda h: be_ref[h], False)

    def far(j, carry):
        step(j, lambda h: cfar_ref[h], False)
        return carry

    lax.fori_loop(0, jnp.maximum(qi - 1, 0), far, 0)
    lam = lam_ref[0]
    for h, cols in enumerate(heads):
        o = _diff_head_out(acc_sc[h], l_sc[h], tq, lam, g_ref[...], lam_init)
        o_ref[0, :, cols] = o.astype(BF16)


def _prompt_attention(q, kb, vb, bias_diag, bias_sub, c_far, lam, subln_g, lam_init):
    nb, s, aw = q.shape
    tq = bias_diag.shape[2]
    smem = pl.BlockSpec(memory_space=pltpu.SMEM)
    const3 = lambda b, i: (0, 0, 0)
    return pl.pallas_call(
        functools.partial(_attn_kernel, tq=tq, lam_init=lam_init),
        grid=(nb, s // tq),
        in_specs=[smem, smem,
                  pl.BlockSpec((1, tq, aw), lambda b, i: (b, i, 0)),
                  pl.BlockSpec((1, s, aw), lambda b, i: (b, 0, 0)),
                  pl.BlockSpec((1, s, aw), lambda b, i: (b, 0, 0)),
                  pl.BlockSpec(bias_diag.shape, const3),
                  pl.BlockSpec(bias_sub.shape, const3),
                  pl.BlockSpec((1, V_DIM), lambda b, i: (0, 0))],
        out_specs=pl.BlockSpec((1, tq, aw), lambda b, i: (b, i, 0)),
        out_shape=jax.ShapeDtypeStruct((nb, s, aw), BF16),
        scratch_shapes=[pltpu.VMEM((N_HEADS, 2 * tq, V_DIM), BF16),
                        pltpu.VMEM((N_HEADS, 2 * tq, LANES), F32), pltpu.VMEM((N_HEADS, 2 * tq, LANES), F32),
                        pltpu.VMEM((N_HEADS, 2 * tq, V_DIM), F32)],
        compiler_params=_params("arbitrary", "arbitrary"),
        name="prompt_attention",
    )(c_far, lam, q, kb, vb, bias_diag, bias_sub, subln_g)


def _layer_norm(z, g, b):
    mu = jnp.mean(z, axis=-1, keepdims=True)
    zc = z - mu
    var = jnp.mean(zc * zc, axis=-1, keepdims=True)
    return zc * lax.rsqrt(var + LN_EPS) * g + b


def _mod_get(mod_ref, j, per_row):
    return mod_ref[j] if per_row else mod_ref[0, j:j + 1, :]


def _outproj_router_kernel(o_ref, pm_ref, x_ref, mod_ref, wo_ref, lng_ref, lnb_ref, wr_ref, br_ref,
                           y_ref, tokx_ref, route_ref, cnt_ref, carry_sc, *, alpha, per_row):
    t = x_ref.shape[1]
    d = x_ref.shape[2]
    aw = o_ref.shape[2]

    @pl.when((pl.program_id(0) == 0) & (pl.program_id(1) == 0))
    def _():
        carry_sc[...] = jnp.zeros(carry_sc.shape, F32)

    h = _dot(o_ref[0], wo_ref[0:aw, :]) + _dot(pm_ref[0], wo_ref[aw:, :])
    gate1 = _mod_get(mod_ref, 2, per_row)
    y1 = _layer_norm(alpha * x_ref[0] + gate1 * h, lng_ref[...], lnb_ref[...])
    y_ref[0] = y1
    tok = y1 * (1.0 + _mod_get(mod_ref, 4, per_row)) + _mod_get(mod_ref, 3, per_row)
    tokx_ref[0, :, 0:d] = tok

    logits = _dot(tok.astype(BF16), wr_ref[...]) + br_ref[...]
    lane = lax.broadcasted_iota(jnp.int32, logits.shape, 1)
    lane_f = lane.astype(F32)
    neg = jnp.full_like(logits, -jnp.inf)

    def first_argmax(vals, vmax):
        return jnp.min(jnp.where(vals == vmax, lane_f, float(LANES)), axis=1, keepdims=True).astype(jnp.int32)

    gl = jnp.where(lane < N_EXP_GROUPS, logits, neg)
    gmax = jnp.max(gl, axis=1, keepdims=True)
    gidx = first_argmax(gl, gmax)
    g_w = 1.0 / jnp.sum(jnp.exp(gl - gmax), axis=1, keepdims=True)
    lo_lane = N_EXP_GROUPS + EXP_PER_GROUP * gidx
    el = jnp.where((lane >= lo_lane) & (lane < lo_lane + EXP_PER_GROUP), logits, neg)
    v1 = jnp.max(el, axis=1, keepdims=True)
    i1 = first_argmax(el, v1)
    el2 = jnp.where(lane == i1, neg, el)
    v2 = jnp.max(el2, axis=1, keepdims=True)
    i2 = first_argmax(el2, v2)
    e21 = jnp.exp(v2 - v1)
    w1 = g_w / (1.0 + e21)
    w2 = g_w * e21 / (1.0 + e21)
    first_lo = i1 < i2
    cw_lo = jnp.where(first_lo, w1, w2)
    cw_hi = jnp.where(first_lo, w2, w1)
    a = jnp.minimum(i1, i2) - lo_lane
    b = jnp.maximum(i1, i2) - lo_lane
    pair_base = jnp.where(a == 0, 0, jnp.where(a == 1, 3, 5))
    bucket = gidx * len(EXPERT_PAIRS) + pair_base + (b - a - 1)

    xlane = lax.broadcasted_iota(jnp.int32, (t, ROUTE_EXTRA), 1)
    tokx_ref[0, :, d:] = jnp.where(xlane == 0, cw_lo, jnp.where(xlane == 1, cw_hi, 0.0))

    onehot = lane == bucket
    row = lax.broadcasted_iota(jnp.int32, (t, t), 0)
    col = lax.broadcasted_iota(jnp.int32, (t, t), 1)
    ltri = jnp.where(col < row, 1.0, 0.0).astype(BF16)
    prefix = _dot(ltri, jnp.where(onehot, 1.0, 0.0).astype(BF16)) + carry_sc[...]
    rank = jnp.sum(jnp.where(onehot, prefix, 0.0), axis=1, keepdims=True)
    carry_sc[...] = carry_sc[...] + jnp.sum(jnp.where(onehot, 1.0, 0.0), axis=0, keepdims=True)
    cnt_ref[...] = carry_sc[...]
    route_ref[0] = jnp.where(lane == 0, bucket.astype(F32), jnp.where(lane == 1, rank, 0.0))


def _outproj_router(o, pm, x, mod, wo_b, ln_g, ln_b, wr_b, br, *, alpha, per_row):
    nb, s, d = x.shape
    aw = o.shape[2]
    pw = pm.shape[2]
    t = min(SEQ_TILE, s)
    tile = lambda b, i: (b, i, 0)
    const2 = lambda b, i: (0, 0)
    mod_spec = (pl.BlockSpec(mod.shape, lambda b, i: (0, 0, 0)) if per_row
                else pl.BlockSpec((1, 6, d), lambda b, i: (b, 0, 0)))
    return pl.pallas_call(
        functools.partial(_outproj_router_kernel, alpha=alpha, per_row=per_row),
        grid=(nb, s // t),
        in_specs=[pl.BlockSpec((1, t, aw), tile), pl.BlockSpec((1, t, pw), tile), pl.BlockSpec((1, t, d), tile),
                  mod_spec,
                  pl.BlockSpec(wo_b.shape, const2), pl.BlockSpec((1, d), const2), pl.BlockSpec((1, d), const2),
                  pl.BlockSpec(wr_b.shape, const2), pl.BlockSpec((1, LANES), const2)],
        out_specs=[pl.BlockSpec((1, t, d), tile), pl.BlockSpec((1, t, d + ROUTE_EXTRA), tile),
                   pl.BlockSpec((1, t, LANES), tile), pl.BlockSpec((1, LANES), const2)],
        out_shape=[jax.ShapeDtypeStruct((nb, s, d), F32), jax.ShapeDtypeStruct((nb, s, d + ROUTE_EXTRA), F32),
                   jax.ShapeDtypeStruct((nb, s, LANES), F32), jax.ShapeDtypeStruct((1, LANES), F32)],
        scratch_shapes=[pltpu.VMEM((1, LANES), F32)],
        compiler_params=_params("arbitrary", "arbitrary"),
        name="outproj_router",
    )(o, pm, x, mod, wo_b, ln_g, ln_b, wr_b, br)


def _row_copy(src, dst, sem):
    return pltpu.make_async_copy(src, dst, sem)


def _dispatch_kernel(pos_ref, tok_ref, xs_in_ref, xs_ref, sem, *, t):
    del xs_in_ref
    base = pl.program_id(0) * t

    def issue(r, carry):
        _row_copy(tok_ref.at[pl.ds(r, 1)], xs_ref.at[pl.ds(pos_ref[base + r], 1)], sem).start()
        return carry

    lax.fori_loop(0, t, issue, 0, unroll=8)

    def drain(r, carry):
        _row_copy(tok_ref.at[pl.ds(0, 1)], xs_ref.at[pl.ds(0, 1)], sem).wait()
        return carry

    lax.fori_loop(0, t, drain, 0, unroll=8)


def _dispatch(pos, tokx, xs):
    n, w = tokx.shape
    t = min(SEQ_TILE, n)
    return pl.pallas_call(
        functools.partial(_dispatch_kernel, t=t),
        grid_spec=pltpu.PrefetchScalarGridSpec(
            num_scalar_prefetch=1,
            grid=(n // t,),
            in_specs=[pl.BlockSpec((t, w), lambda i, pos: (i, 0)),
                      pl.BlockSpec(memory_space=pl.ANY)],
            out_specs=pl.BlockSpec(memory_space=pl.ANY),
            scratch_shapes=[pltpu.SemaphoreType.DMA(())]),
        out_shape=jax.ShapeDtypeStruct(xs.shape, xs.dtype),
        input_output_aliases={2: 0},
        compiler_params=_params("arbitrary"),
        name="dispatch_rows",
    )(pos, tokx, xs)


def _expert_kernel(ea_ref, eb_ref, valid_ref, xs_ref, wga_ref, wua_ref, wda_ref, wgb_ref, wub_ref, wdb_ref,
                   o_ref, *, d):
    del ea_ref, eb_ref
    ti = pl.program_id(0)

    @pl.when(valid_ref[ti] == 1)
    def _():
        xs = xs_ref[...]
        x = xs[:, 0:d].astype(BF16)

        def ffn(wg_ref, wu_ref, wd_ref, cw):
            g = _dot(x, wg_ref[0])
            u = _dot(x, wu_ref[0])
            hid = (g * _sigmoid(g)) * u * cw
            return _dot(hid.astype(BF16), wd_ref[0])

        o_ref[...] = (ffn(wga_ref, wua_ref, wda_ref, xs[:, d:d + 1])
                      + ffn(wgb_ref, wub_ref, wdb_ref, xs[:, d + 1:d + 2]))

    @pl.when(valid_ref[ti] == 0)
    def _():
        o_ref[...] = jnp.zeros(o_ref.shape, F32)


def _experts(tile_ea, tile_eb, tile_valid, xs, wg_b, wu_b, wd_b):
    r, w = xs.shape
    d = w - ROUTE_EXTRA
    f = wg_b.shape[2]
    tm = EXPERT_TILE
    wa = lambda blk: pl.BlockSpec(blk, lambda i, ea, eb, va: (ea[i], 0, 0))
    wb = lambda blk: pl.BlockSpec(blk, lambda i, ea, eb, va: (eb[i], 0, 0))
    return pl.pallas_call(
        functools.partial(_expert_kernel, d=d),
        grid_spec=pltpu.PrefetchScalarGridSpec(
            num_scalar_prefetch=3,
            grid=(r // tm,),
            in_specs=[pl.BlockSpec((tm, w), lambda i, ea, eb, va: (i, 0)),
                      wa((1, d, f)), wa((1, d, f)), wa((1, f, d)),
                      wb((1, d, f)), wb((1, d, f)), wb((1, f, d))],
            out_specs=pl.BlockSpec((tm, d), lambda i, ea, eb, va: (i, 0))),
        out_shape=jax.ShapeDtypeStruct((r, d), F32),
        compiler_params=_params("arbitrary"),
        name="expert_ffn",
    )(tile_ea, tile_eb, tile_valid, xs, wg_b, wu_b, wd_b, wg_b, wu_b, wd_b)


def _combine_kernel(pos_ref, y_ref, mod_ref, lng_ref, lnb_ref, os_ref, out_ref, buf, sem, *, alpha, per_row):
    t = y_ref.shape[1]
    base = (pl.program_id(0) * pl.num_programs(1) + pl.program_id(1)) * t

    def issue(r, carry):
        _row_copy(os_ref.at[pl.ds(pos_ref[base + r], 1)], buf.at[pl.ds(r, 1)], sem).start()
        return carry

    lax.fori_loop(0, t, issue, 0, unroll=8)

    def drain(r, carry):
        _row_copy(os_ref.at[pl.ds(0, 1)], buf.at[pl.ds(0, 1)], sem).wait()
        return carry

    lax.fori_loop(0, t, drain, 0, unroll=8)
    gate2 = _mod_get(mod_ref, 5, per_row)
    out_ref[0] = _layer_norm(alpha * y_ref[0] + gate2 * buf[...], lng_ref[...], lnb_ref[...])


def _combine(pos, y1, mod, ln_g, ln_b, o_sorted, *, alpha, per_row):
    nb, s, d = y1.shape
    t = min(SEQ_TILE, s)
    tile = lambda b, i, pos: (b, i, 0)
    const2 = lambda b, i, pos: (0, 0)
    mod_spec = (pl.BlockSpec(mod.shape, lambda b, i, pos: (0, 0, 0)) if per_row
                else pl.BlockSpec((1, 6, d), lambda b, i, pos: (b, 0, 0)))
    return pl.pallas_call(
        functools.partial(_combine_kernel, alpha=alpha, per_row=per_row),
        grid_spec=pltpu.PrefetchScalarGridSpec(
            num_scalar_prefetch=1,
            grid=(nb, s // t),
            in_specs=[pl.BlockSpec((1, t, d), tile), mod_spec,
                      pl.BlockSpec((1, d), const2), pl.BlockSpec((1, d), const2),
                      pl.BlockSpec(memory_space=pl.ANY)],
            out_specs=pl.BlockSpec((1, t, d), tile),
            scratch_shapes=[pltpu.VMEM((t, d), F32), pltpu.SemaphoreType.DMA(())]),
        out_shape=jax.ShapeDtypeStruct((nb, s, d), F32),
        compiler_params=_params("arbitrary", "arbitrary"),
        name="combine_rows",
    )(pos, y1, mod, ln_g, ln_b, o_sorted)


def _sample_inproj_kernel(x_ref, shift_ref, scale_ref, win_ref, hist_ref, wpool_ref, pscale_ref,
                          q_ref, k_ref, v_ref, p_ref, pm_ref, *, aw, n_seq, n_new, past_len):
    u = (x_ref[...] * (1.0 + scale_ref[...]) + shift_ref[...]).astype(BF16)
    acc = _dot(u, win_ref[...])
    q_ref[...] = (acc[:, 0:aw] * Q_SCALE).astype(BF16)
    k_ref[...] = acc[:, aw:2 * aw]
    v_ref[...] = acc[:, 2 * aw:3 * aw]
    p = acc[:, 3 * aw:]
    p_ref[...] = p
    n_hist = hist_ref.shape[0]
    rows = [hist_ref[j] for j in range(n_hist)] + [p[t * n_seq:(t + 1) * n_seq] for t in range(n_new)]
    for g, w in enumerate(POOL_WINDOWS):
        cols = slice(g * POOL_GROUP, (g + 1) * POOL_GROUP)
        mixed = []
        for t in range(n_new):
            end = n_hist + t + 1
            start = max(end - w, 0)
            win_sum = rows[start][:, cols]
            for j in range(start + 1, end):
                win_sum = win_sum + rows[j][:, cols]
            count = float(min(w, past_len + t + 1))
            mixed.append(win_sum / count - rows[n_hist + t][:, cols])
        mixed = jnp.concatenate(mixed, axis=0)
        pm_ref[:, cols] = (_dot(mixed.astype(BF16), wpool_ref[g]) * pscale_ref[:, cols]).astype(BF16)


def _sample_inproj(x, shift, scale, win_b, hist_t, wpool_b, pscale, *, n_seq, n_new, past_len):
    n, d = x.shape
    ew = win_b.shape[1]
    aw = (ew - 4 * POOL_GROUP) // 3
    pw = ew - 3 * aw
    return pl.pallas_call(
        functools.partial(_sample_inproj_kernel, aw=aw, n_seq=n_seq, n_new=n_new, past_len=past_len),
        out_shape=[jax.ShapeDtypeStruct((n, aw), BF16), jax.ShapeDtypeStruct((n, aw), F32),
                   jax.ShapeDtypeStruct((n, aw), F32), jax.ShapeDtypeStruct((n, pw), F32),
                   jax.ShapeDtypeStruct((n, pw), BF16)],
        compiler_params=pltpu.CompilerParams(vmem_limit_bytes=VMEM_LIMIT_BYTES),
        name="sample_inproj",
    )(x, shift, scale, win_b, hist_t, wpool_b, pscale)


def _paged_attn_kernel(pt_ref, lam_ref, q_ref, kn_ref, vn_ref, bp_ref, bn_ref, g_ref, *rest,
                       pps, page, n_new, lam_init):
    del pt_ref
    ck = rest[0:pps]
    cv = rest[pps:2 * pps]
    o_ref = rest[2 * pps]
    kcat, vcat, knp, vnp, m_sc, l_sc, acc_sc = rest[2 * pps + 1:]
    step = pl.program_id(1)
    last = pl.num_programs(1) - 1
    heads = [slice(h * V_DIM, (h + 1) * V_DIM) for h in range(N_HEADS)]
    for pg in range(pps):
        for h, cols in enumerate(heads):
            kcat[pg * page:(pg + 1) * page, cols] = ck[pg][pl.ds(h, page, stride=N_HEADS), :].astype(BF16)
            vcat[pg * page:(pg + 1) * page, cols] = cv[pg][pl.ds(h, page, stride=N_HEADS), :].astype(BF16)

    @pl.when(step == 0)
    def _():
        m_sc[...] = jnp.full(m_sc.shape, -jnp.inf, F32)
        l_sc[...] = jnp.zeros(l_sc.shape, F32)
        acc_sc[...] = jnp.zeros(acc_sc.shape, F32)

    qn = q_ref[0]
    zero = jnp.zeros((2 * n_new, V_DIM), BF16)
    qall = jnp.concatenate(
        [jnp.concatenate([_split_branches(qn[:, cols]) if hh == h else zero for hh in range(N_HEADS)], axis=1)
         for h, cols in enumerate(heads)], axis=0)
    _online_update(_dot_nt(qall, kcat[...]) + bp_ref[0], vcat[...], m_sc, l_sc, acc_sc)

    @pl.when(step == last)
    def _():
        knp[...] = jnp.zeros(knp.shape, F32)
        vnp[...] = jnp.zeros(vnp.shape, F32)
        knp[0:kn_ref.shape[1], :] = kn_ref[0]
        vnp[0:vn_ref.shape[1], :] = vn_ref[0]
        _online_update(_dot_nt(qall, knp[...].astype(BF16)) + bn_ref[...], vnp[...].astype(BF16), m_sc, l_sc, acc_sc)
        lam = lam_ref[0]
        for h, cols in enumerate(heads):
            rows = slice(h * 2 * n_new, (h + 1) * 2 * n_new)
            o = _diff_head_out(acc_sc[rows, cols], l_sc[rows, :], n_new, lam, g_ref[...], lam_init)
            o_ref[0, :, cols] = o.astype(BF16)


def _page_index(n, s, pt_ref, *, first_page, pg, pps, n_pages):
    return (first_page + pt_ref[n * n_pages + s * pps + pg], 0)


def _paged_attention(page_table, lam, q, k_new, v_new, bias_past, bias_new, subln_g, cache_k, cache_v, *,
                     layer, lam_init):
    n_seq, n_pages = page_table.shape
    n_new, aw = q.shape[1], q.shape[2]
    n_phys, page = cache_k.shape[1], cache_k.shape[2]
    pps = math.gcd(PAGES_PER_STEP, n_pages)
    n_steps = n_pages // pps
    pad_new = k_new.shape[1]
    ck = cache_k.reshape(-1, V_DIM)
    cv = cache_v.reshape(-1, V_DIM)
    page_specs = [pl.BlockSpec((page * N_HEADS, V_DIM),
                               functools.partial(_page_index, first_page=layer * n_phys, pg=pg, pps=pps,
                                                 n_pages=n_pages))
                  for pg in range(pps)]
    per_seq = lambda n, s, pt: (n, 0, 0)
    return pl.pallas_call(
        functools.partial(_paged_attn_kernel, pps=pps, page=page, n_new=n_new, lam_init=lam_init),
        grid_spec=pltpu.PrefetchScalarGridSpec(
            num_scalar_prefetch=1,
            grid=(n_seq, n_steps),
            in_specs=[pl.BlockSpec(memory_space=pltpu.SMEM),
                      pl.BlockSpec((1, n_new, aw), per_seq),
                      pl.BlockSpec((1, pad_new, aw), per_seq),
                      pl.BlockSpec((1, pad_new, aw), per_seq),
                      pl.BlockSpec((1,) + bias_past.shape[1:], lambda n, s, pt: (s, 0, 0)),
                      pl.BlockSpec(bias_new.shape, lambda n, s, pt: (0, 0)),
                      pl.BlockSpec((1, V_DIM), lambda n, s, pt: (0, 0))] + page_specs + page_specs,
            out_specs=pl.BlockSpec((1, n_new, aw), per_seq),
            scratch_shapes=[pltpu.VMEM((pps * page, aw), BF16), pltpu.VMEM((pps * page, aw), BF16),
                            pltpu.VMEM((LANES, aw), F32), pltpu.VMEM((LANES, aw), F32),
                            pltpu.VMEM((N_HEADS * 2 * n_new, LANES), F32),
                            pltpu.VMEM((N_HEADS * 2 * n_new, LANES), F32),
                            pltpu.VMEM((N_HEADS * 2 * n_new, aw), F32)]),
        out_shape=jax.ShapeDtypeStruct((n_seq, n_new, aw), BF16),
        compiler_params=_params("arbitrary", "arbitrary"),
        name="paged_attention",
    )(page_table.reshape(-1), lam, q, k_new, v_new, bias_past, bias_new, subln_g, *([ck] * pps), *([cv] * pps))


def _rel_bias_lookup(rel_bias, dist):
    n = jnp.maximum(dist, 0)
    max_exact = N_BUCKETS // 2
    nf = jnp.maximum(n, 1).astype(F32)
    large = max_exact + (jnp.log(nf / max_exact) / math.log(MAX_DISTANCE / max_exact)
                         * (N_BUCKETS - max_exact)).astype(jnp.int32)
    large = jnp.minimum(large, N_BUCKETS - 1)
    bucket = jnp.where(n < max_exact, n, large)
    onehot = (bucket[..., None] == jnp.arange(N_BUCKETS)).astype(F32)
    return jnp.einsum('...b,bh->...h', onehot, rel_bias.astype(F32), precision=lax.Precision.HIGHEST) * LOG2E


def _prompt_bias_tables(rel_bias, tq):
    i = jnp.arange(tq)[:, None]
    j = jnp.arange(tq)[None, :]
    diag = jnp.where((j <= i)[None], jnp.transpose(_rel_bias_lookup(rel_bias, i - j), (2, 0, 1)), -jnp.inf)
    sub = jnp.transpose(_rel_bias_lookup(rel_bias, tq + i - j), (2, 0, 1))
    both = lambda b: jnp.concatenate([b, b], axis=1)
    return both(diag), both(sub), rel_bias[N_BUCKETS - 1].astype(F32) * LOG2E


def _sample_bias_tables(rel_bias, past_len, n_new, pad_new, n_steps):
    t = jnp.arange(n_new)
    past = jnp.transpose(_rel_bias_lookup(rel_bias, past_len + t[:, None] - jnp.arange(past_len)[None, :]), (2, 0, 1))
    past = jnp.concatenate([past, past], axis=1)
    past = past.reshape(N_HEADS * 2 * n_new, n_steps, past_len // n_steps).transpose(1, 0, 2)
    tn = jnp.arange(LANES)
    new = jnp.transpose(_rel_bias_lookup(rel_bias, t[:, None] - tn[None, :]), (2, 0, 1))
    new = jnp.where((tn[None, :] <= t[:, None])[None], new, -jnp.inf)
    del pad_new
    return past, jnp.concatenate([new, new], axis=1).reshape(N_HEADS * 2 * n_new, LANES)


def _routing_plan(route_p, cnt_p, route_s, cnt_s, n_rows):
    tm = EXPERT_TILE
    bucket_p = route_p[..., 0].reshape(-1).astype(jnp.int32)
    rank_p = route_p[..., 1].reshape(-1).astype(jnp.int32)
    bucket_s = route_s[..., 0].reshape(-1).astype(jnp.int32)
    rank_s = route_s[..., 1].reshape(-1).astype(jnp.int32)
    cp = cnt_p[0, :N_PAIR_BUCKETS].astype(jnp.int32)
    cs = cnt_s[0, :N_PAIR_BUCKETS].astype(jnp.int32)
    tiles = (cp + cs + tm - 1) // tm
    tile_end = jnp.cumsum(tiles)
    off = (tile_end - tiles) * tm
    lookup = lambda table, idx: jnp.sum(
        jnp.where(idx[:, None] == jnp.arange(table.shape[0])[None, :], table[None, :], 0), axis=1)
    pos_p = lookup(off, bucket_p) + rank_p
    pos_s = lookup(off + cp, bucket_s) + rank_s
    n_tiles = n_rows // tm
    n_used = tile_end[-1]
    ti = jnp.minimum(jnp.arange(n_tiles), n_used - 1)
    tile_bucket = jnp.sum((ti[:, None] >= tile_end[None, :]).astype(jnp.int32), axis=1)
    pair = np.array(EXPERT_PAIRS, np.int32)
    base = np.arange(N_PAIR_BUCKETS) // len(EXPERT_PAIRS) * EXP_PER_GROUP
    tile_ea = lookup(jnp.asarray(base + pair[np.arange(N_PAIR_BUCKETS) % len(EXPERT_PAIRS), 0]), tile_bucket)
    tile_eb = lookup(jnp.asarray(base + pair[np.arange(N_PAIR_BUCKETS) % len(EXPERT_PAIRS), 1]), tile_bucket)
    tile_valid = (jnp.arange(n_tiles) < n_used).astype(jnp.int32)
    return pos_p, pos_s, tile_ea.astype(jnp.int32), tile_eb.astype(jnp.int32), tile_valid


def kernel(x_prompt, x_sample, c_prompt, c_sample, cache_k, cache_v, state_pool, page_table, rel_bias, w_ada, b_ada, w_in, lambda_q1, lambda_k1, lambda_q2, lambda_k2, subln_g, w_pool, pool_scale, w_o, ln1_g, ln1_b, w_router_group, b_router_group, w_router_expert, b_router_expert, w_gate, w_up, w_down, ln2_g, ln2_b):
    depth = w_in.shape[0]
    nb, seq, d = x_prompt.shape
    n_seq, n_new, _ = x_sample.shape
    n_pages = page_table.shape[1]
    page = cache_k.shape[2]
    past_len = n_pages * page
    aw = N_HEADS * V_DIM
    pw = pool_scale.shape[1]
    alpha = (2 * depth) ** 0.25
    tq = min(ATTN_TILE, seq)
    assert tq >= MAX_DISTANCE and seq % tq == 0 and seq % min(SEQ_TILE, seq) == 0
    assert past_len >= POOL_HIST and n_new <= 8
    n_tok_p = nb * seq
    n_tok_s = n_seq * n_new
    assert n_tok_p % min(SEQ_TILE, n_tok_p) == 0
    n_rows = ((n_tok_p + n_tok_s) // EXPERT_TILE + N_PAIR_BUCKETS) * EXPERT_TILE
    pps = math.gcd(PAGES_PER_STEP, n_pages)
    n_steps = n_pages // pps

    bias_diag, bias_sub, c_far = _prompt_bias_tables(rel_bias, tq)
    bias_past, bias_new = _sample_bias_tables(rel_bias, past_len, n_new, 8, n_steps)
    c_all = jnp.concatenate([c_prompt, c_sample], axis=0)
    xs_tm = jnp.transpose(x_sample, (1, 0, 2)).reshape(n_tok_s, d)

    xp, xs_cur = x_prompt, xs_tm
    kp_l, vp_l, pp_l, ks_l, vs_l, ps_l = [], [], [], [], [], []
    for l in range(depth):
        lam_init = 0.8 - 0.6 * math.exp(-0.3 * l)
        lam = (jnp.exp(jnp.sum(lambda_q1[l].astype(F32) * lambda_k1[l].astype(F32)))
               - jnp.exp(jnp.sum(lambda_q2[l].astype(F32) * lambda_k2[l].astype(F32))) + lam_init).reshape(1)
        win_b = w_in[l].astype(BF16)
        wpool_b = w_pool[l].astype(BF16)
        wo_b = w_o[l].astype(BF16)
        wr = jnp.concatenate([w_router_group[l], jnp.transpose(w_router_expert[l], (1, 0, 2)).reshape(d, N_EXPERTS)], axis=1)
        wr_b = jnp.pad(wr, ((0, 0), (0, LANES - wr.shape[1]))).astype(BF16)
        br = jnp.pad(jnp.concatenate([b_router_group[l], b_router_expert[l].reshape(-1)]).astype(F32),
                     (0, LANES - N_EXP_GROUPS - N_EXPERTS)).reshape(1, LANES)
        wg_b = w_gate[l].reshape(N_EXPERTS, d, -1).astype(BF16)
        wu_b = w_up[l].reshape(N_EXPERTS, d, -1).astype(BF16)
        wd_b = w_down[l].reshape(N_EXPERTS, -1, d).astype(BF16)
        pscale = pool_scale[l].reshape(1, pw)
        g_sub = subln_g[l].reshape(1, V_DIM)
        ln1g, ln1b = ln1_g[l].reshape(1, d), ln1_b[l].reshape(1, d)
        ln2g, ln2b = ln2_g[l].reshape(1, d), ln2_b[l].reshape(1, d)

        m_all = _modulation(c_all, w_ada[l], b_ada[l]).reshape(nb + n_seq, 6, d)
        mod_p = m_all[:nb]
        mod_s = jnp.tile(jnp.transpose(m_all[nb:], (1, 0, 2)), (1, n_new, 1))

        q_p, k_p, v_p, kb_p, vb_p, pm_p, ph_p = _prompt_inproj(xp, mod_p, win_b, wpool_b, pscale)
        o_p = _prompt_attention(q_p, kb_p, vb_p, bias_diag, bias_sub, c_far, lam, g_sub, lam_init)
        y1_p, tokx_p, route_p, cnt_p = _outproj_router(o_p, pm_p, xp, mod_p, wo_b, ln1g, ln1b, wr_b, br,
                                                       alpha=alpha, per_row=False)

        hist = state_pool[l]
        q_s, k_s, v_s, p_s, pm_s = _sample_inproj(xs_cur, mod_s[0], mod_s[1], win_b, jnp.transpose(hist, (1, 0, 2)),
                                                  wpool_b, pscale, n_seq=n_seq, n_new=n_new, past_len=past_len)
        seq_major = lambda a: jnp.transpose(a.reshape(n_new, n_seq, -1), (1, 0, 2))
        pad8 = lambda a: jnp.pad(seq_major(a), ((0, 0), (0, 8 - n_new), (0, 0)))
        o_s = _paged_attention(page_table, lam, seq_major(q_s), pad8(k_s), pad8(v_s), bias_past, bias_new, g_sub,
                               cache_k, cache_v, layer=l, lam_init=lam_init)
        o_s_tm = jnp.transpose(o_s, (1, 0, 2)).reshape(1, n_tok_s, aw)
        y1_s, tokx_s, route_s, cnt_s = _outproj_router(o_s_tm, pm_s[None], xs_cur[None], mod_s, wo_b, ln1g, ln1b,
                                                       wr_b, br, alpha=alpha, per_row=True)

        pos_p, pos_s, tile_ea, tile_eb, tile_valid = _routing_plan(route_p, cnt_p, route_s, cnt_s, n_rows)
        xs_sorted = jnp.zeros((n_rows, d + ROUTE_EXTRA), F32)
        xs_sorted = _dispatch(pos_p, tokx_p.reshape(n_tok_p, -1), xs_sorted)
        xs_sorted = _dispatch(pos_s, tokx_s.reshape(n_tok_s, -1), xs_sorted)
        o_sorted = _experts(tile_ea, tile_eb, tile_valid, xs_sorted, wg_b, wu_b, wd_b)
        xp = _combine(pos_p, y1_p, mod_p, ln2g, ln2b, o_sorted, alpha=alpha, per_row=False)
        xs_cur = _combine(pos_s, y1_s, mod_s, ln2g, ln2b, o_sorted, alpha=alpha, per_row=True)[0]

        kp_l.append(k_p)
        vp_l.append(v_p)
        pp_l.append(ph_p)
        ks_l.append(seq_major(k_s).reshape(n_seq, n_new, N_HEADS, V_DIM))
        vs_l.append(seq_major(v_s).reshape(n_seq, n_new, N_HEADS, V_DIM))
        ps_l.append(jnp.concatenate([hist, seq_major(p_s)], axis=1)[:, -POOL_HIST:])

    y_sample = jnp.transpose(xs_cur.reshape(n_new, n_seq, d), (1, 0, 2))
    return (xp, y_sample, jnp.stack(kp_l), jnp.stack(vp_l), jnp.stack(pp_l),
            jnp.stack(ks_l), jnp.stack(vs_l), jnp.stack(ps_l))
```

```python
import functools
import math

import numpy as np
import jax
import jax.numpy as jnp
from jax import lax
from jax.experimental import pallas as pl
from jax.experimental.pallas import tpu as pltpu

F32 = jnp.float32
BF16 = jnp.bfloat16

N_HEADS = 4
HEAD_DIM = 64
V_DIM = 2 * HEAD_DIM
POOL_WINDOWS = (2, 4, 8, 16)
POOL_GROUP = 128
POOL_HIST = max(POOL_WINDOWS) - 1
N_BUCKETS = 32
MAX_DISTANCE = 128
N_EXP_GROUPS = 4
EXP_PER_GROUP = 4
N_EXPERTS = N_EXP_GROUPS * EXP_PER_GROUP
EXPERT_PAIRS = ((0, 1), (0, 2), (0, 3), (1, 2), (1, 3), (2, 3))
N_PAIR_BUCKETS = N_EXP_GROUPS * len(EXPERT_PAIRS)
LN_EPS = 1e-5
RMS_EPS = 1e-5

LANES = 128
POOL_HALO = 16
VMEM_LIMIT_BYTES = 48 * 1024 * 1024
SEQ_TILE = 512
ATTN_TILE = 256
EXPERT_TILE = 256
PAGES_PER_STEP = 16
ROUTE_EXTRA = LANES
LOG2E = math.log2(math.e)
Q_SCALE = HEAD_DIM ** -0.5 * LOG2E


def _params(*sem):
    return pltpu.CompilerParams(dimension_semantics=sem, vmem_limit_bytes=VMEM_LIMIT_BYTES)


def _dot(a, b):
    return jnp.dot(a, b, preferred_element_type=F32)


def _dot_nt(a, b):
    return lax.dot_general(a, b, (((1,), (1,)), ((), ())), preferred_element_type=F32)


def _sigmoid(x):
    return 1.0 / (1.0 + jnp.exp(-x))


def _mod_kernel(c_ref, w_ref, b_ref, o_ref):
    c = c_ref[...]
    s = c * _sigmoid(c)
    s_hi = s.astype(BF16)
    s_lo = (s - s_hi.astype(F32)).astype(BF16)
    w = w_ref[...]
    w_hi = w.astype(BF16)
    w_lo = (w - w_hi.astype(F32)).astype(BF16)
    o_ref[...] = _dot(s_hi, w_hi) + _dot(s_lo, w_hi) + _dot(s_hi, w_lo) + b_ref[...]


def _modulation(c_all, w_ada, b_ada):
    n, d = c_all.shape
    e = w_ada.shape[1]
    bn = 1024
    return pl.pallas_call(
        _mod_kernel,
        grid=(e // bn,),
        in_specs=[pl.BlockSpec((n, d), lambda j: (0, 0)),
                  pl.BlockSpec((d, bn), lambda j: (0, j)),
                  pl.BlockSpec((1, bn), lambda j: (0, j))],
        out_specs=pl.BlockSpec((n, bn), lambda j: (0, j)),
        out_shape=jax.ShapeDtypeStruct((n, e), F32),
        compiler_params=_params("arbitrary"),
        name="modulation",
    )(c_all, w_ada, b_ada.reshape(1, e))


def _pool_mixed(ext, g, w, inv_cnt, rows):
    eg = ext[:, g * POOL_GROUP:(g + 1) * POOL_GROUP]
    s = eg
    step = 1
    while step < w:
        s = s + pltpu.roll(s, step, 0)
        step *= 2
    return s[POOL_HALO:POOL_HALO + rows] * inv_cnt - eg[POOL_HALO:POOL_HALO + rows]


def _inproj_kernel(x_ref, mod_ref, win_ref, wpool_ref, pscale_ref,
                   q_ref, k_ref, v_ref, kb_ref, vb_ref, pm_ref, ph_ref, ext_ref, *, ts, aw):
    i = pl.program_id(1)

    @pl.when(i == 0)
    def _():
        ext_ref[0:POOL_HALO, :] = jnp.zeros((POOL_HALO, ext_ref.shape[1]), F32)

    @pl.when(i > 0)
    def _():
        ext_ref[0:POOL_HALO, :] = ext_ref[ts:ts + POOL_HALO, :]

    shift = mod_ref[0, 0:1, :]
    scale = mod_ref[0, 1:2, :]
    u = (x_ref[0] * (1.0 + scale) + shift).astype(BF16)
    acc = _dot(u, win_ref[...])
    q_ref[0] = (acc[:, 0:aw] * Q_SCALE).astype(BF16)
    k = acc[:, aw:2 * aw]
    v = acc[:, 2 * aw:3 * aw]
    for h in range(N_HEADS):
        k_ref[0, :, h, :] = k[:, h * V_DIM:(h + 1) * V_DIM]
        v_ref[0, :, h, :] = v[:, h * V_DIM:(h + 1) * V_DIM]
    kb_ref[0] = k.astype(BF16)
    vb_ref[0] = v.astype(BF16)
    p = acc[:, 3 * aw:]
    ext_ref[POOL_HALO:POOL_HALO + ts, :] = p
    ext = ext_ref[...]
    pos = i * ts + lax.broadcasted_iota(jnp.int32, (ts, 1), 0)
    for g, w in enumerate(POOL_WINDOWS):
        inv_cnt = 1.0 / jnp.minimum(w, pos + 1).astype(F32)
        mixed = _pool_mixed(ext, g, w, inv_cnt, ts)
        cols = slice(g * POOL_GROUP, (g + 1) * POOL_GROUP)
        pm_ref[0, :, cols] = (_dot(mixed.astype(BF16), wpool_ref[g]) * pscale_ref[:, cols]).astype(BF16)

    @pl.when(i == pl.num_programs(1) - 1)
    def _():
        ph_ref[0] = p[ts - POOL_HIST:ts, :]


def _prompt_inproj(x, mod, win_b, wpool_b, pscale):
    nb, s, d = x.shape
    ew = win_b.shape[1]
    aw = (ew - 4 * POOL_GROUP) // 3
    pw = ew - 3 * aw
    ts = min(SEQ_TILE, s)
    nt = s // ts
    tile = lambda b, i: (b, i, 0)
    const2 = lambda b, i: (0, 0)
    return pl.pallas_call(
        functools.partial(_inproj_kernel, ts=ts, aw=aw),
        grid=(nb, nt),
        in_specs=[pl.BlockSpec((1, ts, d), tile),
                  pl.BlockSpec((1, 6, d), lambda b, i: (b, 0, 0)),
                  pl.BlockSpec((d, ew), const2),
                  pl.BlockSpec(wpool_b.shape, lambda b, i: (0, 0, 0)),
                  pl.BlockSpec((1, pw), const2)],
        out_specs=[pl.BlockSpec((1, ts, aw), tile),
                   pl.BlockSpec((1, ts, N_HEADS, V_DIM), lambda b, i: (b, i, 0, 0)),
                   pl.BlockSpec((1, ts, N_HEADS, V_DIM), lambda b, i: (b, i, 0, 0)),
                   pl.BlockSpec((1, ts, aw), tile), pl.BlockSpec((1, ts, aw), tile),
                   pl.BlockSpec((1, ts, pw), tile),
                   pl.BlockSpec((1, POOL_HIST, pw), lambda b, i: (b, 0, 0))],
        out_shape=[jax.ShapeDtypeStruct((nb, s, aw), BF16),
                   jax.ShapeDtypeStruct((nb, s, N_HEADS, V_DIM), F32),
                   jax.ShapeDtypeStruct((nb, s, N_HEADS, V_DIM), F32),
                   jax.ShapeDtypeStruct((nb, s, aw), BF16),
                   jax.ShapeDtypeStruct((nb, s, aw), BF16),
                   jax.ShapeDtypeStruct((nb, s, pw), BF16),
                   jax.ShapeDtypeStruct((nb, POOL_HIST, pw), F32)],
        scratch_shapes=[pltpu.VMEM((ts + POOL_HALO, pw), F32)],
        compiler_params=_params("arbitrary", "arbitrary"),
        name="prompt_inproj",
    )(x, mod, win_b, wpool_b, pscale)


def _split_branches(qh):
    lane = lax.broadcasted_iota(jnp.int32, qh.shape, 1)
    zero = jnp.zeros_like(qh)
    return jnp.concatenate([jnp.where(lane < HEAD_DIM, qh, zero), jnp.where(lane >= HEAD_DIM, qh, zero)], axis=0)


def _lane_tile(a, width):
    return jnp.concatenate([a] * (width // LANES), axis=1) if width > LANES else a


def _online_update(s, values, m_ref, l_ref, acc_ref, first=False):
    rows, width = s.shape
    m_new = jnp.broadcast_to(jnp.max(s, axis=1, keepdims=True), (rows, LANES))
    if not first:
        m_old = m_ref[...]
        m_new = jnp.maximum(m_old, m_new)
    p = jnp.exp2(s - _lane_tile(m_new, width))
    pv = values(p.astype(BF16))
    l_new = jnp.broadcast_to(jnp.sum(p, axis=1, keepdims=True), (rows, LANES))
    if first:
        acc_ref[...] = pv
        l_ref[...] = l_new
    else:
        alpha = jnp.exp2(m_old - m_new)
        acc_ref[...] = _lane_tile(alpha, pv.shape[1]) * acc_ref[...] + pv
        l_ref[...] = alpha * l_ref[...] + l_new
    m_ref[...] = m_new


def _diff_head_out(acc, l, t, lam, g, lam_init):
    o = acc[0:t] / l[0:t] - lam * (acc[t:2 * t] / l[t:2 * t])
    o = o * lax.rsqrt(jnp.mean(o * o, axis=-1, keepdims=True) + RMS_EPS) * g
    return o * (1.0 - lam_init)


def _attn_kernel(cfar_ref, lam_ref, q_ref, k_ref, v_ref, bd_ref, be_ref, g_ref, o_ref,
                 qs_sc, m_sc, l_sc, acc_sc, *, tq, lam_init):
    qi = pl.program_id(1)
    heads = [slice(h * V_DIM, (h + 1) * V_DIM) for h in range(N_HEADS)]
    for h, cols in enumerate(heads):
        qs_sc[h] = _split_branches(q_ref[0, :, cols])

    def step(j, bias, first):
        rows = pl.ds(pl.multiple_of(j * tq, tq), tq)
        for h, cols in enumerate(heads):
            s = _dot_nt(qs_sc[h], k_ref[0, rows, cols]) + bias(h)
            _online_update(s, lambda p, cols=cols: _dot(p, v_ref[0, rows, cols]),
                           m_sc.at[h], l_sc.at[h], acc_sc.at[h], first=first)

    step(qi, lambda h: bd_ref[h], True)

    @pl.when(qi >= 1)
    def _():
        step(qi - 1, lambda h: be_ref[h], False)

    def far(j, carry):
        step(j, lambda h: cfar_ref[h], False)
        return carry

    lax.fori_loop(0, jnp.maximum(qi - 1, 0), far, 0)
    lam = lam_ref[0]
    for h, cols in enumerate(heads):
        o = _diff_head_out(acc_sc[h], l_sc[h], tq, lam, g_ref[...], lam_init)
        o_ref[0, :, cols] = o.astype(BF16)


def _prompt_attention(q, kb, vb, bias_diag, bias_sub, c_far, lam, subln_g, lam_init):
    nb, s, aw = q.shape
    tq = bias_diag.shape[2]
    smem = pl.BlockSpec(memory_space=pltpu.SMEM)
    const3 = lambda b, i: (0, 0, 0)
    return pl.pallas_call(
        functools.partial(_attn_kernel, tq=tq, lam_init=lam_init),
        grid=(nb, s // tq),
        in_specs=[smem, smem,
                  pl.BlockSpec((1, tq, aw), lambda b, i: (b, i, 0)),
                  pl.BlockSpec((1, s, aw), lambda b, i: (b, 0, 0)),
                  pl.BlockSpec((1, s, aw), lambda b, i: (b, 0, 0)),
                  pl.BlockSpec(bias_diag.shape, const3),
                  pl.BlockSpec(bias_sub.shape, const3),
                  pl.BlockSpec((1, V_DIM), lambda b, i: (0, 0))],
        out_specs=pl.BlockSpec((1, tq, aw), lambda b, i: (b, i, 0)),
        out_shape=jax.ShapeDtypeStruct((nb, s, aw), BF16),
        scratch_shapes=[pltpu.VMEM((N_HEADS, 2 * tq, V_DIM), BF16),
                        pltpu.VMEM((N_HEADS, 2 * tq, LANES), F32), pltpu.VMEM((N_HEADS, 2 * tq, LANES), F32),
                        pltpu.VMEM((N_HEADS, 2 * tq, V_DIM), F32)],
        compiler_params=_params("arbitrary", "arbitrary"),
        name="prompt_attention",
    )(c_far, lam, q, kb, vb, bias_diag, bias_sub, subln_g)


def _layer_norm(z, g, b):
    mu = jnp.mean(z, axis=-1, keepdims=True)
    zc = z - mu
    var = jnp.mean(zc * zc, axis=-1, keepdims=True)
    return zc * lax.rsqrt(var + LN_EPS) * g + b


def _mod_get(mod_ref, j, per_row):
    return mod_ref[j] if per_row else mod_ref[0, j:j + 1, :]


def _outproj_router_kernel(o_ref, pm_ref, x_ref, mod_ref, wo_ref, lng_ref, lnb_ref, wr_ref, br_ref,
                           y_ref, tokx_ref, route_ref, cnt_ref, carry_sc, *, alpha, per_row):
    t = x_ref.shape[1]
    d = x_ref.shape[2]
    aw = o_ref.shape[2]

    @pl.when((pl.program_id(0) == 0) & (pl.program_id(1) == 0))
    def _():
        carry_sc[...] = jnp.zeros(carry_sc.shape, F32)

    h = _dot(o_ref[0], wo_ref[0:aw, :]) + _dot(pm_ref[0], wo_ref[aw:, :])
    gate1 = _mod_get(mod_ref, 2, per_row)
    y1 = _layer_norm(alpha * x_ref[0] + gate1 * h, lng_ref[...], lnb_ref[...])
    y_ref[0] = y1
    tok = y1 * (1.0 + _mod_get(mod_ref, 4, per_row)) + _mod_get(mod_ref, 3, per_row)
    tokx_ref[0, :, 0:d] = tok

    logits = _dot(tok.astype(BF16), wr_ref[...]) + br_ref[...]
    lane = lax.broadcasted_iota(jnp.int32, logits.shape, 1)
    lane_f = lane.astype(F32)
    neg = jnp.full_like(logits, -jnp.inf)

    def first_argmax(vals, vmax):
        return jnp.min(jnp.where(vals == vmax, lane_f, float(LANES)), axis=1, keepdims=True).astype(jnp.int32)

    gl = jnp.where(lane < N_EXP_GROUPS, logits, neg)
    gmax = jnp.max(gl, axis=1, keepdims=True)
    gidx = first_argmax(gl, gmax)
    g_w = 1.0 / jnp.sum(jnp.exp(gl - gmax), axis=1, keepdims=True)
    lo_lane = N_EXP_GROUPS + EXP_PER_GROUP * gidx
    el = jnp.where((lane >= lo_lane) & (lane < lo_lane + EXP_PER_GROUP), logits, neg)
    v1 = jnp.max(el, axis=1, keepdims=True)
    i1 = first_argmax(el, v1)
    el2 = jnp.where(lane == i1, neg, el)
    v2 = jnp.max(el2, axis=1, keepdims=True)
    i2 = first_argmax(el2, v2)
    e21 = jnp.exp(v2 - v1)
    w1 = g_w / (1.0 + e21)
    w2 = g_w * e21 / (1.0 + e21)
    first_lo = i1 < i2
    cw_lo = jnp.where(first_lo, w1, w2)
    cw_hi = jnp.where(first_lo, w2, w1)
    a = jnp.minimum(i1, i2) - lo_lane
    b = jnp.maximum(i1, i2) - lo_lane
    pair_base = jnp.where(a == 0, 0, jnp.where(a == 1, 3, 5))
    bucket = gidx * len(EXPERT_PAIRS) + pair_base + (b - a - 1)

    xlane = lax.broadcasted_iota(jnp.int32, (t, ROUTE_EXTRA), 1)
    tokx_ref[0, :, d:] = jnp.where(xlane == 0, cw_lo, jnp.where(xlane == 1, cw_hi, 0.0))

    onehot = lane == bucket
    row = lax.broadcasted_iota(jnp.int32, (t, t), 0)
    col = lax.broadcasted_iota(jnp.int32, (t, t), 1)
    ltri = jnp.where(col < row, 1.0, 0.0).astype(BF16)
    prefix = _dot(ltri, jnp.where(onehot, 1.0, 0.0).astype(BF16)) + carry_sc[...]
    rank = jnp.sum(jnp.where(onehot, prefix, 0.0), axis=1, keepdims=True)
    carry_sc[...] = carry_sc[...] + jnp.sum(jnp.where(onehot, 1.0, 0.0), axis=0, keepdims=True)
    cnt_ref[...] = carry_sc[...]
    route_ref[0] = jnp.where(lane == 0, bucket.astype(F32), jnp.where(lane == 1, rank, 0.0))


def _outproj_router(o, pm, x, mod, wo_b, ln_g, ln_b, wr_b, br, *, alpha, per_row):
    nb, s, d = x.shape
    aw = o.shape[2]
    pw = pm.shape[2]
    t = min(SEQ_TILE, s)
    tile = lambda b, i: (b, i, 0)
    const2 = lambda b, i: (0, 0)
    mod_spec = (pl.BlockSpec(mod.shape, lambda b, i: (0, 0, 0)) if per_row
                else pl.BlockSpec((1, 6, d), lambda b, i: (b, 0, 0)))
    return pl.pallas_call(
        functools.partial(_outproj_router_kernel, alpha=alpha, per_row=per_row),
        grid=(nb, s // t),
        in_specs=[pl.BlockSpec((1, t, aw), tile), pl.BlockSpec((1, t, pw), tile), pl.BlockSpec((1, t, d), tile),
                  mod_spec,
                  pl.BlockSpec(wo_b.shape, const2), pl.BlockSpec((1, d), const2), pl.BlockSpec((1, d), const2),
                  pl.BlockSpec(wr_b.shape, const2), pl.BlockSpec((1, LANES), const2)],
        out_specs=[pl.BlockSpec((1, t, d), tile), pl.BlockSpec((1, t, d + ROUTE_EXTRA), tile),
                   pl.BlockSpec((1, t, LANES), tile), pl.BlockSpec((1, LANES), const2)],
        out_shape=[jax.ShapeDtypeStruct((nb, s, d), F32), jax.ShapeDtypeStruct((nb, s, d + ROUTE_EXTRA), F32),
                   jax.ShapeDtypeStruct((nb, s, LANES), F32), jax.ShapeDtypeStruct((1, LANES), F32)],
        scratch_shapes=[pltpu.VMEM((1, LANES), F32)],
        compiler_params=_params("arbitrary", "arbitrary"),
        name="outproj_router",
    )(o, pm, x, mod, wo_b, ln_g, ln_b, wr_b, br)


def _row_copy(src, dst, sem):
    return pltpu.make_async_copy(src, dst, sem)


def _dispatch_kernel(pos_ref, tok_ref, xs_in_ref, xs_ref, sem, *, t):
    del xs_in_ref
    base = pl.program_id(0) * t

    def issue(r, carry):
        _row_copy(tok_ref.at[pl.ds(r, 1)], xs_ref.at[pl.ds(pos_ref[base + r], 1)], sem).start()
        return carry

    lax.fori_loop(0, t, issue, 0, unroll=8)

    def drain(r, carry):
        _row_copy(tok_ref.at[pl.ds(0, 1)], xs_ref.at[pl.ds(0, 1)], sem).wait()
        return carry

    lax.fori_loop(0, t, drain, 0, unroll=8)


def _dispatch(pos, tokx, xs):
    n, w = tokx.shape
    t = min(SEQ_TILE, n)
    return pl.pallas_call(
        functools.partial(_dispatch_kernel, t=t),
        grid_spec=pltpu.PrefetchScalarGridSpec(
            num_scalar_prefetch=1,
            grid=(n // t,),
            in_specs=[pl.BlockSpec((t, w), lambda i, pos: (i, 0)),
                      pl.BlockSpec(memory_space=pl.ANY)],
            out_specs=pl.BlockSpec(memory_space=pl.ANY),
            scratch_shapes=[pltpu.SemaphoreType.DMA(())]),
        out_shape=jax.ShapeDtypeStruct(xs.shape, xs.dtype),
        input_output_aliases={2: 0},
        compiler_params=_params("arbitrary"),
        name="dispatch_rows",
    )(pos, tokx, xs)


def _expert_kernel(ea_ref, eb_ref, valid_ref, xs_ref, wga_ref, wua_ref, wda_ref, wgb_ref, wub_ref, wdb_ref,
                   o_ref, *, d):
    del ea_ref, eb_ref
    ti = pl.program_id(0)

    @pl.when(valid_ref[ti] == 1)
    def _():
        xs = xs_ref[...]
        x = xs[:, 0:d].astype(BF16)

        def ffn(wg_ref, wu_ref, wd_ref, cw):
            g = _dot(x, wg_ref[0])
            u = _dot(x, wu_ref[0])
            hid = (g * _sigmoid(g)) * u * cw
            return _dot(hid.astype(BF16), wd_ref[0])

        o_ref[...] = (ffn(wga_ref, wua_ref, wda_ref, xs[:, d:d + 1])
                      + ffn(wgb_ref, wub_ref, wdb_ref, xs[:, d + 1:d + 2]))

    @pl.when(valid_ref[ti] == 0)
    def _():
        o_ref[...] = jnp.zeros(o_ref.shape, F32)


def _experts(tile_ea, tile_eb, tile_valid, xs, wg_b, wu_b, wd_b):
    r, w = xs.shape
    d = w - ROUTE_EXTRA
    f = wg_b.shape[2]
    tm = EXPERT_TILE
    wa = lambda blk: pl.BlockSpec(blk, lambda i, ea, eb, va: (ea[i], 0, 0))
    wb = lambda blk: pl.BlockSpec(blk, lambda i, ea, eb, va: (eb[i], 0, 0))
    return pl.pallas_call(
        functools.partial(_expert_kernel, d=d),
        grid_spec=pltpu.PrefetchScalarGridSpec(
            num_scalar_prefetch=3,
            grid=(r // tm,),
            in_specs=[pl.BlockSpec((tm, w), lambda i, ea, eb, va: (i, 0)),
                      wa((1, d, f)), wa((1, d, f)), wa((1, f, d)),
                      wb((1, d, f)), wb((1, d, f)), wb((1, f, d))],
            out_specs=pl.BlockSpec((tm, d), lambda i, ea, eb, va: (i, 0))),
        out_shape=jax.ShapeDtypeStruct((r, d), F32),
        compiler_params=_params("arbitrary"),
        name="expert_ffn",
    )(tile_ea, tile_eb, tile_valid, xs, wg_b, wu_b, wd_b, wg_b, wu_b, wd_b)


def _combine_kernel(pos_ref, y_ref, mod_ref, lng_ref, lnb_ref, os_ref, out_ref, buf, sem, *, alpha, per_row):
    t = y_ref.shape[1]
    base = (pl.program_id(0) * pl.num_programs(1) + pl.program_id(1)) * t

    def issue(r, carry):
        _row_copy(os_ref.at[pl.ds(pos_ref[base + r], 1)], buf.at[pl.ds(r, 1)], sem).start()
        return carry

    lax.fori_loop(0, t, issue, 0, unroll=8)

    def drain(r, carry):
        _row_copy(os_ref.at[pl.ds(0, 1)], buf.at[pl.ds(0, 1)], sem).wait()
        return carry

    lax.fori_loop(0, t, drain, 0, unroll=8)
    gate2 = _mod_get(mod_ref, 5, per_row)
    out_ref[0] = _layer_norm(alpha * y_ref[0] + gate2 * buf[...], lng_ref[...], lnb_ref[...])


def _combine(pos, y1, mod, ln_g, ln_b, o_sorted, *, alpha, per_row):
    nb, s, d = y1.shape
    t = min(SEQ_TILE, s)
    tile = lambda b, i, pos: (b, i, 0)
    const2 = lambda b, i, pos: (0, 0)
    mod_spec = (pl.BlockSpec(mod.shape, lambda b, i, pos: (0, 0, 0)) if per_row
                else pl.BlockSpec((1, 6, d), lambda b, i, pos: (b, 0, 0)))
    return pl.pallas_call(
        functools.partial(_combine_kernel, alpha=alpha, per_row=per_row),
        grid_spec=pltpu.PrefetchScalarGridSpec(
            num_scalar_prefetch=1,
            grid=(nb, s // t),
            in_specs=[pl.BlockSpec((1, t, d), tile), mod_spec,
                      pl.BlockSpec((1, d), const2), pl.BlockSpec((1, d), const2),
                      pl.BlockSpec(memory_space=pl.ANY)],
            out_specs=pl.BlockSpec((1, t, d), tile),
            scratch_shapes=[pltpu.VMEM((t, d), F32), pltpu.SemaphoreType.DMA(())]),
        out_shape=jax.ShapeDtypeStruct((nb, s, d), F32),
        compiler_params=_params("arbitrary", "arbitrary"),
        name="combine_rows",
    )(pos, y1, mod, ln_g, ln_b, o_sorted)


def _sample_inproj_kernel(x_ref, shift_ref, scale_ref, win_ref, hist_ref, wpool_ref, pscale_ref,
                          q_ref, k_ref, v_ref, p_ref, pm_ref, *, aw, n_seq, n_new, past_len):
    u = (x_ref[...] * (1.0 + scale_ref[...]) + shift_ref[...]).astype(BF16)
    acc = _dot(u, win_ref[...])
    q_ref[...] = (acc[:, 0:aw] * Q_SCALE).astype(BF16)
    k_ref[...] = acc[:, aw:2 * aw]
    v_ref[...] = acc[:, 2 * aw:3 * aw]
    p = acc[:, 3 * aw:]
    p_ref[...] = p
    n_hist = hist_ref.shape[0]
    rows = [hist_ref[j] for j in range(n_hist)] + [p[t * n_seq:(t + 1) * n_seq] for t in range(n_new)]
    for g, w in enumerate(POOL_WINDOWS):
        cols = slice(g * POOL_GROUP, (g + 1) * POOL_GROUP)
        mixed = []
        for t in range(n_new):
            end = n_hist + t + 1
            start = max(end - w, 0)
            win_sum = rows[start][:, cols]
            for j in range(start + 1, end):
                win_sum = win_sum + rows[j][:, cols]
            count = float(min(w, past_len + t + 1))
            mixed.append(win_sum / count - rows[n_hist + t][:, cols])
        mixed = jnp.concatenate(mixed, axis=0)
        pm_ref[:, cols] = (_dot(mixed.astype(BF16), wpool_ref[g]) * pscale_ref[:, cols]).astype(BF16)


def _sample_inproj(x, shift, scale, win_b, hist_t, wpool_b, pscale, *, n_seq, n_new, past_len):
    n, d = x.shape
    ew = win_b.shape[1]
    aw = (ew - 4 * POOL_GROUP) // 3
    pw = ew - 3 * aw
    return pl.pallas_call(
        functools.partial(_sample_inproj_kernel, aw=aw, n_seq=n_seq, n_new=n_new, past_len=past_len),
        out_shape=[jax.ShapeDtypeStruct((n, aw), BF16), jax.ShapeDtypeStruct((n, aw), F32),
                   jax.ShapeDtypeStruct((n, aw), F32), jax.ShapeDtypeStruct((n, pw), F32),
                   jax.ShapeDtypeStruct((n, pw), BF16)],
        compiler_params=pltpu.CompilerParams(vmem_limit_bytes=VMEM_LIMIT_BYTES),
        name="sample_inproj",
    )(x, shift, scale, win_b, hist_t, wpool_b, pscale)


def _paged_attn_kernel(pt_ref, lam_ref, q_ref, kn_ref, vn_ref, bias_ref, bn_ref, g_ref, *rest,
                       pps, n_new, lam_init):
    del pt_ref
    ck = rest[0:pps]
    cv = rest[pps:2 * pps]
    o_ref = rest[2 * pps]
    m_sc, l_sc, acc_sc = rest[2 * pps + 1:]
    step = pl.program_id(1)
    last = pl.num_programs(1) - 1
    heads = [slice(h * V_DIM, (h + 1) * V_DIM) for h in range(N_HEADS)]
    page_rows = ck[0].shape[0]

    @pl.when(step == 0)
    def _():
        m_sc[...] = jnp.full(m_sc.shape, -jnp.inf, F32)
        l_sc[...] = jnp.zeros(l_sc.shape, F32)
        acc_sc[...] = jnp.zeros(acc_sc.shape, F32)

    qn = q_ref[0]
    q_rows = jnp.concatenate([_split_branches(qn[:, cols]) for cols in heads], axis=0)
    bias = bias_ref[jnp.where(step == last, 1, 0)]
    s = jnp.concatenate([_dot_nt(q_rows, ck[pg][...].astype(BF16)) for pg in range(pps)], axis=1) + bias

    def past_values(p):
        acc = _dot(p[:, 0:page_rows], cv[0][...].astype(BF16))
        for pg in range(1, pps):
            acc = acc + _dot(p[:, pg * page_rows:(pg + 1) * page_rows], cv[pg][...].astype(BF16))
        return acc

    _online_update(s, past_values, m_sc, l_sc, acc_sc)

    @pl.when(step == last)
    def _():
        sn = _dot_nt(q_rows, kn_ref[0].astype(BF16)) + bn_ref[...]
        _online_update(sn, lambda p: _dot(p, vn_ref[0].astype(BF16)), m_sc, l_sc, acc_sc)
        lam = lam_ref[0]
        for h, cols in enumerate(heads):
            rows = slice(h * 2 * n_new, (h + 1) * 2 * n_new)
            o = _diff_head_out(acc_sc[rows, :], l_sc[rows, :], n_new, lam, g_ref[...], lam_init)
            o_ref[0, :, cols] = o.astype(BF16)


def _page_index(n, s, pt_ref, *, first_page, pg, pps, n_pages):
    return (first_page + pt_ref[n * n_pages + s * pps + pg], 0)


def _paged_attention(page_table, lam, q, k_new, v_new, bias_past, bias_new, subln_g, cache_k, cache_v, *,
                     layer, lam_init):
    n_seq, n_pages = page_table.shape
    n_new, aw = q.shape[1], q.shape[2]
    n_phys, page = cache_k.shape[1], cache_k.shape[2]
    pps = math.gcd(PAGES_PER_STEP, n_pages)
    n_steps = n_pages // pps
    n_rows = N_HEADS * 2 * n_new
    ck = cache_k.reshape(-1, V_DIM)
    cv = cache_v.reshape(-1, V_DIM)
    page_specs = [pl.BlockSpec((page * N_HEADS, V_DIM),
                               functools.partial(_page_index, first_page=layer * n_phys, pg=pg, pps=pps,
                                                 n_pages=n_pages))
                  for pg in range(pps)]
    per_seq = lambda n, s, pt: (n, 0, 0)
    return pl.pallas_call(
        functools.partial(_paged_attn_kernel, pps=pps, n_new=n_new, lam_init=lam_init),
        grid_spec=pltpu.PrefetchScalarGridSpec(
            num_scalar_prefetch=1,
            grid=(n_seq, n_steps),
            in_specs=[pl.BlockSpec(memory_space=pltpu.SMEM),
                      pl.BlockSpec((1, n_new, aw), per_seq),
                      pl.BlockSpec((1,) + k_new.shape[1:], per_seq),
                      pl.BlockSpec((1,) + v_new.shape[1:], per_seq),
                      pl.BlockSpec(bias_past.shape, lambda n, s, pt: (0, 0, 0)),
                      pl.BlockSpec(bias_new.shape, lambda n, s, pt: (0, 0)),
                      pl.BlockSpec((1, V_DIM), lambda n, s, pt: (0, 0))] + page_specs + page_specs,
            out_specs=pl.BlockSpec((1, n_new, aw), per_seq),
            scratch_shapes=[pltpu.VMEM((n_rows, LANES), F32), pltpu.VMEM((n_rows, LANES), F32),
                            pltpu.VMEM((n_rows, V_DIM), F32)]),
        out_shape=jax.ShapeDtypeStruct((n_seq, n_new, aw), BF16),
        compiler_params=_params("arbitrary", "arbitrary"),
        name="paged_attention",
    )(page_table.reshape(-1), lam, q, k_new, v_new, bias_past, bias_new, subln_g, *([ck] * pps), *([cv] * pps))


def _rel_bias_lookup(rel_bias, dist):
    n = jnp.maximum(dist, 0)
    max_exact = N_BUCKETS // 2
    nf = jnp.maximum(n, 1).astype(F32)
    large = max_exact + (jnp.log(nf / max_exact) / math.log(MAX_DISTANCE / max_exact)
                         * (N_BUCKETS - max_exact)).astype(jnp.int32)
    large = jnp.minimum(large, N_BUCKETS - 1)
    bucket = jnp.where(n < max_exact, n, large)
    onehot = (bucket[..., None] == jnp.arange(N_BUCKETS)).astype(F32)
    return jnp.einsum('...b,bh->...h', onehot, rel_bias.astype(F32), precision=lax.Precision.HIGHEST) * LOG2E


def _prompt_bias_tables(rel_bias, tq):
    i = jnp.arange(tq)[:, None]
    j = jnp.arange(tq)[None, :]
    diag = jnp.where((j <= i)[None], jnp.transpose(_rel_bias_lookup(rel_bias, i - j), (2, 0, 1)), -jnp.inf)
    sub = jnp.transpose(_rel_bias_lookup(rel_bias, tq + i - j), (2, 0, 1))
    both = lambda b: jnp.concatenate([b, b], axis=1)
    return both(diag), both(sub), rel_bias[N_BUCKETS - 1].astype(F32) * LOG2E


def _head_interleave(table):
    n_h, n_t, n_k = table.shape
    same = jnp.arange(n_h)[:, None, None, None] == jnp.arange(n_h)[None, None, None, :]
    wide = jnp.where(same, table[:, :, :, None], -jnp.inf)
    wide = jnp.concatenate([wide, wide], axis=1)
    return wide.reshape(n_h * 2 * n_t, n_k * n_h)


def _sample_bias_tables(rel_bias, past_len, n_new, step_keys):
    assert step_keys >= MAX_DISTANCE
    t = jnp.arange(n_new)
    key = jnp.arange(step_keys)
    far = jnp.transpose(_rel_bias_lookup(rel_bias, past_len + t[:, None] - key[None, :]), (2, 0, 1))
    near = jnp.transpose(_rel_bias_lookup(rel_bias, step_keys + t[:, None] - key[None, :]), (2, 0, 1))
    tn = jnp.arange(LANES // N_HEADS)
    new = jnp.transpose(_rel_bias_lookup(rel_bias, t[:, None] - tn[None, :]), (2, 0, 1))
    new = jnp.where((tn[None, :] <= t[:, None])[None], new, -jnp.inf)
    return jnp.stack([_head_interleave(far), _head_interleave(near)]), _head_interleave(new)


def _routing_plan(route_p, cnt_p, route_s, cnt_s, n_rows):
    tm = EXPERT_TILE
    bucket_p = route_p[..., 0].reshape(-1).astype(jnp.int32)
    rank_p = route_p[..., 1].reshape(-1).astype(jnp.int32)
    bucket_s = route_s[..., 0].reshape(-1).astype(jnp.int32)
    rank_s = route_s[..., 1].reshape(-1).astype(jnp.int32)
    cp = cnt_p[0, :N_PAIR_BUCKETS].astype(jnp.int32)
    cs = cnt_s[0, :N_PAIR_BUCKETS].astype(jnp.int32)
    tiles = (cp + cs + tm - 1) // tm
    tile_end = jnp.cumsum(tiles)
    off = (tile_end - tiles) * tm
    lookup = lambda table, idx: jnp.sum(
        jnp.where(idx[:, None] == jnp.arange(table.shape[0])[None, :], table[None, :], 0), axis=1)
    pos_p = lookup(off, bucket_p) + rank_p
    pos_s = lookup(off + cp, bucket_s) + rank_s
    n_tiles = n_rows // tm
    n_used = tile_end[-1]
    ti = jnp.minimum(jnp.arange(n_tiles), n_used - 1)
    tile_bucket = jnp.sum((ti[:, None] >= tile_end[None, :]).astype(jnp.int32), axis=1)
    pair = np.array(EXPERT_PAIRS, np.int32)
    base = np.arange(N_PAIR_BUCKETS) // len(EXPERT_PAIRS) * EXP_PER_GROUP
    tile_ea = lookup(jnp.asarray(base + pair[np.arange(N_PAIR_BUCKETS) % len(EXPERT_PAIRS), 0]), tile_bucket)
    tile_eb = lookup(jnp.asarray(base + pair[np.arange(N_PAIR_BUCKETS) % len(EXPERT_PAIRS), 1]), tile_bucket)
    tile_valid = (jnp.arange(n_tiles) < n_used).astype(jnp.int32)
    return pos_p, pos_s, tile_ea.astype(jnp.int32), tile_eb.astype(jnp.int32), tile_valid


def kernel(x_prompt, x_sample, c_prompt, c_sample, cache_k, cache_v, state_pool, page_table, rel_bias, w_ada, b_ada, w_in, lambda_q1, lambda_k1, lambda_q2, lambda_k2, subln_g, w_pool, pool_scale, w_o, ln1_g, ln1_b, w_router_group, b_router_group, w_router_expert, b_router_expert, w_gate, w_up, w_down, ln2_g, ln2_b):
    depth = w_in.shape[0]
    nb, seq, d = x_prompt.shape
    n_seq, n_new, _ = x_sample.shape
    n_pages = page_table.shape[1]
    page = cache_k.shape[2]
    past_len = n_pages * page
    aw = N_HEADS * V_DIM
    pw = pool_scale.shape[1]
    alpha = (2 * depth) ** 0.25
    tq = min(ATTN_TILE, seq)
    assert tq >= MAX_DISTANCE and seq % tq == 0 and seq % min(SEQ_TILE, seq) == 0
    assert past_len >= POOL_HIST and n_new <= 8
    n_tok_p = nb * seq
    n_tok_s = n_seq * n_new
    assert n_tok_p % min(SEQ_TILE, n_tok_p) == 0
    n_rows = ((n_tok_p + n_tok_s) // EXPERT_TILE + N_PAIR_BUCKETS) * EXPERT_TILE
    pps = math.gcd(PAGES_PER_STEP, n_pages)
    n_steps = n_pages // pps

    bias_diag, bias_sub, c_far = _prompt_bias_tables(rel_bias, tq)
    bias_past, bias_new = _sample_bias_tables(rel_bias, past_len, n_new, pps * page)
    c_all = jnp.concatenate([c_prompt, c_sample], axis=0)
    xs_tm = jnp.transpose(x_sample, (1, 0, 2)).reshape(n_tok_s, d)

    xp, xs_cur = x_prompt, xs_tm
    kp_l, vp_l, pp_l, ks_l, vs_l, ps_l = [], [], [], [], [], []
    for l in range(depth):
        lam_init = 0.8 - 0.6 * math.exp(-0.3 * l)
        lam = (jnp.exp(jnp.sum(lambda_q1[l].astype(F32) * lambda_k1[l].astype(F32)))
               - jnp.exp(jnp.sum(lambda_q2[l].astype(F32) * lambda_k2[l].astype(F32))) + lam_init).reshape(1)
        win_b = w_in[l].astype(BF16)
        wpool_b = w_pool[l].astype(BF16)
        wo_b = w_o[l].astype(BF16)
        wr = jnp.concatenate([w_router_group[l], jnp.transpose(w_router_expert[l], (1, 0, 2)).reshape(d, N_EXPERTS)], axis=1)
        wr_b = jnp.pad(wr, ((0, 0), (0, LANES - wr.shape[1]))).astype(BF16)
        br = jnp.pad(jnp.concatenate([b_router_group[l], b_router_expert[l].reshape(-1)]).astype(F32),
                     (0, LANES - N_EXP_GROUPS - N_EXPERTS)).reshape(1, LANES)
        wg_b = w_gate[l].reshape(N_EXPERTS, d, -1).astype(BF16)
        wu_b = w_up[l].reshape(N_EXPERTS, d, -1).astype(BF16)
        wd_b = w_down[l].reshape(N_EXPERTS, -1, d).astype(BF16)
        pscale = pool_scale[l].reshape(1, pw)
        g_sub = subln_g[l].reshape(1, V_DIM)
        ln1g, ln1b = ln1_g[l].reshape(1, d), ln1_b[l].reshape(1, d)
        ln2g, ln2b = ln2_g[l].reshape(1, d), ln2_b[l].reshape(1, d)

        m_all = _modulation(c_all, w_ada[l], b_ada[l]).reshape(nb + n_seq, 6, d)
        mod_p = m_all[:nb]
        mod_s = jnp.tile(jnp.transpose(m_all[nb:], (1, 0, 2)), (1, n_new, 1))

        q_p, k_p, v_p, kb_p, vb_p, pm_p, ph_p = _prompt_inproj(xp, mod_p, win_b, wpool_b, pscale)
        o_p = _prompt_attention(q_p, kb_p, vb_p, bias_diag, bias_sub, c_far, lam, g_sub, lam_init)
        y1_p, tokx_p, route_p, cnt_p = _outproj_router(o_p, pm_p, xp, mod_p, wo_b, ln1g, ln1b, wr_b, br,
                                                       alpha=alpha, per_row=False)

        hist = state_pool[l]
        q_s, k_s, v_s, p_s, pm_s = _sample_inproj(xs_cur, mod_s[0], mod_s[1], win_b, jnp.transpose(hist, (1, 0, 2)),
                                                  wpool_b, pscale, n_seq=n_seq, n_new=n_new, past_len=past_len)
        seq_major = lambda a: jnp.transpose(a.reshape(n_new, n_seq, -1), (1, 0, 2))
        new_rows = lambda a: jnp.pad(seq_major(a).reshape(n_seq, n_new * N_HEADS, V_DIM),
                                     ((0, 0), (0, LANES - n_new * N_HEADS), (0, 0)))
        o_s = _paged_attention(page_table, lam, seq_major(q_s), new_rows(k_s), new_rows(v_s), bias_past, bias_new, g_sub,
                               cache_k, cache_v, layer=l, lam_init=lam_init)
        o_s_tm = jnp.transpose(o_s, (1, 0, 2)).reshape(1, n_tok_s, aw)
        y1_s, tokx_s, route_s, cnt_s = _outproj_router(o_s_tm, pm_s[None], xs_cur[None], mod_s, wo_b, ln1g, ln1b,
                                                       wr_b, br, alpha=alpha, per_row=True)

        pos_p, pos_s, tile_ea, tile_eb, tile_valid = _routing_plan(route_p, cnt_p, route_s, cnt_s, n_rows)
        xs_sorted = jnp.zeros((n_rows, d + ROUTE_EXTRA), F32)
        xs_sorted = _dispatch(pos_p, tokx_p.reshape(n_tok_p, -1), xs_sorted)
        xs_sorted = _dispatch(pos_s, tokx_s.reshape(n_tok_s, -1), xs_sorted)
        o_sorted = _experts(tile_ea, tile_eb, tile_valid, xs_sorted, wg_b, wu_b, wd_b)
        xp = _combine(pos_p, y1_p, mod_p, ln2g, ln2b, o_sorted, alpha=alpha, per_row=False)
        xs_cur = _combine(pos_s, y1_s, mod_s, ln2g, ln2b, o_sorted, alpha=alpha, per_row=True)[0]

        kp_l.append(k_p)
        vp_l.append(v_p)
        pp_l.append(ph_p)
        ks_l.append(seq_major(k_s).reshape(n_seq, n_new, N_HEADS, V_DIM))
        vs_l.append(seq_major(v_s).reshape(n_seq, n_new, N_HEADS, V_DIM))
        ps_l.append(jnp.concatenate([hist, seq_major(p_s)], axis=1)[:, -POOL_HIST:])

    y_sample = jnp.transpose(xs_cur.reshape(n_new, n_seq, d), (1, 0, 2))
    return (xp, y_sample, jnp.stack(kp_l), jnp.stack(vp_l), jnp.stack(pp_l),
            jnp.stack(ks_l), jnp.stack(vs_l), jnp.stack(ps_l))
```

```python
import functools
import math

import numpy as np
import jax
import jax.numpy as jnp
from jax import lax
from jax.experimental import pallas as pl
from jax.experimental.pallas import tpu as pltpu

F32 = jnp.float32
BF16 = jnp.bfloat16

N_HEADS = 4
HEAD_DIM = 64
V_DIM = 2 * HEAD_DIM
POOL_WINDOWS = (2, 4, 8, 16)
POOL_GROUP = 128
POOL_HIST = max(POOL_WINDOWS) - 1
N_BUCKETS = 32
MAX_DISTANCE = 128
N_EXP_GROUPS = 4
EXP_PER_GROUP = 4
N_EXPERTS = N_EXP_GROUPS * EXP_PER_GROUP
EXPERT_PAIRS = ((0, 1), (0, 2), (0, 3), (1, 2), (1, 3), (2, 3))
N_PAIR_BUCKETS = N_EXP_GROUPS * len(EXPERT_PAIRS)
LN_EPS = 1e-5
RMS_EPS = 1e-5

LANES = 128
POOL_HALO = 16
VMEM_LIMIT_BYTES = 48 * 1024 * 1024
SEQ_TILE = 512
ATTN_TILE = 256
EXPERT_TILE = 256
PAGES_PER_STEP = 16
ROUTE_EXTRA = LANES
LOG2E = math.log2(math.e)
Q_SCALE = HEAD_DIM ** -0.5 * LOG2E


def _params(*sem):
    return pltpu.CompilerParams(dimension_semantics=sem, vmem_limit_bytes=VMEM_LIMIT_BYTES)


def _dot(a, b):
    return jnp.dot(a, b, preferred_element_type=F32)


def _dot_nt(a, b):
    return lax.dot_general(a, b, (((1,), (1,)), ((), ())), preferred_element_type=F32)


def _sigmoid(x):
    return 1.0 / (1.0 + jnp.exp(-x))


def _mod_kernel(c_ref, w_ref, b_ref, o_ref):
    c = c_ref[...]
    s = c * _sigmoid(c)
    s_hi = s.astype(BF16)
    s_lo = (s - s_hi.astype(F32)).astype(BF16)
    w = w_ref[...]
    w_hi = w.astype(BF16)
    w_lo = (w - w_hi.astype(F32)).astype(BF16)
    o_ref[...] = _dot(s_hi, w_hi) + _dot(s_lo, w_hi) + _dot(s_hi, w_lo) + b_ref[...]


def _modulation(c_all, w_ada, b_ada):
    n, d = c_all.shape
    e = w_ada.shape[1]
    bn = 1024
    return pl.pallas_call(
        _mod_kernel,
        grid=(e // bn,),
        in_specs=[pl.BlockSpec((n, d), lambda j: (0, 0)),
                  pl.BlockSpec((d, bn), lambda j: (0, j)),
                  pl.BlockSpec((1, bn), lambda j: (0, j))],
        out_specs=pl.BlockSpec((n, bn), lambda j: (0, j)),
        out_shape=jax.ShapeDtypeStruct((n, e), F32),
        compiler_params=_params("arbitrary"),
        name="modulation",
    )(c_all, w_ada, b_ada.reshape(1, e))


def _pool_mixed(ext, g, w, inv_cnt, rows):
    eg = ext[:, g * POOL_GROUP:(g + 1) * POOL_GROUP]
    s = eg
    step = 1
    while step < w:
        s = s + pltpu.roll(s, step, 0)
        step *= 2
    return s[POOL_HALO:POOL_HALO + rows] * inv_cnt - eg[POOL_HALO:POOL_HALO + rows]


def _inproj_kernel(x_ref, mod_ref, win_ref, wpool_ref, pscale_ref,
                   q_ref, k_ref, v_ref, kb_ref, vb_ref, pm_ref, ph_ref, ext_ref, *, ts, aw):
    i = pl.program_id(1)

    @pl.when(i == 0)
    def _():
        ext_ref[0:POOL_HALO, :] = jnp.zeros((POOL_HALO, ext_ref.shape[1]), F32)

    @pl.when(i > 0)
    def _():
        ext_ref[0:POOL_HALO, :] = ext_ref[ts:ts + POOL_HALO, :]

    shift = mod_ref[0, 0:1, :]
    scale = mod_ref[0, 1:2, :]
    u = (x_ref[0] * (1.0 + scale) + shift).astype(BF16)
    acc = _dot(u, win_ref[...])
    q_ref[0] = (acc[:, 0:aw] * Q_SCALE).astype(BF16)
    k = acc[:, aw:2 * aw]
    v = acc[:, 2 * aw:3 * aw]
    for h in range(N_HEADS):
        k_ref[0, pl.ds(h, ts, stride=N_HEADS), :] = k[:, h * V_DIM:(h + 1) * V_DIM]
        v_ref[0, pl.ds(h, ts, stride=N_HEADS), :] = v[:, h * V_DIM:(h + 1) * V_DIM]
    kb_ref[0] = k.astype(BF16)
    vb_ref[0] = v.astype(BF16)
    p = acc[:, 3 * aw:]
    ext_ref[POOL_HALO:POOL_HALO + ts, :] = p
    ext = ext_ref[...]
    pos = i * ts + lax.broadcasted_iota(jnp.int32, (ts, 1), 0)
    for g, w in enumerate(POOL_WINDOWS):
        inv_cnt = 1.0 / jnp.minimum(w, pos + 1).astype(F32)
        mixed = _pool_mixed(ext, g, w, inv_cnt, ts)
        cols = slice(g * POOL_GROUP, (g + 1) * POOL_GROUP)
        pm_ref[0, :, cols] = (_dot(mixed.astype(BF16), wpool_ref[g]) * pscale_ref[:, cols]).astype(BF16)

    @pl.when(i == pl.num_programs(1) - 1)
    def _():
        ph_ref[0] = p[ts - POOL_HIST:ts, :]


def _prompt_inproj(x, mod, win_b, wpool_b, pscale):
    nb, s, d = x.shape
    ew = win_b.shape[1]
    aw = (ew - 4 * POOL_GROUP) // 3
    pw = ew - 3 * aw
    ts = min(SEQ_TILE, s)
    nt = s // ts
    tile = lambda b, i: (b, i, 0)
    const2 = lambda b, i: (0, 0)
    return pl.pallas_call(
        functools.partial(_inproj_kernel, ts=ts, aw=aw),
        grid=(nb, nt),
        in_specs=[pl.BlockSpec((1, ts, d), tile),
                  pl.BlockSpec((1, 6, d), lambda b, i: (b, 0, 0)),
                  pl.BlockSpec((d, ew), const2),
                  pl.BlockSpec(wpool_b.shape, lambda b, i: (0, 0, 0)),
                  pl.BlockSpec((1, pw), const2)],
        out_specs=[pl.BlockSpec((1, ts, aw), tile),
                   pl.BlockSpec((1, ts * N_HEADS, V_DIM), tile),
                   pl.BlockSpec((1, ts * N_HEADS, V_DIM), tile),
                   pl.BlockSpec((1, ts, aw), tile), pl.BlockSpec((1, ts, aw), tile),
                   pl.BlockSpec((1, ts, pw), tile),
                   pl.BlockSpec((1, POOL_HIST, pw), lambda b, i: (b, 0, 0))],
        out_shape=[jax.ShapeDtypeStruct((nb, s, aw), BF16),
                   jax.ShapeDtypeStruct((nb, s * N_HEADS, V_DIM), F32),
                   jax.ShapeDtypeStruct((nb, s * N_HEADS, V_DIM), F32),
                   jax.ShapeDtypeStruct((nb, s, aw), BF16),
                   jax.ShapeDtypeStruct((nb, s, aw), BF16),
                   jax.ShapeDtypeStruct((nb, s, pw), BF16),
                   jax.ShapeDtypeStruct((nb, POOL_HIST, pw), F32)],
        scratch_shapes=[pltpu.VMEM((ts + POOL_HALO, pw), F32)],
        compiler_params=_params("arbitrary", "arbitrary"),
        name="prompt_inproj",
    )(x, mod, win_b, wpool_b, pscale)


def _split_branches(qh):
    lane = lax.broadcasted_iota(jnp.int32, qh.shape, 1)
    zero = jnp.zeros_like(qh)
    return jnp.concatenate([jnp.where(lane < HEAD_DIM, qh, zero), jnp.where(lane >= HEAD_DIM, qh, zero)], axis=0)


def _lane_tile(a, width):
    return jnp.concatenate([a] * (width // LANES), axis=1) if width > LANES else a


def _online_update(s, values, m_ref, l_ref, acc_ref, first=False):
    rows, width = s.shape
    m_new = jnp.broadcast_to(jnp.max(s, axis=1, keepdims=True), (rows, LANES))
    if not first:
        m_old = m_ref[...]
        m_new = jnp.maximum(m_old, m_new)
    p = jnp.exp2(s - _lane_tile(m_new, width))
    pv = values(p.astype(BF16))
    l_new = jnp.broadcast_to(jnp.sum(p, axis=1, keepdims=True), (rows, LANES))
    if first:
        acc_ref[...] = pv
        l_ref[...] = l_new
    else:
        alpha = jnp.exp2(m_old - m_new)
        acc_ref[...] = _lane_tile(alpha, pv.shape[1]) * acc_ref[...] + pv
        l_ref[...] = alpha * l_ref[...] + l_new
    m_ref[...] = m_new


def _diff_head_out(acc, l, t, lam, g, lam_init):
    o = acc[0:t] / l[0:t] - lam * (acc[t:2 * t] / l[t:2 * t])
    o = o * lax.rsqrt(jnp.mean(o * o, axis=-1, keepdims=True) + RMS_EPS) * g
    return o * (1.0 - lam_init)


def _attn_kernel(cfar_ref, lam_ref, q_ref, k_ref, v_ref, bd_ref, be_ref, g_ref, o_ref,
                 qs_sc, m_sc, l_sc, acc_sc, *, tq, lam_init):
    qi = pl.program_id(1)
    heads = [slice(h * V_DIM, (h + 1) * V_DIM) for h in range(N_HEADS)]
    for h, cols in enumerate(heads):
        qs_sc[h] = _split_branches(q_ref[0, :, cols])

    def step(j, bias, first):
        rows = pl.ds(pl.multiple_of(j * tq, tq), tq)
        for h, cols in enumerate(heads):
            s = _dot_nt(qs_sc[h], k_ref[0, rows, cols]) + bias(h)
            _online_update(s, lambda p, cols=cols: _dot(p, v_ref[0, rows, cols]),
                           m_sc.at[h], l_sc.at[h], acc_sc.at[h], first=first)

    step(qi, lambda h: bd_ref[h], True)

    @pl.when(qi >= 1)
    def _():
        step(qi - 1, lambda h: be_ref[h], False)

    def far(j, carry):
        step(j, lambda h: cfar_ref[h], False)
        return carry

    lax.fori_loop(0, jnp.maximum(qi - 1, 0), far, 0)
    lam = lam_ref[0]
    for h, cols in enumerate(heads):
        o = _diff_head_out(acc_sc[h], l_sc[h], tq, lam, g_ref[...], lam_init)
        o_ref[0, :, cols] = o.astype(BF16)


def _prompt_attention(q, kb, vb, bias_diag, bias_sub, c_far, lam, subln_g, lam_init):
    nb, s, aw = q.shape
    tq = bias_diag.shape[2]
    smem = pl.BlockSpec(memory_space=pltpu.SMEM)
    const3 = lambda b, i: (0, 0, 0)
    return pl.pallas_call(
        functools.partial(_attn_kernel, tq=tq, lam_init=lam_init),
        grid=(nb, s // tq),
        in_specs=[smem, smem,
                  pl.BlockSpec((1, tq, aw), lambda b, i: (b, i, 0)),
                  pl.BlockSpec((1, s, aw), lambda b, i: (b, 0, 0)),
                  pl.BlockSpec((1, s, aw), lambda b, i: (b, 0, 0)),
                  pl.BlockSpec(bias_diag.shape, const3),
                  pl.BlockSpec(bias_sub.shape, const3),
                  pl.BlockSpec((1, V_DIM), lambda b, i: (0, 0))],
        out_specs=pl.BlockSpec((1, tq, aw), lambda b, i: (b, i, 0)),
        out_shape=jax.ShapeDtypeStruct((nb, s, aw), BF16),
        scratch_shapes=[pltpu.VMEM((N_HEADS, 2 * tq, V_DIM), BF16),
                        pltpu.VMEM((N_HEADS, 2 * tq, LANES), F32), pltpu.VMEM((N_HEADS, 2 * tq, LANES), F32),
                        pltpu.VMEM((N_HEADS, 2 * tq, V_DIM), F32)],
        compiler_params=_params("arbitrary", "arbitrary"),
        name="prompt_attention",
    )(c_far, lam, q, kb, vb, bias_diag, bias_sub, subln_g)


def _layer_norm(z, g, b):
    mu = jnp.mean(z, axis=-1, keepdims=True)
    zc = z - mu
    var = jnp.mean(zc * zc, axis=-1, keepdims=True)
    return zc * lax.rsqrt(var + LN_EPS) * g + b


def _mod_get(mod_ref, j, per_row):
    return mod_ref[j] if per_row else mod_ref[0, j:j + 1, :]


def _outproj_router_kernel(o_ref, pm_ref, x_ref, mod_ref, wo_ref, lng_ref, lnb_ref, wr_ref, br_ref,
                           y_ref, tokx_ref, route_ref, cnt_ref, carry_sc, *, alpha, per_row):
    t = x_ref.shape[1]
    d = x_ref.shape[2]
    aw = o_ref.shape[2]

    @pl.when((pl.program_id(0) == 0) & (pl.program_id(1) == 0))
    def _():
        carry_sc[...] = jnp.zeros(carry_sc.shape, F32)

    h = _dot(o_ref[0], wo_ref[0:aw, :]) + _dot(pm_ref[0], wo_ref[aw:, :])
    gate1 = _mod_get(mod_ref, 2, per_row)
    y1 = _layer_norm(alpha * x_ref[0] + gate1 * h, lng_ref[...], lnb_ref[...])
    y_ref[0] = y1
    tok = y1 * (1.0 + _mod_get(mod_ref, 4, per_row)) + _mod_get(mod_ref, 3, per_row)
    tokx_ref[0, :, 0:d] = tok

    logits = _dot(tok.astype(BF16), wr_ref[...]) + br_ref[...]
    lane = lax.broadcasted_iota(jnp.int32, logits.shape, 1)
    lane_f = lane.astype(F32)
    neg = jnp.full_like(logits, -jnp.inf)

    def first_argmax(vals, vmax):
        return jnp.min(jnp.where(vals == vmax, lane_f, float(LANES)), axis=1, keepdims=True).astype(jnp.int32)

    gl = jnp.where(lane < N_EXP_GROUPS, logits, neg)
    gmax = jnp.max(gl, axis=1, keepdims=True)
    gidx = first_argmax(gl, gmax)
    g_w = 1.0 / jnp.sum(jnp.exp(gl - gmax), axis=1, keepdims=True)
    lo_lane = N_EXP_GROUPS + EXP_PER_GROUP * gidx
    el = jnp.where((lane >= lo_lane) & (lane < lo_lane + EXP_PER_GROUP), logits, neg)
    v1 = jnp.max(el, axis=1, keepdims=True)
    i1 = first_argmax(el, v1)
    el2 = jnp.where(lane == i1, neg, el)
    v2 = jnp.max(el2, axis=1, keepdims=True)
    i2 = first_argmax(el2, v2)
    e21 = jnp.exp(v2 - v1)
    w1 = g_w / (1.0 + e21)
    w2 = g_w * e21 / (1.0 + e21)
    first_lo = i1 < i2
    cw_lo = jnp.where(first_lo, w1, w2)
    cw_hi = jnp.where(first_lo, w2, w1)
    a = jnp.minimum(i1, i2) - lo_lane
    b = jnp.maximum(i1, i2) - lo_lane
    pair_base = jnp.where(a == 0, 0, jnp.where(a == 1, 3, 5))
    bucket = gidx * len(EXPERT_PAIRS) + pair_base + (b - a - 1)

    xlane = lax.broadcasted_iota(jnp.int32, (t, ROUTE_EXTRA), 1)
    tokx_ref[0, :, d:] = jnp.where(xlane == 0, cw_lo, jnp.where(xlane == 1, cw_hi, 0.0))

    onehot = lane == bucket
    row = lax.broadcasted_iota(jnp.int32, (t, t), 0)
    col = lax.broadcasted_iota(jnp.int32, (t, t), 1)
    ltri = jnp.where(col < row, 1.0, 0.0).astype(BF16)
    prefix = _dot(ltri, jnp.where(onehot, 1.0, 0.0).astype(BF16)) + carry_sc[...]
    rank = jnp.sum(jnp.where(onehot, prefix, 0.0), axis=1, keepdims=True)
    carry_sc[...] = carry_sc[...] + jnp.sum(jnp.where(onehot, 1.0, 0.0), axis=0, keepdims=True)
    cnt_ref[...] = carry_sc[...]
    route_ref[0] = jnp.where(lane == 0, bucket.astype(F32), jnp.where(lane == 1, rank, 0.0))


def _outproj_router(o, pm, x, mod, wo_b, ln_g, ln_b, wr_b, br, *, alpha, per_row):
    nb, s, d = x.shape
    aw = o.shape[2]
    pw = pm.shape[2]
    t = min(SEQ_TILE, s)
    tile = lambda b, i: (b, i, 0)
    const2 = lambda b, i: (0, 0)
    mod_spec = (pl.BlockSpec(mod.shape, lambda b, i: (0, 0, 0)) if per_row
                else pl.BlockSpec((1, 6, d), lambda b, i: (b, 0, 0)))
    return pl.pallas_call(
        functools.partial(_outproj_router_kernel, alpha=alpha, per_row=per_row),
        grid=(nb, s // t),
        in_specs=[pl.BlockSpec((1, t, aw), tile), pl.BlockSpec((1, t, pw), tile), pl.BlockSpec((1, t, d), tile),
                  mod_spec,
                  pl.BlockSpec(wo_b.shape, const2), pl.BlockSpec((1, d), const2), pl.BlockSpec((1, d), const2),
                  pl.BlockSpec(wr_b.shape, const2), pl.BlockSpec((1, LANES), const2)],
        out_specs=[pl.BlockSpec((1, t, d), tile), pl.BlockSpec((1, t, d + ROUTE_EXTRA), tile),
                   pl.BlockSpec((1, t, LANES), tile), pl.BlockSpec((1, LANES), const2)],
        out_shape=[jax.ShapeDtypeStruct((nb, s, d), F32), jax.ShapeDtypeStruct((nb, s, d + ROUTE_EXTRA), F32),
                   jax.ShapeDtypeStruct((nb, s, LANES), F32), jax.ShapeDtypeStruct((1, LANES), F32)],
        scratch_shapes=[pltpu.VMEM((1, LANES), F32)],
        compiler_params=_params("arbitrary", "arbitrary"),
        name="outproj_router",
    )(o, pm, x, mod, wo_b, ln_g, ln_b, wr_b, br)


def _issue_row_copies(src_ref, dst_ref, sem, src_row, dst_row, n):
    for r in range(n):
        pltpu.make_async_copy(src_ref.at[pl.ds(src_row(r), 1)], dst_ref.at[pl.ds(dst_row(r), 1)],
                              sem).start(priority=r % 2)


def _wait_row_copies(src_ref, dst_ref, sem, n):
    pltpu.make_async_copy(src_ref.at[pl.ds(0, n)], dst_ref.at[pl.ds(0, n)], sem).wait()


def _dispatch_kernel(pos_ref, tok_ref, xs_in_ref, xs_ref, sem, *, t):
    del xs_in_ref
    i = pl.program_id(0)
    base = i * t
    _issue_row_copies(tok_ref, xs_ref, sem.at[i % 2], lambda r: base + r, lambda r: pos_ref[base + r], t)

    @pl.when(i > 0)
    def _():
        _wait_row_copies(tok_ref, xs_ref, sem.at[(i - 1) % 2], t)

    @pl.when(i == pl.num_programs(0) - 1)
    def _():
        _wait_row_copies(tok_ref, xs_ref, sem.at[i % 2], t)


def _dispatch(pos, tokx, xs):
    n, w = tokx.shape
    t = min(SEQ_TILE, n)
    return pl.pallas_call(
        functools.partial(_dispatch_kernel, t=t),
        grid_spec=pltpu.PrefetchScalarGridSpec(
            num_scalar_prefetch=1,
            grid=(n // t,),
            in_specs=[pl.BlockSpec(memory_space=pl.ANY), pl.BlockSpec(memory_space=pl.ANY)],
            out_specs=pl.BlockSpec(memory_space=pl.ANY),
            scratch_shapes=[pltpu.SemaphoreType.DMA((2,))]),
        out_shape=jax.ShapeDtypeStruct(xs.shape, xs.dtype),
        input_output_aliases={2: 0},
        compiler_params=_params("arbitrary"),
        name="dispatch_rows",
    )(pos, tokx, xs)


def _expert_kernel(ea_ref, eb_ref, valid_ref, xs_ref, wga_ref, wua_ref, wda_ref, wgb_ref, wub_ref, wdb_ref,
                   o_ref, *, d):
    del ea_ref, eb_ref
    ti = pl.program_id(0)

    @pl.when(valid_ref[ti] == 1)
    def _():
        xs = xs_ref[...]
        x = xs[:, 0:d].astype(BF16)

        def ffn(wg_ref, wu_ref, wd_ref, cw):
            g = _dot(x, wg_ref[0])
            u = _dot(x, wu_ref[0])
            hid = (g * _sigmoid(g)) * u * cw
            return _dot(hid.astype(BF16), wd_ref[0])

        o_ref[...] = (ffn(wga_ref, wua_ref, wda_ref, xs[:, d:d + 1])
                      + ffn(wgb_ref, wub_ref, wdb_ref, xs[:, d + 1:d + 2]))

    @pl.when(valid_ref[ti] == 0)
    def _():
        o_ref[...] = jnp.zeros(o_ref.shape, F32)


def _experts(tile_ea, tile_eb, tile_valid, xs, wg_b, wu_b, wd_b):
    r, w = xs.shape
    d = w - ROUTE_EXTRA
    f = wg_b.shape[2]
    tm = EXPERT_TILE
    wa = lambda blk: pl.BlockSpec(blk, lambda i, ea, eb, va: (ea[i], 0, 0))
    wb = lambda blk: pl.BlockSpec(blk, lambda i, ea, eb, va: (eb[i], 0, 0))
    return pl.pallas_call(
        functools.partial(_expert_kernel, d=d),
        grid_spec=pltpu.PrefetchScalarGridSpec(
            num_scalar_prefetch=3,
            grid=(r // tm,),
            in_specs=[pl.BlockSpec((tm, w), lambda i, ea, eb, va: (i, 0)),
                      wa((1, d, f)), wa((1, d, f)), wa((1, f, d)),
                      wb((1, d, f)), wb((1, d, f)), wb((1, f, d))],
            out_specs=pl.BlockSpec((tm, d), lambda i, ea, eb, va: (i, 0))),
        out_shape=jax.ShapeDtypeStruct((r, d), F32),
        compiler_params=_params("arbitrary"),
        name="expert_ffn",
    )(tile_ea, tile_eb, tile_valid, xs, wg_b, wu_b, wd_b, wg_b, wu_b, wd_b)


def _combine_kernel(pos_ref, y_ref, mod_ref, lng_ref, lnb_ref, os_ref, out_ref, buf, sem, *, alpha, per_row):
    t = y_ref.shape[1]
    k = pl.program_id(0) * pl.num_programs(1) + pl.program_id(1)
    n_tiles = pl.num_programs(0) * pl.num_programs(1)

    def fetch(tile):
        slot = tile % 2
        _issue_row_copies(os_ref, buf.at[slot], sem.at[slot], lambda r: pos_ref[tile * t + r], lambda r: r, t)

    @pl.when(k == 0)
    def _():
        fetch(k)

    @pl.when(k + 1 < n_tiles)
    def _():
        fetch(k + 1)

    slot = k % 2
    _wait_row_copies(os_ref, buf.at[slot], sem.at[slot], t)
    gate2 = _mod_get(mod_ref, 5, per_row)
    out_ref[0] = _layer_norm(alpha * y_ref[0] + gate2 * buf[slot], lng_ref[...], lnb_ref[...])


def _combine(pos, y1, mod, ln_g, ln_b, o_sorted, *, alpha, per_row):
    nb, s, d = y1.shape
    t = min(SEQ_TILE, s)
    tile = lambda b, i, pos: (b, i, 0)
    const2 = lambda b, i, pos: (0, 0)
    mod_spec = (pl.BlockSpec(mod.shape, lambda b, i, pos: (0, 0, 0)) if per_row
                else pl.BlockSpec((1, 6, d), lambda b, i, pos: (b, 0, 0)))
    return pl.pallas_call(
        functools.partial(_combine_kernel, alpha=alpha, per_row=per_row),
        grid_spec=pltpu.PrefetchScalarGridSpec(
            num_scalar_prefetch=1,
            grid=(nb, s // t),
            in_specs=[pl.BlockSpec((1, t, d), tile), mod_spec,
                      pl.BlockSpec((1, d), const2), pl.BlockSpec((1, d), const2),
                      pl.BlockSpec(memory_space=pl.ANY)],
            out_specs=pl.BlockSpec((1, t, d), tile),
            scratch_shapes=[pltpu.VMEM((2, t, d), F32), pltpu.SemaphoreType.DMA((2,))]),
        out_shape=jax.ShapeDtypeStruct((nb, s, d), F32),
        compiler_params=_params("arbitrary", "arbitrary"),
        name="combine_rows",
    )(pos, y1, mod, ln_g, ln_b, o_sorted)


def _sample_inproj_kernel(x_ref, shift_ref, scale_ref, win_ref, hist_ref, wpool_ref, pscale_ref,
                          q_ref, k_ref, v_ref, p_ref, pm_ref, *, aw, n_seq, n_new, past_len):
    u = (x_ref[...] * (1.0 + scale_ref[...]) + shift_ref[...]).astype(BF16)
    acc = _dot(u, win_ref[...])
    q_ref[...] = (acc[:, 0:aw] * Q_SCALE).astype(BF16)
    k_ref[...] = acc[:, aw:2 * aw]
    v_ref[...] = acc[:, 2 * aw:3 * aw]
    p = acc[:, 3 * aw:]
    p_ref[...] = p
    n_hist = hist_ref.shape[0]
    rows = [hist_ref[j] for j in range(n_hist)] + [p[t * n_seq:(t + 1) * n_seq] for t in range(n_new)]
    for g, w in enumerate(POOL_WINDOWS):
        cols = slice(g * POOL_GROUP, (g + 1) * POOL_GROUP)
        mixed = []
        for t in range(n_new):
            end = n_hist + t + 1
            start = max(end - w, 0)
            win_sum = rows[start][:, cols]
            for j in range(start + 1, end):
                win_sum = win_sum + rows[j][:, cols]
            count = float(min(w, past_len + t + 1))
            mixed.append(win_sum / count - rows[n_hist + t][:, cols])
        mixed = jnp.concatenate(mixed, axis=0)
        pm_ref[:, cols] = (_dot(mixed.astype(BF16), wpool_ref[g]) * pscale_ref[:, cols]).astype(BF16)


def _sample_inproj(x, shift, scale, win_b, hist_t, wpool_b, pscale, *, n_seq, n_new, past_len):
    n, d = x.shape
    ew = win_b.shape[1]
    aw = (ew - 4 * POOL_GROUP) // 3
    pw = ew - 3 * aw
    return pl.pallas_call(
        functools.partial(_sample_inproj_kernel, aw=aw, n_seq=n_seq, n_new=n_new, past_len=past_len),
        out_shape=[jax.ShapeDtypeStruct((n, aw), BF16), jax.ShapeDtypeStruct((n, aw), F32),
                   jax.ShapeDtypeStruct((n, aw), F32), jax.ShapeDtypeStruct((n, pw), F32),
                   jax.ShapeDtypeStruct((n, pw), BF16)],
        compiler_params=pltpu.CompilerParams(vmem_limit_bytes=VMEM_LIMIT_BYTES),
        name="sample_inproj",
    )(x, shift, scale, win_b, hist_t, wpool_b, pscale)


def _paged_attn_kernel(pt_ref, lam_ref, q_ref, kn_ref, vn_ref, bias_ref, bn_ref, g_ref, *rest,
                       pps, n_new, lam_init):
    del pt_ref
    ck = rest[0:pps]
    cv = rest[pps:2 * pps]
    o_ref = rest[2 * pps]
    m_sc, l_sc, acc_sc = rest[2 * pps + 1:]
    step = pl.program_id(1)
    last = pl.num_programs(1) - 1
    heads = [slice(h * V_DIM, (h + 1) * V_DIM) for h in range(N_HEADS)]
    page_rows = ck[0].shape[0]

    @pl.when(step == 0)
    def _():
        m_sc[...] = jnp.full(m_sc.shape, -jnp.inf, F32)
        l_sc[...] = jnp.zeros(l_sc.shape, F32)
        acc_sc[...] = jnp.zeros(acc_sc.shape, F32)

    qn = q_ref[0]
    q_rows = jnp.concatenate([_split_branches(qn[:, cols]) for cols in heads], axis=0)
    bias = bias_ref[jnp.where(step == last, 1, 0)]
    s = jnp.concatenate([_dot_nt(q_rows, ck[pg][...].astype(BF16)) for pg in range(pps)], axis=1) + bias

    def past_values(p):
        acc = _dot(p[:, 0:page_rows], cv[0][...].astype(BF16))
        for pg in range(1, pps):
            acc = acc + _dot(p[:, pg * page_rows:(pg + 1) * page_rows], cv[pg][...].astype(BF16))
        return acc

    _online_update(s, past_values, m_sc, l_sc, acc_sc)

    @pl.when(step == last)
    def _():
        sn = _dot_nt(q_rows, kn_ref[0].astype(BF16)) + bn_ref[...]
        _online_update(sn, lambda p: _dot(p, vn_ref[0].astype(BF16)), m_sc, l_sc, acc_sc)
        lam = lam_ref[0]
        for h, cols in enumerate(heads):
            rows = slice(h * 2 * n_new, (h + 1) * 2 * n_new)
            o = _diff_head_out(acc_sc[rows, :], l_sc[rows, :], n_new, lam, g_ref[...], lam_init)
            o_ref[0, :, cols] = o.astype(BF16)


def _page_index(n, s, pt_ref, *, first_page, pg, pps, n_pages):
    return (first_page + pt_ref[n * n_pages + s * pps + pg], 0)


def _paged_attention(page_table, lam, q, k_new, v_new, bias_past, bias_new, subln_g, cache_k, cache_v, *,
                     layer, lam_init):
    n_seq, n_pages = page_table.shape
    n_new, aw = q.shape[1], q.shape[2]
    n_phys, page = cache_k.shape[1], cache_k.shape[2]
    pps = math.gcd(PAGES_PER_STEP, n_pages)
    n_steps = n_pages // pps
    n_rows = N_HEADS * 2 * n_new
    ck = cache_k.reshape(-1, V_DIM)
    cv = cache_v.reshape(-1, V_DIM)
    page_specs = [pl.BlockSpec((page * N_HEADS, V_DIM),
                               functools.partial(_page_index, first_page=layer * n_phys, pg=pg, pps=pps,
                                                 n_pages=n_pages))
                  for pg in range(pps)]
    per_seq = lambda n, s, pt: (n, 0, 0)
    return pl.pallas_call(
        functools.partial(_paged_attn_kernel, pps=pps, n_new=n_new, lam_init=lam_init),
        grid_spec=pltpu.PrefetchScalarGridSpec(
            num_scalar_prefetch=1,
            grid=(n_seq, n_steps),
            in_specs=[pl.BlockSpec(memory_space=pltpu.SMEM),
                      pl.BlockSpec((1, n_new, aw), per_seq),
                      pl.BlockSpec((1,) + k_new.shape[1:], per_seq),
                      pl.BlockSpec((1,) + v_new.shape[1:], per_seq),
                      pl.BlockSpec(bias_past.shape, lambda n, s, pt: (0, 0, 0)),
                      pl.BlockSpec(bias_new.shape, lambda n, s, pt: (0, 0)),
                      pl.BlockSpec((1, V_DIM), lambda n, s, pt: (0, 0))] + page_specs + page_specs,
            out_specs=pl.BlockSpec((1, n_new, aw), per_seq),
            scratch_shapes=[pltpu.VMEM((n_rows, LANES), F32), pltpu.VMEM((n_rows, LANES), F32),
                            pltpu.VMEM((n_rows, V_DIM), F32)]),
        out_shape=jax.ShapeDtypeStruct((n_seq, n_new, aw), BF16),
        compiler_params=_params("arbitrary", "arbitrary"),
        name="paged_attention",
    )(page_table.reshape(-1), lam, q, k_new, v_new, bias_past, bias_new, subln_g, *([ck] * pps), *([cv] * pps))


def _rel_bias_lookup(rel_bias, dist):
    n = jnp.maximum(dist, 0)
    max_exact = N_BUCKETS // 2
    nf = jnp.maximum(n, 1).astype(F32)
    large = max_exact + (jnp.log(nf / max_exact) / math.log(MAX_DISTANCE / max_exact)
                         * (N_BUCKETS - max_exact)).astype(jnp.int32)
    large = jnp.minimum(large, N_BUCKETS - 1)
    bucket = jnp.where(n < max_exact, n, large)
    onehot = (bucket.reshape(1, -1) == jnp.arange(N_BUCKETS)[:, None]).astype(F32)
    table = jnp.dot(rel_bias.astype(F32).T, onehot, precision=lax.Precision.HIGHEST) * LOG2E
    return table.reshape((rel_bias.shape[1],) + dist.shape)


def _prompt_bias_tables(rel_bias, tq):
    i = jnp.arange(tq)[:, None]
    j = jnp.arange(tq)[None, :]
    diag = jnp.where((j <= i)[None], _rel_bias_lookup(rel_bias, i - j), -jnp.inf)
    sub = _rel_bias_lookup(rel_bias, tq + i - j)
    both = lambda b: jnp.concatenate([b, b], axis=1)
    return both(diag), both(sub), rel_bias[N_BUCKETS - 1].astype(F32) * LOG2E


def _head_interleaved_bias(rel_bias, base, n_new, n_keys, causal):
    lane = jnp.arange(n_keys * N_HEADS)
    key, lane_head = lane // N_HEADS, lane % N_HEADS
    t = jnp.arange(n_new)[:, None]
    table = _rel_bias_lookup(rel_bias, base + t - key[None, :])
    keep = lane_head[None, None, :] == jnp.arange(N_HEADS)[:, None, None]
    if causal:
        keep = keep & (key[None, :] <= t)[None]
    table = jnp.where(keep, table, -jnp.inf)
    return jnp.concatenate([table, table], axis=1).reshape(N_HEADS * 2 * n_new, n_keys * N_HEADS)


def _sample_bias_tables(rel_bias, past_len, n_new, step_keys):
    assert step_keys >= MAX_DISTANCE
    far = _head_interleaved_bias(rel_bias, past_len, n_new, step_keys, causal=False)
    near = _head_interleaved_bias(rel_bias, step_keys, n_new, step_keys, causal=False)
    new = _head_interleaved_bias(rel_bias, 0, n_new, LANES // N_HEADS, causal=True)
    return jnp.stack([far, near]), new


def _routing_plan(route_p, cnt_p, route_s, cnt_s, n_rows):
    tm = EXPERT_TILE
    bucket_p = route_p[..., 0].reshape(-1).astype(jnp.int32)
    rank_p = route_p[..., 1].reshape(-1).astype(jnp.int32)
    bucket_s = route_s[..., 0].reshape(-1).astype(jnp.int32)
    rank_s = route_s[..., 1].reshape(-1).astype(jnp.int32)
    cp = cnt_p[0, :N_PAIR_BUCKETS].astype(jnp.int32)
    cs = cnt_s[0, :N_PAIR_BUCKETS].astype(jnp.int32)
    tiles = (cp + cs + tm - 1) // tm
    tile_end = jnp.cumsum(tiles)
    off = (tile_end - tiles) * tm
    lookup = lambda table, idx: jnp.sum(
        jnp.where(idx[:, None] == jnp.arange(table.shape[0])[None, :], table[None, :], 0), axis=1)
    pos_p = lookup(off, bucket_p) + rank_p
    pos_s = lookup(off + cp, bucket_s) + rank_s
    n_tiles = n_rows // tm
    n_used = tile_end[-1]
    ti = jnp.minimum(jnp.arange(n_tiles), n_used - 1)
    tile_bucket = jnp.sum((ti[:, None] >= tile_end[None, :]).astype(jnp.int32), axis=1)
    pair = np.array(EXPERT_PAIRS, np.int32)
    base = np.arange(N_PAIR_BUCKETS) // len(EXPERT_PAIRS) * EXP_PER_GROUP
    tile_ea = lookup(jnp.asarray(base + pair[np.arange(N_PAIR_BUCKETS) % len(EXPERT_PAIRS), 0]), tile_bucket)
    tile_eb = lookup(jnp.asarray(base + pair[np.arange(N_PAIR_BUCKETS) % len(EXPERT_PAIRS), 1]), tile_bucket)
    tile_valid = (jnp.arange(n_tiles) < n_used).astype(jnp.int32)
    return pos_p, pos_s, tile_ea.astype(jnp.int32), tile_eb.astype(jnp.int32), tile_valid


def kernel(x_prompt, x_sample, c_prompt, c_sample, cache_k, cache_v, state_pool, page_table, rel_bias, w_ada, b_ada, w_in, lambda_q1, lambda_k1, lambda_q2, lambda_k2, subln_g, w_pool, pool_scale, w_o, ln1_g, ln1_b, w_router_group, b_router_group, w_router_expert, b_router_expert, w_gate, w_up, w_down, ln2_g, ln2_b):
    depth = w_in.shape[0]
    nb, seq, d = x_prompt.shape
    n_seq, n_new, _ = x_sample.shape
    n_pages = page_table.shape[1]
    page = cache_k.shape[2]
    past_len = n_pages * page
    aw = N_HEADS * V_DIM
    pw = pool_scale.shape[1]
    alpha = (2 * depth) ** 0.25
    tq = min(ATTN_TILE, seq)
    assert tq >= MAX_DISTANCE and seq % tq == 0 and seq % min(SEQ_TILE, seq) == 0
    assert past_len >= POOL_HIST and n_new <= 8
    n_tok_p = nb * seq
    n_tok_s = n_seq * n_new
    assert n_tok_p % min(SEQ_TILE, n_tok_p) == 0
    n_rows = ((n_tok_p + n_tok_s) // EXPERT_TILE + N_PAIR_BUCKETS) * EXPERT_TILE
    pps = math.gcd(PAGES_PER_STEP, n_pages)
    n_steps = n_pages // pps

    bias_diag, bias_sub, c_far = _prompt_bias_tables(rel_bias, tq)
    bias_past, bias_new = _sample_bias_tables(rel_bias, past_len, n_new, pps * page)
    c_all = jnp.concatenate([c_prompt, c_sample], axis=0)
    xs_tm = jnp.transpose(x_sample, (1, 0, 2)).reshape(n_tok_s, d)

    xp, xs_cur = x_prompt, xs_tm
    kp_l, vp_l, pp_l, ks_l, vs_l, ps_l = [], [], [], [], [], []
    for l in range(depth):
        lam_init = 0.8 - 0.6 * math.exp(-0.3 * l)
        lam = (jnp.exp(jnp.sum(lambda_q1[l].astype(F32) * lambda_k1[l].astype(F32)))
               - jnp.exp(jnp.sum(lambda_q2[l].astype(F32) * lambda_k2[l].astype(F32))) + lam_init).reshape(1)
        win_b = w_in[l].astype(BF16)
        wpool_b = w_pool[l].astype(BF16)
        wo_b = w_o[l].astype(BF16)
        wr = jnp.concatenate([w_router_group[l], jnp.transpose(w_router_expert[l], (1, 0, 2)).reshape(d, N_EXPERTS)], axis=1)
        wr_b = jnp.pad(wr, ((0, 0), (0, LANES - wr.shape[1]))).astype(BF16)
        br = jnp.pad(jnp.concatenate([b_router_group[l], b_router_expert[l].reshape(-1)]).astype(F32),
                     (0, LANES - N_EXP_GROUPS - N_EXPERTS)).reshape(1, LANES)
        wg_b = w_gate[l].reshape(N_EXPERTS, d, -1).astype(BF16)
        wu_b = w_up[l].reshape(N_EXPERTS, d, -1).astype(BF16)
        wd_b = w_down[l].reshape(N_EXPERTS, -1, d).astype(BF16)
        pscale = pool_scale[l].reshape(1, pw)
        g_sub = subln_g[l].reshape(1, V_DIM)
        ln1g, ln1b = ln1_g[l].reshape(1, d), ln1_b[l].reshape(1, d)
        ln2g, ln2b = ln2_g[l].reshape(1, d), ln2_b[l].reshape(1, d)

        m_all = _modulation(c_all, w_ada[l], b_ada[l]).reshape(nb + n_seq, 6, d)
        mod_p = m_all[:nb]
        mod_s = jnp.tile(jnp.transpose(m_all[nb:], (1, 0, 2)), (1, n_new, 1))

        q_p, k_p, v_p, kb_p, vb_p, pm_p, ph_p = _prompt_inproj(xp, mod_p, win_b, wpool_b, pscale)
        o_p = _prompt_attention(q_p, kb_p, vb_p, bias_diag, bias_sub, c_far, lam, g_sub, lam_init)
        y1_p, tokx_p, route_p, cnt_p = _outproj_router(o_p, pm_p, xp, mod_p, wo_b, ln1g, ln1b, wr_b, br,
                                                       alpha=alpha, per_row=False)

        hist = state_pool[l]
        q_s, k_s, v_s, p_s, pm_s = _sample_inproj(xs_cur, mod_s[0], mod_s[1], win_b, jnp.transpose(hist, (1, 0, 2)),
                                                  wpool_b, pscale, n_seq=n_seq, n_new=n_new, past_len=past_len)
        seq_major = lambda a: jnp.transpose(a.reshape(n_new, n_seq, -1), (1, 0, 2))
        new_rows = lambda a: jnp.pad(seq_major(a).reshape(n_seq, n_new * N_HEADS, V_DIM),
                                     ((0, 0), (0, LANES - n_new * N_HEADS), (0, 0)))
        o_s = _paged_attention(page_table, lam, seq_major(q_s), new_rows(k_s), new_rows(v_s), bias_past, bias_new, g_sub,
                               cache_k, cache_v, layer=l, lam_init=lam_init)
        o_s_tm = jnp.transpose(o_s, (1, 0, 2)).reshape(1, n_tok_s, aw)
        y1_s, tokx_s, route_s, cnt_s = _outproj_router(o_s_tm, pm_s[None], xs_cur[None], mod_s, wo_b, ln1g, ln1b,
                                                       wr_b, br, alpha=alpha, per_row=True)

        pos_p, pos_s, tile_ea, tile_eb, tile_valid = _routing_plan(route_p, cnt_p, route_s, cnt_s, n_rows)
        xs_sorted = jnp.zeros((n_rows, d + ROUTE_EXTRA), F32)
        xs_sorted = _dispatch(pos_p, tokx_p.reshape(n_tok_p, -1), xs_sorted)
        xs_sorted = _dispatch(pos_s, tokx_s.reshape(n_tok_s, -1), xs_sorted)
        o_sorted = _experts(tile_ea, tile_eb, tile_valid, xs_sorted, wg_b, wu_b, wd_b)
        xp = _combine(pos_p, y1_p, mod_p, ln2g, ln2b, o_sorted, alpha=alpha, per_row=False)
        xs_cur = _combine(pos_s, y1_s, mod_s, ln2g, ln2b, o_sorted, alpha=alpha, per_row=True)[0]

        kp_l.append(k_p.reshape(nb, seq, N_HEADS, V_DIM))
        vp_l.append(v_p.reshape(nb, seq, N_HEADS, V_DIM))
        pp_l.append(ph_p)
        ks_l.append(seq_major(k_s).reshape(n_seq, n_new, N_HEADS, V_DIM))
        vs_l.append(seq_major(v_s).reshape(n_seq, n_new, N_HEADS, V_DIM))
        ps_l.append(jnp.concatenate([hist, seq_major(p_s)], axis=1)[:, -POOL_HIST:])

    y_sample = jnp.transpose(xs_cur.reshape(n_new, n_seq, d), (1, 0, 2))
    return (xp, y_sample, jnp.stack(kp_l), jnp.stack(vp_l), jnp.stack(pp_l),
            jnp.stack(ks_l), jnp.stack(vs_l), jnp.stack(ps_l))
```

```python
import functools
import math

import numpy as np
import jax
import jax.numpy as jnp
from jax import lax
from jax.experimental import pallas as pl
from jax.experimental.pallas import tpu as pltpu

F32 = jnp.float32
BF16 = jnp.bfloat16

N_HEADS = 4
HEAD_DIM = 64
V_DIM = 2 * HEAD_DIM
POOL_WINDOWS = (2, 4, 8, 16)
POOL_GROUP = 128
POOL_HIST = max(POOL_WINDOWS) - 1
N_BUCKETS = 32
MAX_DISTANCE = 128
N_EXP_GROUPS = 4
EXP_PER_GROUP = 4
N_EXPERTS = N_EXP_GROUPS * EXP_PER_GROUP
EXPERT_PAIRS = ((0, 1), (0, 2), (0, 3), (1, 2), (1, 3), (2, 3))
N_PAIR_BUCKETS = N_EXP_GROUPS * len(EXPERT_PAIRS)
LN_EPS = 1e-5
RMS_EPS = 1e-5

LANES = 128
POOL_HALO = 16
VMEM_LIMIT_BYTES = 48 * 1024 * 1024
SEQ_TILE = 512
ATTN_TILE = 256
EXPERT_TILE = 256
PAGES_PER_STEP = 16
ROUTE_EXTRA = LANES
LOG2E = math.log2(math.e)
Q_SCALE = HEAD_DIM ** -0.5 * LOG2E


def _params(*sem):
    return pltpu.CompilerParams(dimension_semantics=sem, vmem_limit_bytes=VMEM_LIMIT_BYTES)


def _dot(a, b):
    return jnp.dot(a, b, preferred_element_type=F32)


def _dot_nt(a, b):
    return lax.dot_general(a, b, (((1,), (1,)), ((), ())), preferred_element_type=F32)


def _sigmoid(x):
    return 1.0 / (1.0 + jnp.exp(-x))


def _mod_kernel(c_ref, w_ref, b_ref, o_ref):
    c = c_ref[...]
    s = c * _sigmoid(c)
    s_hi = s.astype(BF16)
    s_lo = (s - s_hi.astype(F32)).astype(BF16)
    w = w_ref[...]
    w_hi = w.astype(BF16)
    w_lo = (w - w_hi.astype(F32)).astype(BF16)
    o_ref[...] = _dot(s_hi, w_hi) + _dot(s_lo, w_hi) + _dot(s_hi, w_lo) + b_ref[...]


def _modulation(c_all, w_ada, b_ada):
    n, d = c_all.shape
    e = w_ada.shape[1]
    bn = 1024
    return pl.pallas_call(
        _mod_kernel,
        grid=(e // bn,),
        in_specs=[pl.BlockSpec((n, d), lambda j: (0, 0)),
                  pl.BlockSpec((d, bn), lambda j: (0, j)),
                  pl.BlockSpec((1, bn), lambda j: (0, j))],
        out_specs=pl.BlockSpec((n, bn), lambda j: (0, j)),
        out_shape=jax.ShapeDtypeStruct((n, e), F32),
        compiler_params=_params("arbitrary"),
        name="modulation",
    )(c_all, w_ada, b_ada.reshape(1, e))


def _pool_mixed(ext, g, w, inv_cnt, rows):
    eg = ext[:, g * POOL_GROUP:(g + 1) * POOL_GROUP]
    s = eg
    step = 1
    while step < w:
        s = s + pltpu.roll(s, step, 0)
        step *= 2
    return s[POOL_HALO:POOL_HALO + rows] * inv_cnt - eg[POOL_HALO:POOL_HALO + rows]


def _inproj_kernel(x_ref, mod_ref, win_ref, wpool_ref, pscale_ref,
                   q_ref, k_ref, v_ref, kb_ref, vb_ref, pm_ref, ph_ref, ext_ref, *, ts, aw):
    i = pl.program_id(1)

    @pl.when(i == 0)
    def _():
        ext_ref[0:POOL_HALO, :] = jnp.zeros((POOL_HALO, ext_ref.shape[1]), F32)

    @pl.when(i > 0)
    def _():
        ext_ref[0:POOL_HALO, :] = ext_ref[ts:ts + POOL_HALO, :]

    shift = mod_ref[0, 0:1, :]
    scale = mod_ref[0, 1:2, :]
    u = (x_ref[0] * (1.0 + scale) + shift).astype(BF16)
    acc = _dot(u, win_ref[...])
    q_ref[0] = (acc[:, 0:aw] * Q_SCALE).astype(BF16)
    k = acc[:, aw:2 * aw]
    v = acc[:, 2 * aw:3 * aw]
    for h in range(N_HEADS):
        k_ref[0, pl.ds(h, ts, stride=N_HEADS), :] = k[:, h * V_DIM:(h + 1) * V_DIM]
        v_ref[0, pl.ds(h, ts, stride=N_HEADS), :] = v[:, h * V_DIM:(h + 1) * V_DIM]
    kb_ref[0] = k.astype(BF16)
    vb_ref[0] = v.astype(BF16)
    p = acc[:, 3 * aw:]
    ext_ref[POOL_HALO:POOL_HALO + ts, :] = p
    ext = ext_ref[...]
    pos = i * ts + lax.broadcasted_iota(jnp.int32, (ts, 1), 0)
    for g, w in enumerate(POOL_WINDOWS):
        inv_cnt = 1.0 / jnp.minimum(w, pos + 1).astype(F32)
        mixed = _pool_mixed(ext, g, w, inv_cnt, ts)
        cols = slice(g * POOL_GROUP, (g + 1) * POOL_GROUP)
        pm_ref[0, :, cols] = (_dot(mixed.astype(BF16), wpool_ref[g]) * pscale_ref[:, cols]).astype(BF16)

    @pl.when(i == pl.num_programs(1) - 1)
    def _():
        ph_ref[0] = p[ts - POOL_HIST:ts, :]


def _prompt_inproj(x, mod, win_b, wpool_b, pscale):
    nb, s, d = x.shape
    ew = win_b.shape[1]
    aw = (ew - 4 * POOL_GROUP) // 3
    pw = ew - 3 * aw
    ts = min(SEQ_TILE, s)
    nt = s // ts
    tile = lambda b, i: (b, i, 0)
    const2 = lambda b, i: (0, 0)
    return pl.pallas_call(
        functools.partial(_inproj_kernel, ts=ts, aw=aw),
        grid=(nb, nt),
        in_specs=[pl.BlockSpec((1, ts, d), tile),
                  pl.BlockSpec((1, 6, d), lambda b, i: (b, 0, 0)),
                  pl.BlockSpec((d, ew), const2),
                  pl.BlockSpec(wpool_b.shape, lambda b, i: (0, 0, 0)),
                  pl.BlockSpec((1, pw), const2)],
        out_specs=[pl.BlockSpec((1, ts, aw), tile),
                   pl.BlockSpec((1, ts * N_HEADS, V_DIM), tile),
                   pl.BlockSpec((1, ts * N_HEADS, V_DIM), tile),
                   pl.BlockSpec((1, ts, aw), tile), pl.BlockSpec((1, ts, aw), tile),
                   pl.BlockSpec((1, ts, pw), tile),
                   pl.BlockSpec((1, POOL_HIST, pw), lambda b, i: (b, 0, 0))],
        out_shape=[jax.ShapeDtypeStruct((nb, s, aw), BF16),
                   jax.ShapeDtypeStruct((nb, s * N_HEADS, V_DIM), F32),
                   jax.ShapeDtypeStruct((nb, s * N_HEADS, V_DIM), F32),
                   jax.ShapeDtypeStruct((nb, s, aw), BF16),
                   jax.ShapeDtypeStruct((nb, s, aw), BF16),
                   jax.ShapeDtypeStruct((nb, s, pw), BF16),
                   jax.ShapeDtypeStruct((nb, POOL_HIST, pw), F32)],
        scratch_shapes=[pltpu.VMEM((ts + POOL_HALO, pw), F32)],
        compiler_params=_params("arbitrary", "arbitrary"),
        name="prompt_inproj",
    )(x, mod, win_b, wpool_b, pscale)


def _split_branches(qh):
    lane = lax.broadcasted_iota(jnp.int32, qh.shape, 1)
    zero = jnp.zeros_like(qh)
    return jnp.concatenate([jnp.where(lane < HEAD_DIM, qh, zero), jnp.where(lane >= HEAD_DIM, qh, zero)], axis=0)


def _lane_tile(a, width):
    return jnp.concatenate([a] * (width // LANES), axis=1) if width > LANES else a


def _online_update(s, values, m_ref, l_ref, acc_ref, first=False):
    rows, width = s.shape
    m_new = jnp.broadcast_to(jnp.max(s, axis=1, keepdims=True), (rows, LANES))
    if not first:
        m_old = m_ref[...]
        m_new = jnp.maximum(m_old, m_new)
    p = jnp.exp2(s - _lane_tile(m_new, width))
    pv = values(p.astype(BF16))
    l_new = jnp.broadcast_to(jnp.sum(p, axis=1, keepdims=True), (rows, LANES))
    if first:
        acc_ref[...] = pv
        l_ref[...] = l_new
    else:
        alpha = jnp.exp2(m_old - m_new)
        acc_ref[...] = _lane_tile(alpha, pv.shape[1]) * acc_ref[...] + pv
        l_ref[...] = alpha * l_ref[...] + l_new
    m_ref[...] = m_new


def _diff_head_out(acc, l, t, lam, g, lam_init):
    o = acc[0:t] / l[0:t] - lam * (acc[t:2 * t] / l[t:2 * t])
    o = o * lax.rsqrt(jnp.mean(o * o, axis=-1, keepdims=True) + RMS_EPS) * g
    return o * (1.0 - lam_init)


def _attn_kernel(cfar_ref, lam_ref, q_ref, k_ref, v_ref, bd_ref, be_ref, g_ref, o_ref,
                 qs_sc, m_sc, l_sc, acc_sc, *, tq, lam_init):
    qi = pl.program_id(1)
    heads = [slice(h * V_DIM, (h + 1) * V_DIM) for h in range(N_HEADS)]
    for h, cols in enumerate(heads):
        qs_sc[h] = _split_branches(q_ref[0, :, cols])

    def step(j, bias, first):
        rows = pl.ds(pl.multiple_of(j * tq, tq), tq)
        for h, cols in enumerate(heads):
            s = _dot_nt(qs_sc[h], k_ref[0, rows, cols]) + bias(h)
            _online_update(s, lambda p, cols=cols: _dot(p, v_ref[0, rows, cols]),
                           m_sc.at[h], l_sc.at[h], acc_sc.at[h], first=first)

    step(qi, lambda h: bd_ref[h], True)

    @pl.when(qi >= 1)
    def _():
        step(qi - 1, lambda h: be_ref[h], False)

    def far(j, carry):
        step(j, lambda h: cfar_ref[h], False)
        return carry

    lax.fori_loop(0, jnp.maximum(qi - 1, 0), far, 0)
    lam = lam_ref[0]
    for h, cols in enumerate(heads):
        o = _diff_head_out(acc_sc[h], l_sc[h], tq, lam, g_ref[...], lam_init)
        o_ref[0, :, cols] = o.astype(BF16)


def _prompt_attention(q, kb, vb, bias_diag, bias_sub, c_far, lam, subln_g, lam_init):
    nb, s, aw = q.shape
    tq = bias_diag.shape[2]
    smem = pl.BlockSpec(memory_space=pltpu.SMEM)
    const3 = lambda b, i: (0, 0, 0)
    return pl.pallas_call(
        functools.partial(_attn_kernel, tq=tq, lam_init=lam_init),
        grid=(nb, s // tq),
        in_specs=[smem, smem,
                  pl.BlockSpec((1, tq, aw), lambda b, i: (b, i, 0)),
                  pl.BlockSpec((1, s, aw), lambda b, i: (b, 0, 0)),
                  pl.BlockSpec((1, s, aw), lambda b, i: (b, 0, 0)),
                  pl.BlockSpec(bias_diag.shape, const3),
                  pl.BlockSpec(bias_sub.shape, const3),
                  pl.BlockSpec((1, V_DIM), lambda b, i: (0, 0))],
        out_specs=pl.BlockSpec((1, tq, aw), lambda b, i: (b, i, 0)),
        out_shape=jax.ShapeDtypeStruct((nb, s, aw), BF16),
        scratch_shapes=[pltpu.VMEM((N_HEADS, 2 * tq, V_DIM), BF16),
                        pltpu.VMEM((N_HEADS, 2 * tq, LANES), F32), pltpu.VMEM((N_HEADS, 2 * tq, LANES), F32),
                        pltpu.VMEM((N_HEADS, 2 * tq, V_DIM), F32)],
        compiler_params=_params("arbitrary", "arbitrary"),
        name="prompt_attention",
    )(c_far, lam, q, kb, vb, bias_diag, bias_sub, subln_g)


def _layer_norm(z, g, b):
    mu = jnp.mean(z, axis=-1, keepdims=True)
    zc = z - mu
    var = jnp.mean(zc * zc, axis=-1, keepdims=True)
    return zc * lax.rsqrt(var + LN_EPS) * g + b


def _mod_get(mod_ref, j, per_row):
    return mod_ref[j] if per_row else mod_ref[0, j:j + 1, :]


def _outproj_router_kernel(o_ref, pm_ref, x_ref, mod_ref, wo_ref, lng_ref, lnb_ref, wr_ref, br_ref,
                           y_ref, tokx_ref, route_ref, cnt_ref, carry_sc, *, alpha, per_row):
    t = x_ref.shape[1]
    d = x_ref.shape[2]
    aw = o_ref.shape[2]

    @pl.when((pl.program_id(0) == 0) & (pl.program_id(1) == 0))
    def _():
        carry_sc[...] = jnp.zeros(carry_sc.shape, F32)

    h = _dot(o_ref[0], wo_ref[0:aw, :]) + _dot(pm_ref[0], wo_ref[aw:, :])
    gate1 = _mod_get(mod_ref, 2, per_row)
    y1 = _layer_norm(alpha * x_ref[0] + gate1 * h, lng_ref[...], lnb_ref[...])
    y_ref[0] = y1
    tok = y1 * (1.0 + _mod_get(mod_ref, 4, per_row)) + _mod_get(mod_ref, 3, per_row)
    tokx_ref[0, :, 0:d] = tok

    logits = _dot(tok.astype(BF16), wr_ref[...]) + br_ref[...]
    lane = lax.broadcasted_iota(jnp.int32, logits.shape, 1)
    lane_f = lane.astype(F32)
    neg = jnp.full_like(logits, -jnp.inf)

    def first_argmax(vals, vmax):
        return jnp.min(jnp.where(vals == vmax, lane_f, float(LANES)), axis=1, keepdims=True).astype(jnp.int32)

    gl = jnp.where(lane < N_EXP_GROUPS, logits, neg)
    gmax = jnp.max(gl, axis=1, keepdims=True)
    gidx = first_argmax(gl, gmax)
    g_w = 1.0 / jnp.sum(jnp.exp(gl - gmax), axis=1, keepdims=True)
    lo_lane = N_EXP_GROUPS + EXP_PER_GROUP * gidx
    el = jnp.where((lane >= lo_lane) & (lane < lo_lane + EXP_PER_GROUP), logits, neg)
    v1 = jnp.max(el, axis=1, keepdims=True)
    i1 = first_argmax(el, v1)
    el2 = jnp.where(lane == i1, neg, el)
    v2 = jnp.max(el2, axis=1, keepdims=True)
    i2 = first_argmax(el2, v2)
    e21 = jnp.exp(v2 - v1)
    w1 = g_w / (1.0 + e21)
    w2 = g_w * e21 / (1.0 + e21)
    first_lo = i1 < i2
    cw_lo = jnp.where(first_lo, w1, w2)
    cw_hi = jnp.where(first_lo, w2, w1)
    a = jnp.minimum(i1, i2) - lo_lane
    b = jnp.maximum(i1, i2) - lo_lane
    pair_base = jnp.where(a == 0, 0, jnp.where(a == 1, 3, 5))
    bucket = gidx * len(EXPERT_PAIRS) + pair_base + (b - a - 1)

    xlane = lax.broadcasted_iota(jnp.int32, (t, ROUTE_EXTRA), 1)
    tokx_ref[0, :, d:] = jnp.where(xlane == 0, cw_lo, jnp.where(xlane == 1, cw_hi, 0.0))

    onehot = lane == bucket
    row = lax.broadcasted_iota(jnp.int32, (t, t), 0)
    col = lax.broadcasted_iota(jnp.int32, (t, t), 1)
    ltri = jnp.where(col < row, 1.0, 0.0).astype(BF16)
    prefix = _dot(ltri, jnp.where(onehot, 1.0, 0.0).astype(BF16)) + carry_sc[...]
    rank = jnp.sum(jnp.where(onehot, prefix, 0.0), axis=1, keepdims=True)
    carry_sc[...] = carry_sc[...] + jnp.sum(jnp.where(onehot, 1.0, 0.0), axis=0, keepdims=True)
    cnt_ref[...] = carry_sc[...]
    route_ref[0] = jnp.where(lane == 0, bucket.astype(F32), jnp.where(lane == 1, rank, 0.0))


def _outproj_router(o, pm, x, mod, wo_b, ln_g, ln_b, wr_b, br, *, alpha, per_row):
    nb, s, d = x.shape
    aw = o.shape[2]
    pw = pm.shape[2]
    t = min(SEQ_TILE, s)
    tile = lambda b, i: (b, i, 0)
    const2 = lambda b, i: (0, 0)
    mod_spec = (pl.BlockSpec(mod.shape, lambda b, i: (0, 0, 0)) if per_row
                else pl.BlockSpec((1, 6, d), lambda b, i: (b, 0, 0)))
    return pl.pallas_call(
        functools.partial(_outproj_router_kernel, alpha=alpha, per_row=per_row),
        grid=(nb, s // t),
        in_specs=[pl.BlockSpec((1, t, aw), tile), pl.BlockSpec((1, t, pw), tile), pl.BlockSpec((1, t, d), tile),
                  mod_spec,
                  pl.BlockSpec(wo_b.shape, const2), pl.BlockSpec((1, d), const2), pl.BlockSpec((1, d), const2),
                  pl.BlockSpec(wr_b.shape, const2), pl.BlockSpec((1, LANES), const2)],
        out_specs=[pl.BlockSpec((1, t, d), tile), pl.BlockSpec((1, t, d + ROUTE_EXTRA), tile),
                   pl.BlockSpec((1, t, LANES), tile), pl.BlockSpec((1, LANES), const2)],
        out_shape=[jax.ShapeDtypeStruct((nb, s, d), F32), jax.ShapeDtypeStruct((nb, s, d + ROUTE_EXTRA), F32),
                   jax.ShapeDtypeStruct((nb, s, LANES), F32), jax.ShapeDtypeStruct((1, LANES), F32)],
        scratch_shapes=[pltpu.VMEM((1, LANES), F32)],
        compiler_params=_params("arbitrary", "arbitrary"),
        name="outproj_router",
    )(o, pm, x, mod, wo_b, ln_g, ln_b, wr_b, br)


def _issue_row_copies(src_ref, dst_ref, sem, src_row, dst_row, n):
    for r in range(n):
        pltpu.make_async_copy(src_ref.at[pl.ds(src_row(r), 1)], dst_ref.at[pl.ds(dst_row(r), 1)],
                              sem).start(priority=r % 2)


def _wait_row_copies(src_ref, dst_ref, sem, n):
    pltpu.make_async_copy(src_ref.at[pl.ds(0, n)], dst_ref.at[pl.ds(0, n)], sem).wait()


def _dispatch_kernel(pos_ref, tok_ref, xs_in_ref, xs_ref, sem, *, t):
    del xs_in_ref
    base = pl.program_id(0) * t
    _issue_row_copies(tok_ref, xs_ref, sem, lambda r: r, lambda r: pos_ref[base + r], t)
    _wait_row_copies(tok_ref, xs_ref, sem, t)


def _dispatch(pos, tokx, xs):
    n, w = tokx.shape
    t = min(SEQ_TILE, n)
    return pl.pallas_call(
        functools.partial(_dispatch_kernel, t=t),
        grid_spec=pltpu.PrefetchScalarGridSpec(
            num_scalar_prefetch=1,
            grid=(n // t,),
            in_specs=[pl.BlockSpec((t, w), lambda i, pos: (i, 0)), pl.BlockSpec(memory_space=pl.ANY)],
            out_specs=pl.BlockSpec(memory_space=pl.ANY),
            scratch_shapes=[pltpu.SemaphoreType.DMA(())]),
        out_shape=jax.ShapeDtypeStruct(xs.shape, xs.dtype),
        input_output_aliases={2: 0},
        compiler_params=_params("arbitrary"),
        name="dispatch_rows",
    )(pos, tokx, xs)


def _expert_kernel(ea_ref, eb_ref, valid_ref, xs_ref, wga_ref, wua_ref, wda_ref, wgb_ref, wub_ref, wdb_ref,
                   o_ref, *, d):
    del ea_ref, eb_ref
    ti = pl.program_id(0)

    @pl.when(valid_ref[ti] == 1)
    def _():
        xs = xs_ref[...]
        x = xs[:, 0:d].astype(BF16)

        def ffn(wg_ref, wu_ref, wd_ref, cw):
            g = _dot(x, wg_ref[0])
            u = _dot(x, wu_ref[0])
            hid = (g * _sigmoid(g)) * u * cw
            return _dot(hid.astype(BF16), wd_ref[0])

        o_ref[...] = (ffn(wga_ref, wua_ref, wda_ref, xs[:, d:d + 1])
                      + ffn(wgb_ref, wub_ref, wdb_ref, xs[:, d + 1:d + 2]))

    @pl.when(valid_ref[ti] == 0)
    def _():
        o_ref[...] = jnp.zeros(o_ref.shape, F32)


def _experts(tile_ea, tile_eb, tile_valid, xs, wg_b, wu_b, wd_b):
    r, w = xs.shape
    d = w - ROUTE_EXTRA
    f = wg_b.shape[2]
    tm = EXPERT_TILE
    wa = lambda blk: pl.BlockSpec(blk, lambda i, ea, eb, va: (ea[i], 0, 0))
    wb = lambda blk: pl.BlockSpec(blk, lambda i, ea, eb, va: (eb[i], 0, 0))
    return pl.pallas_call(
        functools.partial(_expert_kernel, d=d),
        grid_spec=pltpu.PrefetchScalarGridSpec(
            num_scalar_prefetch=3,
            grid=(r // tm,),
            in_specs=[pl.BlockSpec((tm, w), lambda i, ea, eb, va: (i, 0)),
                      wa((1, d, f)), wa((1, d, f)), wa((1, f, d)),
                      wb((1, d, f)), wb((1, d, f)), wb((1, f, d))],
            out_specs=pl.BlockSpec((tm, d), lambda i, ea, eb, va: (i, 0))),
        out_shape=jax.ShapeDtypeStruct((r, d), F32),
        compiler_params=_params("arbitrary"),
        name="expert_ffn",
    )(tile_ea, tile_eb, tile_valid, xs, wg_b, wu_b, wd_b, wg_b, wu_b, wd_b)


def _combine_kernel(pos_ref, y_ref, mod_ref, lng_ref, lnb_ref, os_ref, out_ref, buf, sem, *, alpha, per_row):
    t = y_ref.shape[1]
    k = pl.program_id(0) * pl.num_programs(1) + pl.program_id(1)
    n_tiles = pl.num_programs(0) * pl.num_programs(1)

    def fetch(tile):
        slot = tile % 2
        _issue_row_copies(os_ref, buf.at[slot], sem.at[slot], lambda r: pos_ref[tile * t + r], lambda r: r, t)

    @pl.when(k == 0)
    def _():
        fetch(k)

    @pl.when(k + 1 < n_tiles)
    def _():
        fetch(k + 1)

    slot = k % 2
    _wait_row_copies(os_ref, buf.at[slot], sem.at[slot], t)
    gate2 = _mod_get(mod_ref, 5, per_row)
    out_ref[0] = _layer_norm(alpha * y_ref[0] + gate2 * buf[slot], lng_ref[...], lnb_ref[...])


def _combine(pos, y1, mod, ln_g, ln_b, o_sorted, *, alpha, per_row):
    nb, s, d = y1.shape
    t = min(SEQ_TILE, s)
    tile = lambda b, i, pos: (b, i, 0)
    const2 = lambda b, i, pos: (0, 0)
    mod_spec = (pl.BlockSpec(mod.shape, lambda b, i, pos: (0, 0, 0)) if per_row
                else pl.BlockSpec((1, 6, d), lambda b, i, pos: (b, 0, 0)))
    return pl.pallas_call(
        functools.partial(_combine_kernel, alpha=alpha, per_row=per_row),
        grid_spec=pltpu.PrefetchScalarGridSpec(
            num_scalar_prefetch=1,
            grid=(nb, s // t),
            in_specs=[pl.BlockSpec((1, t, d), tile), mod_spec,
                      pl.BlockSpec((1, d), const2), pl.BlockSpec((1, d), const2),
                      pl.BlockSpec(memory_space=pl.ANY)],
            out_specs=pl.BlockSpec((1, t, d), tile),
            scratch_shapes=[pltpu.VMEM((2, t, d), F32), pltpu.SemaphoreType.DMA((2,))]),
        out_shape=jax.ShapeDtypeStruct((nb, s, d), F32),
        compiler_params=_params("arbitrary", "arbitrary"),
        name="combine_rows",
    )(pos, y1, mod, ln_g, ln_b, o_sorted)


def _sample_inproj_kernel(x_ref, shift_ref, scale_ref, win_ref, hist_ref, wpool_ref, pscale_ref,
                          q_ref, k_ref, v_ref, p_ref, pm_ref, *, aw, n_seq, n_new, past_len):
    u = (x_ref[...] * (1.0 + scale_ref[...]) + shift_ref[...]).astype(BF16)
    acc = _dot(u, win_ref[...])
    q_ref[...] = (acc[:, 0:aw] * Q_SCALE).astype(BF16)
    k_ref[...] = acc[:, aw:2 * aw]
    v_ref[...] = acc[:, 2 * aw:3 * aw]
    p = acc[:, 3 * aw:]
    p_ref[...] = p
    n_hist = hist_ref.shape[0]
    rows = [hist_ref[j] for j in range(n_hist)] + [p[t * n_seq:(t + 1) * n_seq] for t in range(n_new)]
    for g, w in enumerate(POOL_WINDOWS):
        cols = slice(g * POOL_GROUP, (g + 1) * POOL_GROUP)
        mixed = []
        for t in range(n_new):
            end = n_hist + t + 1
            start = max(end - w, 0)
            win_sum = rows[start][:, cols]
            for j in range(start + 1, end):
                win_sum = win_sum + rows[j][:, cols]
            count = float(min(w, past_len + t + 1))
            mixed.append(win_sum / count - rows[n_hist + t][:, cols])
        mixed = jnp.concatenate(mixed, axis=0)
        pm_ref[:, cols] = (_dot(mixed.astype(BF16), wpool_ref[g]) * pscale_ref[:, cols]).astype(BF16)


def _sample_inproj(x, shift, scale, win_b, hist_t, wpool_b, pscale, *, n_seq, n_new, past_len):
    n, d = x.shape
    ew = win_b.shape[1]
    aw = (ew - 4 * POOL_GROUP) // 3
    pw = ew - 3 * aw
    return pl.pallas_call(
        functools.partial(_sample_inproj_kernel, aw=aw, n_seq=n_seq, n_new=n_new, past_len=past_len),
        out_shape=[jax.ShapeDtypeStruct((n, aw), BF16), jax.ShapeDtypeStruct((n, aw), F32),
                   jax.ShapeDtypeStruct((n, aw), F32), jax.ShapeDtypeStruct((n, pw), F32),
                   jax.ShapeDtypeStruct((n, pw), BF16)],
        compiler_params=pltpu.CompilerParams(vmem_limit_bytes=VMEM_LIMIT_BYTES),
        name="sample_inproj",
    )(x, shift, scale, win_b, hist_t, wpool_b, pscale)


def _paged_attn_kernel(pt_ref, lam_ref, q_ref, kn_ref, vn_ref, bias_ref, bn_ref, g_ref, *rest,
                       pps, n_new, lam_init):
    del pt_ref
    ck = rest[0:pps]
    cv = rest[pps:2 * pps]
    o_ref = rest[2 * pps]
    m_sc, l_sc, acc_sc = rest[2 * pps + 1:]
    step = pl.program_id(1)
    last = pl.num_programs(1) - 1
    heads = [slice(h * V_DIM, (h + 1) * V_DIM) for h in range(N_HEADS)]
    page_rows = ck[0].shape[0]

    @pl.when(step == 0)
    def _():
        m_sc[...] = jnp.full(m_sc.shape, -jnp.inf, F32)
        l_sc[...] = jnp.zeros(l_sc.shape, F32)
        acc_sc[...] = jnp.zeros(acc_sc.shape, F32)

    qn = q_ref[0]
    q_rows = jnp.concatenate([_split_branches(qn[:, cols]) for cols in heads], axis=0)
    bias = bias_ref[jnp.where(step == last, 1, 0)]
    s = jnp.concatenate([_dot_nt(q_rows, ck[pg][...].astype(BF16)) for pg in range(pps)], axis=1) + bias

    def past_values(p):
        acc = _dot(p[:, 0:page_rows], cv[0][...].astype(BF16))
        for pg in range(1, pps):
            acc = acc + _dot(p[:, pg * page_rows:(pg + 1) * page_rows], cv[pg][...].astype(BF16))
        return acc

    _online_update(s, past_values, m_sc, l_sc, acc_sc)

    @pl.when(step == last)
    def _():
        sn = _dot_nt(q_rows, kn_ref[0].astype(BF16)) + bn_ref[...]
        _online_update(sn, lambda p: _dot(p, vn_ref[0].astype(BF16)), m_sc, l_sc, acc_sc)
        lam = lam_ref[0]
        for h, cols in enumerate(heads):
            rows = slice(h * 2 * n_new, (h + 1) * 2 * n_new)
            o = _diff_head_out(acc_sc[rows, :], l_sc[rows, :], n_new, lam, g_ref[...], lam_init)
            o_ref[0, :, cols] = o.astype(BF16)


def _page_index(n, s, pt_ref, *, first_page, pg, pps, n_pages):
    return (first_page + pt_ref[n * n_pages + s * pps + pg], 0)


def _paged_attention(page_table, lam, q, k_new, v_new, bias_past, bias_new, subln_g, cache_k, cache_v, *,
                     layer, lam_init):
    n_seq, n_pages = page_table.shape
    n_new, aw = q.shape[1], q.shape[2]
    n_phys, page = cache_k.shape[1], cache_k.shape[2]
    pps = math.gcd(PAGES_PER_STEP, n_pages)
    n_steps = n_pages // pps
    n_rows = N_HEADS * 2 * n_new
    ck = cache_k.reshape(-1, V_DIM)
    cv = cache_v.reshape(-1, V_DIM)
    page_specs = [pl.BlockSpec((page * N_HEADS, V_DIM),
                               functools.partial(_page_index, first_page=layer * n_phys, pg=pg, pps=pps,
                                                 n_pages=n_pages))
                  for pg in range(pps)]
    per_seq = lambda n, s, pt: (n, 0, 0)
    return pl.pallas_call(
        functools.partial(_paged_attn_kernel, pps=pps, n_new=n_new, lam_init=lam_init),
        grid_spec=pltpu.PrefetchScalarGridSpec(
            num_scalar_prefetch=1,
            grid=(n_seq, n_steps),
            in_specs=[pl.BlockSpec(memory_space=pltpu.SMEM),
                      pl.BlockSpec((1, n_new, aw), per_seq),
                      pl.BlockSpec((1,) + k_new.shape[1:], per_seq),
                      pl.BlockSpec((1,) + v_new.shape[1:], per_seq),
                      pl.BlockSpec(bias_past.shape, lambda n, s, pt: (0, 0, 0)),
                      pl.BlockSpec(bias_new.shape, lambda n, s, pt: (0, 0)),
                      pl.BlockSpec((1, V_DIM), lambda n, s, pt: (0, 0))] + page_specs + page_specs,
            out_specs=pl.BlockSpec((1, n_new, aw), per_seq),
            scratch_shapes=[pltpu.VMEM((n_rows, LANES), F32), pltpu.VMEM((n_rows, LANES), F32),
                            pltpu.VMEM((n_rows, V_DIM), F32)]),
        out_shape=jax.ShapeDtypeStruct((n_seq, n_new, aw), BF16),
        compiler_params=_params("arbitrary", "arbitrary"),
        name="paged_attention",
    )(page_table.reshape(-1), lam, q, k_new, v_new, bias_past, bias_new, subln_g, *([ck] * pps), *([cv] * pps))


def _rel_bias_lookup(rel_bias, dist):
    n = jnp.maximum(dist, 0)
    max_exact = N_BUCKETS // 2
    nf = jnp.maximum(n, 1).astype(F32)
    large = max_exact + (jnp.log(nf / max_exact) / math.log(MAX_DISTANCE / max_exact)
                         * (N_BUCKETS - max_exact)).astype(jnp.int32)
    large = jnp.minimum(large, N_BUCKETS - 1)
    bucket = jnp.where(n < max_exact, n, large)
    onehot = (bucket.reshape(1, -1) == jnp.arange(N_BUCKETS)[:, None]).astype(F32)
    table = jnp.dot(rel_bias.astype(F32).T, onehot, precision=lax.Precision.HIGHEST) * LOG2E
    return table.reshape((rel_bias.shape[1],) + dist.shape)


def _prompt_bias_tables(rel_bias, tq):
    i = jnp.arange(tq)[:, None]
    j = jnp.arange(tq)[None, :]
    diag = jnp.where((j <= i)[None], _rel_bias_lookup(rel_bias, i - j), -jnp.inf)
    sub = _rel_bias_lookup(rel_bias, tq + i - j)
    both = lambda b: jnp.concatenate([b, b], axis=1)
    return both(diag), both(sub), rel_bias[N_BUCKETS - 1].astype(F32) * LOG2E


def _head_interleaved_bias(rel_bias, base, n_new, n_keys, causal):
    lane = jnp.arange(n_keys * N_HEADS)
    key, lane_head = lane // N_HEADS, lane % N_HEADS
    t = jnp.arange(n_new)[:, None]
    table = _rel_bias_lookup(rel_bias, base + t - key[None, :])
    keep = lane_head[None, None, :] == jnp.arange(N_HEADS)[:, None, None]
    if causal:
        keep = keep & (key[None, :] <= t)[None]
    table = jnp.where(keep, table, -jnp.inf)
    return jnp.concatenate([table, table], axis=1).reshape(N_HEADS * 2 * n_new, n_keys * N_HEADS)


def _sample_bias_tables(rel_bias, past_len, n_new, step_keys):
    assert step_keys >= MAX_DISTANCE
    far = _head_interleaved_bias(rel_bias, past_len, n_new, step_keys, causal=False)
    near = _head_interleaved_bias(rel_bias, step_keys, n_new, step_keys, causal=False)
    new = _head_interleaved_bias(rel_bias, 0, n_new, LANES // N_HEADS, causal=True)
    return jnp.stack([far, near]), new


def _routing_plan(route_p, cnt_p, route_s, cnt_s, n_rows):
    tm = EXPERT_TILE
    bucket_p = route_p[..., 0].reshape(-1).astype(jnp.int32)
    rank_p = route_p[..., 1].reshape(-1).astype(jnp.int32)
    bucket_s = route_s[..., 0].reshape(-1).astype(jnp.int32)
    rank_s = route_s[..., 1].reshape(-1).astype(jnp.int32)
    cp = cnt_p[0, :N_PAIR_BUCKETS].astype(jnp.int32)
    cs = cnt_s[0, :N_PAIR_BUCKETS].astype(jnp.int32)
    tiles = (cp + cs + tm - 1) // tm
    tile_end = jnp.cumsum(tiles)
    off = (tile_end - tiles) * tm
    lookup = lambda table, idx: jnp.sum(
        jnp.where(idx[:, None] == jnp.arange(table.shape[0])[None, :], table[None, :], 0), axis=1)
    pos_p = lookup(off, bucket_p) + rank_p
    pos_s = lookup(off + cp, bucket_s) + rank_s
    n_tiles = n_rows // tm
    n_used = tile_end[-1]
    ti = jnp.minimum(jnp.arange(n_tiles), n_used - 1)
    tile_bucket = jnp.sum((ti[:, None] >= tile_end[None, :]).astype(jnp.int32), axis=1)
    pair = np.array(EXPERT_PAIRS, np.int32)
    base = np.arange(N_PAIR_BUCKETS) // len(EXPERT_PAIRS) * EXP_PER_GROUP
    tile_ea = lookup(jnp.asarray(base + pair[np.arange(N_PAIR_BUCKETS) % len(EXPERT_PAIRS), 0]), tile_bucket)
    tile_eb = lookup(jnp.asarray(base + pair[np.arange(N_PAIR_BUCKETS) % len(EXPERT_PAIRS), 1]), tile_bucket)
    tile_valid = (jnp.arange(n_tiles) < n_used).astype(jnp.int32)
    return pos_p, pos_s, tile_ea.astype(jnp.int32), tile_eb.astype(jnp.int32), tile_valid


def kernel(x_prompt, x_sample, c_prompt, c_sample, cache_k, cache_v, state_pool, page_table, rel_bias, w_ada, b_ada, w_in, lambda_q1, lambda_k1, lambda_q2, lambda_k2, subln_g, w_pool, pool_scale, w_o, ln1_g, ln1_b, w_router_group, b_router_group, w_router_expert, b_router_expert, w_gate, w_up, w_down, ln2_g, ln2_b):
    depth = w_in.shape[0]
    nb, seq, d = x_prompt.shape
    n_seq, n_new, _ = x_sample.shape
    n_pages = page_table.shape[1]
    page = cache_k.shape[2]
    past_len = n_pages * page
    aw = N_HEADS * V_DIM
    pw = pool_scale.shape[1]
    alpha = (2 * depth) ** 0.25
    tq = min(ATTN_TILE, seq)
    assert tq >= MAX_DISTANCE and seq % tq == 0 and seq % min(SEQ_TILE, seq) == 0
    assert past_len >= POOL_HIST and n_new <= 8
    n_tok_p = nb * seq
    n_tok_s = n_seq * n_new
    assert n_tok_p % min(SEQ_TILE, n_tok_p) == 0
    n_rows = ((n_tok_p + n_tok_s) // EXPERT_TILE + N_PAIR_BUCKETS) * EXPERT_TILE
    pps = math.gcd(PAGES_PER_STEP, n_pages)
    n_steps = n_pages // pps

    bias_diag, bias_sub, c_far = _prompt_bias_tables(rel_bias, tq)
    bias_past, bias_new = _sample_bias_tables(rel_bias, past_len, n_new, pps * page)
    c_all = jnp.concatenate([c_prompt, c_sample], axis=0)
    xs_tm = jnp.transpose(x_sample, (1, 0, 2)).reshape(n_tok_s, d)

    xp, xs_cur = x_prompt, xs_tm
    kp_l, vp_l, pp_l, ks_l, vs_l, ps_l = [], [], [], [], [], []
    for l in range(depth):
        lam_init = 0.8 - 0.6 * math.exp(-0.3 * l)
        lam = (jnp.exp(jnp.sum(lambda_q1[l].astype(F32) * lambda_k1[l].astype(F32)))
               - jnp.exp(jnp.sum(lambda_q2[l].astype(F32) * lambda_k2[l].astype(F32))) + lam_init).reshape(1)
        win_b = w_in[l].astype(BF16)
        wpool_b = w_pool[l].astype(BF16)
        wo_b = w_o[l].astype(BF16)
        wr = jnp.concatenate([w_router_group[l], jnp.transpose(w_router_expert[l], (1, 0, 2)).reshape(d, N_EXPERTS)], axis=1)
        wr_b = jnp.pad(wr, ((0, 0), (0, LANES - wr.shape[1]))).astype(BF16)
        br = jnp.pad(jnp.concatenate([b_router_group[l], b_router_expert[l].reshape(-1)]).astype(F32),
                     (0, LANES - N_EXP_GROUPS - N_EXPERTS)).reshape(1, LANES)
        wg_b = w_gate[l].reshape(N_EXPERTS, d, -1).astype(BF16)
        wu_b = w_up[l].reshape(N_EXPERTS, d, -1).astype(BF16)
        wd_b = w_down[l].reshape(N_EXPERTS, -1, d).astype(BF16)
        pscale = pool_scale[l].reshape(1, pw)
        g_sub = subln_g[l].reshape(1, V_DIM)
        ln1g, ln1b = ln1_g[l].reshape(1, d), ln1_b[l].reshape(1, d)
        ln2g, ln2b = ln2_g[l].reshape(1, d), ln2_b[l].reshape(1, d)

        m_all = _modulation(c_all, w_ada[l], b_ada[l]).reshape(nb + n_seq, 6, d)
        mod_p = m_all[:nb]
        mod_s = jnp.tile(jnp.transpose(m_all[nb:], (1, 0, 2)), (1, n_new, 1))

        q_p, k_p, v_p, kb_p, vb_p, pm_p, ph_p = _prompt_inproj(xp, mod_p, win_b, wpool_b, pscale)
        o_p = _prompt_attention(q_p, kb_p, vb_p, bias_diag, bias_sub, c_far, lam, g_sub, lam_init)
        y1_p, tokx_p, route_p, cnt_p = _outproj_router(o_p, pm_p, xp, mod_p, wo_b, ln1g, ln1b, wr_b, br,
                                                       alpha=alpha, per_row=False)

        hist = state_pool[l]
        q_s, k_s, v_s, p_s, pm_s = _sample_inproj(xs_cur, mod_s[0], mod_s[1], win_b, jnp.transpose(hist, (1, 0, 2)),
                                                  wpool_b, pscale, n_seq=n_seq, n_new=n_new, past_len=past_len)
        seq_major = lambda a: jnp.transpose(a.reshape(n_new, n_seq, -1), (1, 0, 2))
        new_rows = lambda a: jnp.pad(seq_major(a).reshape(n_seq, n_new * N_HEADS, V_DIM),
                                     ((0, 0), (0, LANES - n_new * N_HEADS), (0, 0)))
        o_s = _paged_attention(page_table, lam, seq_major(q_s), new_rows(k_s), new_rows(v_s), bias_past, bias_new, g_sub,
                               cache_k, cache_v, layer=l, lam_init=lam_init)
        o_s_tm = jnp.transpose(o_s, (1, 0, 2)).reshape(1, n_tok_s, aw)
        y1_s, tokx_s, route_s, cnt_s = _outproj_router(o_s_tm, pm_s[None], xs_cur[None], mod_s, wo_b, ln1g, ln1b,
                                                       wr_b, br, alpha=alpha, per_row=True)

        pos_p, pos_s, tile_ea, tile_eb, tile_valid = _routing_plan(route_p, cnt_p, route_s, cnt_s, n_rows)
        xs_sorted = jnp.zeros((n_rows, d + ROUTE_EXTRA), F32)
        xs_sorted = _dispatch(pos_p, tokx_p.reshape(n_tok_p, -1), xs_sorted)
        xs_sorted = _dispatch(pos_s, tokx_s.reshape(n_tok_s, -1), xs_sorted)
        o_sorted = _experts(tile_ea, tile_eb, tile_valid, xs_sorted, wg_b, wu_b, wd_b)
        xp = _combine(pos_p, y1_p, mod_p, ln2g, ln2b, o_sorted, alpha=alpha, per_row=False)
        xs_cur = _combine(pos_s, y1_s, mod_s, ln2g, ln2b, o_sorted, alpha=alpha, per_row=True)[0]

        kp_l.append(k_p.reshape(nb, seq, N_HEADS, V_DIM))
        vp_l.append(v_p.reshape(nb, seq, N_HEADS, V_DIM))
        pp_l.append(ph_p)
        ks_l.append(seq_major(k_s).reshape(n_seq, n_new, N_HEADS, V_DIM))
        vs_l.append(seq_major(v_s).reshape(n_seq, n_new, N_HEADS, V_DIM))
        ps_l.append(jnp.concatenate([hist, seq_major(p_s)], axis=1)[:, -POOL_HIST:])

    y_sample = jnp.transpose(xs_cur.reshape(n_new, n_seq, d), (1, 0, 2))
    return (xp, y_sample, jnp.stack(kp_l), jnp.stack(vp_l), jnp.stack(pp_l),
            jnp.stack(ks_l), jnp.stack(vs_l), jnp.stack(ps_l))
```

```python
import functools
import math

import numpy as np
import jax
import jax.numpy as jnp
from jax import lax
from jax.experimental import pallas as pl
from jax.experimental.pallas import tpu as pltpu

F32 = jnp.float32
BF16 = jnp.bfloat16

N_HEADS = 4
HEAD_DIM = 64
V_DIM = 2 * HEAD_DIM
POOL_WINDOWS = (2, 4, 8, 16)
POOL_GROUP = 128
POOL_HIST = max(POOL_WINDOWS) - 1
N_BUCKETS = 32
MAX_DISTANCE = 128
N_EXP_GROUPS = 4
EXP_PER_GROUP = 4
N_EXPERTS = N_EXP_GROUPS * EXP_PER_GROUP
EXPERT_PAIRS = ((0, 1), (0, 2), (0, 3), (1, 2), (1, 3), (2, 3))
N_PAIR_BUCKETS = N_EXP_GROUPS * len(EXPERT_PAIRS)
LN_EPS = 1e-5
RMS_EPS = 1e-5

LANES = 128
POOL_HALO = 16
VMEM_LIMIT_BYTES = 48 * 1024 * 1024
SEQ_TILE = 512
ATTN_TILE = 256
EXPERT_TILE = 256
PAGES_PER_STEP = 16
PAGE_RING = 3
ROUTE_EXTRA = LANES
LOG2E = math.log2(math.e)
Q_SCALE = HEAD_DIM ** -0.5 * LOG2E


def _params(*sem):
    return pltpu.CompilerParams(dimension_semantics=sem, vmem_limit_bytes=VMEM_LIMIT_BYTES)


def _dot(a, b):
    return jnp.dot(a, b, preferred_element_type=F32)


def _dot_nt(a, b):
    return lax.dot_general(a, b, (((1,), (1,)), ((), ())), preferred_element_type=F32)


def _sigmoid(x):
    return 1.0 / (1.0 + jnp.exp(-x))


def _mod_kernel(c_ref, w_ref, b_ref, o_ref):
    c = c_ref[...]
    s = c * _sigmoid(c)
    s_hi = s.astype(BF16)
    s_lo = (s - s_hi.astype(F32)).astype(BF16)
    w = w_ref[...]
    w_hi = w.astype(BF16)
    w_lo = (w - w_hi.astype(F32)).astype(BF16)
    o_ref[...] = _dot(s_hi, w_hi) + _dot(s_lo, w_hi) + _dot(s_hi, w_lo) + b_ref[...]


def _modulation(c_all, w_ada, b_ada):
    n, d = c_all.shape
    e = w_ada.shape[1]
    bn = 1024
    return pl.pallas_call(
        _mod_kernel,
        grid=(e // bn,),
        in_specs=[pl.BlockSpec((n, d), lambda j: (0, 0)),
                  pl.BlockSpec((d, bn), lambda j: (0, j)),
                  pl.BlockSpec((1, bn), lambda j: (0, j))],
        out_specs=pl.BlockSpec((n, bn), lambda j: (0, j)),
        out_shape=jax.ShapeDtypeStruct((n, e), F32),
        compiler_params=_params("arbitrary"),
        name="modulation",
    )(c_all, w_ada, b_ada.reshape(1, e))


def _pool_mixed(ext, g, w, inv_cnt, rows):
    eg = ext[:, g * POOL_GROUP:(g + 1) * POOL_GROUP]
    s = eg
    step = 1
    while step < w:
        s = s + pltpu.roll(s, step, 0)
        step *= 2
    return s[POOL_HALO:POOL_HALO + rows] * inv_cnt - eg[POOL_HALO:POOL_HALO + rows]


def _inproj_kernel(x_ref, mod_ref, win_ref, wpool_ref, pscale_ref,
                   q_ref, k_ref, v_ref, kb_ref, vb_ref, pm_ref, ph_ref, ext_ref, *, ts, aw):
    i = pl.program_id(1)

    @pl.when(i == 0)
    def _():
        ext_ref[0:POOL_HALO, :] = jnp.zeros((POOL_HALO, ext_ref.shape[1]), F32)

    @pl.when(i > 0)
    def _():
        ext_ref[0:POOL_HALO, :] = ext_ref[ts:ts + POOL_HALO, :]

    shift = mod_ref[0, 0:1, :]
    scale = mod_ref[0, 1:2, :]
    u = (x_ref[0] * (1.0 + scale) + shift).astype(BF16)
    acc = _dot(u, win_ref[...])
    q_ref[0] = (acc[:, 0:aw] * Q_SCALE).astype(BF16)
    k = acc[:, aw:2 * aw]
    v = acc[:, 2 * aw:3 * aw]
    for h in range(N_HEADS):
        k_ref[0, pl.ds(h, ts, stride=N_HEADS), :] = k[:, h * V_DIM:(h + 1) * V_DIM]
        v_ref[0, pl.ds(h, ts, stride=N_HEADS), :] = v[:, h * V_DIM:(h + 1) * V_DIM]
    kb_ref[0] = k.astype(BF16)
    vb_ref[0] = v.astype(BF16)
    p = acc[:, 3 * aw:]
    ext_ref[POOL_HALO:POOL_HALO + ts, :] = p
    ext = ext_ref[...]
    pos = i * ts + lax.broadcasted_iota(jnp.int32, (ts, 1), 0)
    for g, w in enumerate(POOL_WINDOWS):
        inv_cnt = 1.0 / jnp.minimum(w, pos + 1).astype(F32)
        mixed = _pool_mixed(ext, g, w, inv_cnt, ts)
        cols = slice(g * POOL_GROUP, (g + 1) * POOL_GROUP)
        pm_ref[0, :, cols] = (_dot(mixed.astype(BF16), wpool_ref[g]) * pscale_ref[:, cols]).astype(BF16)

    @pl.when(i == pl.num_programs(1) - 1)
    def _():
        ph_ref[0] = p[ts - POOL_HIST:ts, :]


def _prompt_inproj(x, mod, win_b, wpool_b, pscale):
    nb, s, d = x.shape
    ew = win_b.shape[1]
    aw = (ew - 4 * POOL_GROUP) // 3
    pw = ew - 3 * aw
    ts = min(SEQ_TILE, s)
    nt = s // ts
    tile = lambda b, i: (b, i, 0)
    const2 = lambda b, i: (0, 0)
    return pl.pallas_call(
        functools.partial(_inproj_kernel, ts=ts, aw=aw),
        grid=(nb, nt),
        in_specs=[pl.BlockSpec((1, ts, d), tile),
                  pl.BlockSpec((1, 6, d), lambda b, i: (b, 0, 0)),
                  pl.BlockSpec((d, ew), const2),
                  pl.BlockSpec(wpool_b.shape, lambda b, i: (0, 0, 0)),
                  pl.BlockSpec((1, pw), const2)],
        out_specs=[pl.BlockSpec((1, ts, aw), tile),
                   pl.BlockSpec((1, ts * N_HEADS, V_DIM), tile),
                   pl.BlockSpec((1, ts * N_HEADS, V_DIM), tile),
                   pl.BlockSpec((1, ts, aw), tile), pl.BlockSpec((1, ts, aw), tile),
                   pl.BlockSpec((1, ts, pw), tile),
                   pl.BlockSpec((1, POOL_HIST, pw), lambda b, i: (b, 0, 0))],
        out_shape=[jax.ShapeDtypeStruct((nb, s, aw), BF16),
                   jax.ShapeDtypeStruct((nb, s * N_HEADS, V_DIM), F32),
                   jax.ShapeDtypeStruct((nb, s * N_HEADS, V_DIM), F32),
                   jax.ShapeDtypeStruct((nb, s, aw), BF16),
                   jax.ShapeDtypeStruct((nb, s, aw), BF16),
                   jax.ShapeDtypeStruct((nb, s, pw), BF16),
                   jax.ShapeDtypeStruct((nb, POOL_HIST, pw), F32)],
        scratch_shapes=[pltpu.VMEM((ts + POOL_HALO, pw), F32)],
        compiler_params=_params("arbitrary", "arbitrary"),
        name="prompt_inproj",
    )(x, mod, win_b, wpool_b, pscale)


def _split_branches(qh):
    lane = lax.broadcasted_iota(jnp.int32, qh.shape, 1)
    zero = jnp.zeros_like(qh)
    return jnp.concatenate([jnp.where(lane < HEAD_DIM, qh, zero), jnp.where(lane >= HEAD_DIM, qh, zero)], axis=0)


def _lane_tile(a, width):
    return jnp.concatenate([a] * (width // LANES), axis=1) if width > LANES else a


def _online_update(s, values, m_ref, l_ref, acc_ref, first=False):
    rows, width = s.shape
    m_new = jnp.broadcast_to(jnp.max(s, axis=1, keepdims=True), (rows, LANES))
    if not first:
        m_old = m_ref[...]
        m_new = jnp.maximum(m_old, m_new)
    p = jnp.exp2(s - _lane_tile(m_new, width))
    pv = values(p.astype(BF16))
    l_new = jnp.broadcast_to(jnp.sum(p, axis=1, keepdims=True), (rows, LANES))
    if first:
        acc_ref[...] = pv
        l_ref[...] = l_new
    else:
        alpha = jnp.exp2(m_old - m_new)
        acc_ref[...] = _lane_tile(alpha, pv.shape[1]) * acc_ref[...] + pv
        l_ref[...] = alpha * l_ref[...] + l_new
    m_ref[...] = m_new


def _diff_head_out(acc, l, t, lam, g, lam_init):
    o = acc[0:t] / l[0:t] - lam * (acc[t:2 * t] / l[t:2 * t])
    o = o * lax.rsqrt(jnp.mean(o * o, axis=-1, keepdims=True) + RMS_EPS) * g
    return o * (1.0 - lam_init)


def _attn_kernel(cfar_ref, lam_ref, q_ref, k_ref, v_ref, bd_ref, be_ref, g_ref, o_ref,
                 qs_sc, m_sc, l_sc, acc_sc, *, tq, lam_init):
    qi = pl.program_id(1)
    heads = [slice(h * V_DIM, (h + 1) * V_DIM) for h in range(N_HEADS)]
    for h, cols in enumerate(heads):
        qs_sc[h] = _split_branches(q_ref[0, :, cols])

    def step(j, bias, first):
        rows = pl.ds(pl.multiple_of(j * tq, tq), tq)
        for h, cols in enumerate(heads):
            s = _dot_nt(qs_sc[h], k_ref[0, rows, cols]) + bias(h)
            _online_update(s, lambda p, cols=cols: _dot(p, v_ref[0, rows, cols]),
                           m_sc.at[h], l_sc.at[h], acc_sc.at[h], first=first)

    step(qi, lambda h: bd_ref[h], True)

    @pl.when(qi >= 1)
    def _():
        step(qi - 1, lambda h: be_ref[h], False)

    def far(j, carry):
        step(j, lambda h: cfar_ref[h], False)
        return carry

    lax.fori_loop(0, jnp.maximum(qi - 1, 0), far, 0)
    lam = lam_ref[0]
    for h, cols in enumerate(heads):
        o = _diff_head_out(acc_sc[h], l_sc[h], tq, lam, g_ref[...], lam_init)
        o_ref[0, :, cols] = o.astype(BF16)


def _prompt_attention(q, kb, vb, bias_diag, bias_sub, c_far, lam, subln_g, lam_init):
    nb, s, aw = q.shape
    tq = bias_diag.shape[2]
    smem = pl.BlockSpec(memory_space=pltpu.SMEM)
    const3 = lambda b, i: (0, 0, 0)
    return pl.pallas_call(
        functools.partial(_attn_kernel, tq=tq, lam_init=lam_init),
        grid=(nb, s // tq),
        in_specs=[smem, smem,
                  pl.BlockSpec((1, tq, aw), lambda b, i: (b, i, 0)),
                  pl.BlockSpec((1, s, aw), lambda b, i: (b, 0, 0)),
                  pl.BlockSpec((1, s, aw), lambda b, i: (b, 0, 0)),
                  pl.BlockSpec(bias_diag.shape, const3),
                  pl.BlockSpec(bias_sub.shape, const3),
                  pl.BlockSpec((1, V_DIM), lambda b, i: (0, 0))],
        out_specs=pl.BlockSpec((1, tq, aw), lambda b, i: (b, i, 0)),
        out_shape=jax.ShapeDtypeStruct((nb, s, aw), BF16),
        scratch_shapes=[pltpu.VMEM((N_HEADS, 2 * tq, V_DIM), BF16),
                        pltpu.VMEM((N_HEADS, 2 * tq, LANES), F32), pltpu.VMEM((N_HEADS, 2 * tq, LANES), F32),
                        pltpu.VMEM((N_HEADS, 2 * tq, V_DIM), F32)],
        compiler_params=_params("arbitrary", "arbitrary"),
        name="prompt_attention",
    )(c_far, lam, q, kb, vb, bias_diag, bias_sub, subln_g)


def _layer_norm(z, g, b):
    mu = jnp.mean(z, axis=-1, keepdims=True)
    zc = z - mu
    var = jnp.mean(zc * zc, axis=-1, keepdims=True)
    return zc * lax.rsqrt(var + LN_EPS) * g + b


def _mod_get(mod_ref, j, per_row):
    return mod_ref[j] if per_row else mod_ref[0, j:j + 1, :]


def _outproj_router_kernel(o_ref, pm_ref, x_ref, mod_ref, wo_ref, lng_ref, lnb_ref, wr_ref, br_ref,
                           y_ref, tokx_ref, route_ref, cnt_ref, carry_sc, *, alpha, per_row):
    t = x_ref.shape[1]
    d = x_ref.shape[2]
    aw = o_ref.shape[2]

    @pl.when((pl.program_id(0) == 0) & (pl.program_id(1) == 0))
    def _():
        carry_sc[...] = jnp.zeros(carry_sc.shape, F32)

    h = _dot(o_ref[0], wo_ref[0:aw, :]) + _dot(pm_ref[0], wo_ref[aw:, :])
    gate1 = _mod_get(mod_ref, 2, per_row)
    y1 = _layer_norm(alpha * x_ref[0] + gate1 * h, lng_ref[...], lnb_ref[...])
    y_ref[0] = y1
    tok = y1 * (1.0 + _mod_get(mod_ref, 4, per_row)) + _mod_get(mod_ref, 3, per_row)
    tokx_ref[0, :, 0:d] = tok

    logits = _dot(tok.astype(BF16), wr_ref[...]) + br_ref[...]
    lane = lax.broadcasted_iota(jnp.int32, logits.shape, 1)
    lane_f = lane.astype(F32)
    neg = jnp.full_like(logits, -jnp.inf)

    def first_argmax(vals, vmax):
        return jnp.min(jnp.where(vals == vmax, lane_f, float(LANES)), axis=1, keepdims=True).astype(jnp.int32)

    gl = jnp.where(lane < N_EXP_GROUPS, logits, neg)
    gmax = jnp.max(gl, axis=1, keepdims=True)
    gidx = first_argmax(gl, gmax)
    g_w = 1.0 / jnp.sum(jnp.exp(gl - gmax), axis=1, keepdims=True)
    lo_lane = N_EXP_GROUPS + EXP_PER_GROUP * gidx
    el = jnp.where((lane >= lo_lane) & (lane < lo_lane + EXP_PER_GROUP), logits, neg)
    v1 = jnp.max(el, axis=1, keepdims=True)
    i1 = first_argmax(el, v1)
    el2 = jnp.where(lane == i1, neg, el)
    v2 = jnp.max(el2, axis=1, keepdims=True)
    i2 = first_argmax(el2, v2)
    e21 = jnp.exp(v2 - v1)
    w1 = g_w / (1.0 + e21)
    w2 = g_w * e21 / (1.0 + e21)
    first_lo = i1 < i2
    cw_lo = jnp.where(first_lo, w1, w2)
    cw_hi = jnp.where(first_lo, w2, w1)
    a = jnp.minimum(i1, i2) - lo_lane
    b = jnp.maximum(i1, i2) - lo_lane
    pair_base = jnp.where(a == 0, 0, jnp.where(a == 1, 3, 5))
    bucket = gidx * len(EXPERT_PAIRS) + pair_base + (b - a - 1)

    xlane = lax.broadcasted_iota(jnp.int32, (t, ROUTE_EXTRA), 1)
    tokx_ref[0, :, d:] = jnp.where(xlane == 0, cw_lo, jnp.where(xlane == 1, cw_hi, 0.0))

    onehot = lane == bucket
    row = lax.broadcasted_iota(jnp.int32, (t, t), 0)
    col = lax.broadcasted_iota(jnp.int32, (t, t), 1)
    ltri = jnp.where(col < row, 1.0, 0.0).astype(BF16)
    prefix = _dot(ltri, jnp.where(onehot, 1.0, 0.0).astype(BF16)) + carry_sc[...]
    rank = jnp.sum(jnp.where(onehot, prefix, 0.0), axis=1, keepdims=True)
    carry_sc[...] = carry_sc[...] + jnp.sum(jnp.where(onehot, 1.0, 0.0), axis=0, keepdims=True)
    cnt_ref[...] = carry_sc[...]
    route_ref[0] = jnp.where(lane == 0, bucket.astype(F32), jnp.where(lane == 1, rank, 0.0))


def _outproj_router(o, pm, x, mod, wo_b, ln_g, ln_b, wr_b, br, *, alpha, per_row):
    nb, s, d = x.shape
    aw = o.shape[2]
    pw = pm.shape[2]
    t = min(SEQ_TILE, s)
    tile = lambda b, i: (b, i, 0)
    const2 = lambda b, i: (0, 0)
    mod_spec = (pl.BlockSpec(mod.shape, lambda b, i: (0, 0, 0)) if per_row
                else pl.BlockSpec((1, 6, d), lambda b, i: (b, 0, 0)))
    return pl.pallas_call(
        functools.partial(_outproj_router_kernel, alpha=alpha, per_row=per_row),
        grid=(nb, s // t),
        in_specs=[pl.BlockSpec((1, t, aw), tile), pl.BlockSpec((1, t, pw), tile), pl.BlockSpec((1, t, d), tile),
                  mod_spec,
                  pl.BlockSpec(wo_b.shape, const2), pl.BlockSpec((1, d), const2), pl.BlockSpec((1, d), const2),
                  pl.BlockSpec(wr_b.shape, const2), pl.BlockSpec((1, LANES), const2)],
        out_specs=[pl.BlockSpec((1, t, d), tile), pl.BlockSpec((1, t, d + ROUTE_EXTRA), tile),
                   pl.BlockSpec((1, t, LANES), tile), pl.BlockSpec((1, LANES), const2)],
        out_shape=[jax.ShapeDtypeStruct((nb, s, d), F32), jax.ShapeDtypeStruct((nb, s, d + ROUTE_EXTRA), F32),
                   jax.ShapeDtypeStruct((nb, s, LANES), F32), jax.ShapeDtypeStruct((1, LANES), F32)],
        scratch_shapes=[pltpu.VMEM((1, LANES), F32)],
        compiler_params=_params("arbitrary", "arbitrary"),
        name="outproj_router",
    )(o, pm, x, mod, wo_b, ln_g, ln_b, wr_b, br)


def _issue_row_copies(src_ref, dst_ref, sem, src_row, dst_row, n):
    for r in range(n):
        pltpu.make_async_copy(src_ref.at[pl.ds(src_row(r), 1)], dst_ref.at[pl.ds(dst_row(r), 1)],
                              sem).start(priority=r % 2)


def _wait_row_copies(src_ref, dst_ref, sem, n):
    pltpu.make_async_copy(src_ref.at[pl.ds(0, n)], dst_ref.at[pl.ds(0, n)], sem).wait()


def _dispatch_kernel(pos_ref, tok_ref, xs_in_ref, xs_ref, sem, *, t):
    del xs_in_ref
    base = pl.program_id(0) * t
    _issue_row_copies(tok_ref, xs_ref, sem, lambda r: r, lambda r: pos_ref[base + r], t)
    _wait_row_copies(tok_ref, xs_ref, sem, t)


def _dispatch(pos, tokx, xs):
    n, w = tokx.shape
    t = min(SEQ_TILE, n)
    return pl.pallas_call(
        functools.partial(_dispatch_kernel, t=t),
        grid_spec=pltpu.PrefetchScalarGridSpec(
            num_scalar_prefetch=1,
            grid=(n // t,),
            in_specs=[pl.BlockSpec((t, w), lambda i, pos: (i, 0)), pl.BlockSpec(memory_space=pl.ANY)],
            out_specs=pl.BlockSpec(memory_space=pl.ANY),
            scratch_shapes=[pltpu.SemaphoreType.DMA(())]),
        out_shape=jax.ShapeDtypeStruct(xs.shape, xs.dtype),
        input_output_aliases={2: 0},
        compiler_params=_params("arbitrary"),
        name="dispatch_rows",
    )(pos, tokx, xs)


def _expert_kernel(ea_ref, eb_ref, valid_ref, xs_ref, wga_ref, wua_ref, wda_ref, wgb_ref, wub_ref, wdb_ref,
                   o_ref, *, d):
    del ea_ref, eb_ref
    ti = pl.program_id(0)

    @pl.when(valid_ref[ti] == 1)
    def _():
        xs = xs_ref[...]
        x = xs[:, 0:d].astype(BF16)

        def ffn(wg_ref, wu_ref, wd_ref, cw):
            g = _dot(x, wg_ref[0].astype(BF16))
            u = _dot(x, wu_ref[0].astype(BF16))
            hid = (g * _sigmoid(g)) * u * cw
            return _dot(hid.astype(BF16), wd_ref[0].astype(BF16))

        o_ref[...] = (ffn(wga_ref, wua_ref, wda_ref, xs[:, d:d + 1])
                      + ffn(wgb_ref, wub_ref, wdb_ref, xs[:, d + 1:d + 2]))

    @pl.when(valid_ref[ti] == 0)
    def _():
        o_ref[...] = jnp.zeros(o_ref.shape, F32)


def _experts(tile_ea, tile_eb, tile_valid, xs, wg_b, wu_b, wd_b):
    r, w = xs.shape
    d = w - ROUTE_EXTRA
    f = wg_b.shape[2]
    tm = EXPERT_TILE
    wa = lambda blk: pl.BlockSpec(blk, lambda i, ea, eb, va: (ea[i], 0, 0))
    wb = lambda blk: pl.BlockSpec(blk, lambda i, ea, eb, va: (eb[i], 0, 0))
    return pl.pallas_call(
        functools.partial(_expert_kernel, d=d),
        grid_spec=pltpu.PrefetchScalarGridSpec(
            num_scalar_prefetch=3,
            grid=(r // tm,),
            in_specs=[pl.BlockSpec((tm, w), lambda i, ea, eb, va: (i, 0)),
                      wa((1, d, f)), wa((1, d, f)), wa((1, f, d)),
                      wb((1, d, f)), wb((1, d, f)), wb((1, f, d))],
            out_specs=pl.BlockSpec((tm, d), lambda i, ea, eb, va: (i, 0))),
        out_shape=jax.ShapeDtypeStruct((r, d), F32),
        compiler_params=_params("arbitrary"),
        name="expert_ffn",
    )(tile_ea, tile_eb, tile_valid, xs, wg_b, wu_b, wd_b, wg_b, wu_b, wd_b)


def _combine_kernel(pos_ref, y_ref, mod_ref, lng_ref, lnb_ref, os_ref, out_ref, buf, sem, *, alpha, per_row):
    t = y_ref.shape[1]
    k = pl.program_id(0) * pl.num_programs(1) + pl.program_id(1)
    n_tiles = pl.num_programs(0) * pl.num_programs(1)

    def fetch(tile):
        slot = tile % 2
        _issue_row_copies(os_ref, buf.at[slot], sem.at[slot], lambda r: pos_ref[tile * t + r], lambda r: r, t)

    @pl.when(k == 0)
    def _():
        fetch(k)

    @pl.when(k + 1 < n_tiles)
    def _():
        fetch(k + 1)

    slot = k % 2
    _wait_row_copies(os_ref, buf.at[slot], sem.at[slot], t)
    gate2 = _mod_get(mod_ref, 5, per_row)
    out_ref[0] = _layer_norm(alpha * y_ref[0] + gate2 * buf[slot], lng_ref[...], lnb_ref[...])


def _combine(pos, y1, mod, ln_g, ln_b, o_sorted, *, alpha, per_row):
    nb, s, d = y1.shape
    t = min(SEQ_TILE, s)
    tile = lambda b, i, pos: (b, i, 0)
    const2 = lambda b, i, pos: (0, 0)
    mod_spec = (pl.BlockSpec(mod.shape, lambda b, i, pos: (0, 0, 0)) if per_row
                else pl.BlockSpec((1, 6, d), lambda b, i, pos: (b, 0, 0)))
    return pl.pallas_call(
        functools.partial(_combine_kernel, alpha=alpha, per_row=per_row),
        grid_spec=pltpu.PrefetchScalarGridSpec(
            num_scalar_prefetch=1,
            grid=(nb, s // t),
            in_specs=[pl.BlockSpec((1, t, d), tile), mod_spec,
                      pl.BlockSpec((1, d), const2), pl.BlockSpec((1, d), const2),
                      pl.BlockSpec(memory_space=pl.ANY)],
            out_specs=pl.BlockSpec((1, t, d), tile),
            scratch_shapes=[pltpu.VMEM((2, t, d), F32), pltpu.SemaphoreType.DMA((2,))]),
        out_shape=jax.ShapeDtypeStruct((nb, s, d), F32),
        compiler_params=_params("arbitrary", "arbitrary"),
        name="combine_rows",
    )(pos, y1, mod, ln_g, ln_b, o_sorted)


def _sample_inproj_kernel(x_ref, shift_ref, scale_ref, win_ref, hist_ref, wpool_ref, pscale_ref,
                          q_ref, k_ref, v_ref, p_ref, pm_ref, *, aw, n_seq, n_new, past_len):
    u = (x_ref[...] * (1.0 + scale_ref[...]) + shift_ref[...]).astype(BF16)
    acc = _dot(u, win_ref[...])
    q_ref[...] = (acc[:, 0:aw] * Q_SCALE).astype(BF16)
    k_ref[...] = acc[:, aw:2 * aw]
    v_ref[...] = acc[:, 2 * aw:3 * aw]
    p = acc[:, 3 * aw:]
    p_ref[...] = p
    n_hist = hist_ref.shape[0]
    rows = [hist_ref[j] for j in range(n_hist)] + [p[t * n_seq:(t + 1) * n_seq] for t in range(n_new)]
    for g, w in enumerate(POOL_WINDOWS):
        cols = slice(g * POOL_GROUP, (g + 1) * POOL_GROUP)
        mixed = []
        for t in range(n_new):
            end = n_hist + t + 1
            start = max(end - w, 0)
            win_sum = rows[start][:, cols]
            for j in range(start + 1, end):
                win_sum = win_sum + rows[j][:, cols]
            count = float(min(w, past_len + t + 1))
            mixed.append(win_sum / count - rows[n_hist + t][:, cols])
        mixed = jnp.concatenate(mixed, axis=0)
        pm_ref[:, cols] = (_dot(mixed.astype(BF16), wpool_ref[g]) * pscale_ref[:, cols]).astype(BF16)


def _sample_inproj(x, shift, scale, win_b, hist_t, wpool_b, pscale, *, n_seq, n_new, past_len):
    n, d = x.shape
    ew = win_b.shape[1]
    aw = (ew - 4 * POOL_GROUP) // 3
    pw = ew - 3 * aw
    return pl.pallas_call(
        functools.partial(_sample_inproj_kernel, aw=aw, n_seq=n_seq, n_new=n_new, past_len=past_len),
        out_shape=[jax.ShapeDtypeStruct((n, aw), BF16), jax.ShapeDtypeStruct((n, aw), F32),
                   jax.ShapeDtypeStruct((n, aw), F32), jax.ShapeDtypeStruct((n, pw), F32),
                   jax.ShapeDtypeStruct((n, pw), BF16)],
        compiler_params=pltpu.CompilerParams(vmem_limit_bytes=VMEM_LIMIT_BYTES),
        name="sample_inproj",
    )(x, shift, scale, win_b, hist_t, wpool_b, pscale)


def _paged_attn_kernel(pt_ref, lam_ref, q_ref, kn_ref, vn_ref, bias_ref, bn_ref, g_ref, ck_ref, cv_ref, o_ref,
                       kbuf, vbuf, sem, m_sc, l_sc, acc_sc, *, pps, page_rows, first_page, n_new, lam_init):
    step = pl.program_id(1)
    n_steps = pl.num_programs(1)
    last = n_steps - 1
    g = pl.program_id(0) * n_steps + step
    total = pl.num_programs(0) * n_steps
    heads = [slice(h * V_DIM, (h + 1) * V_DIM) for h in range(N_HEADS)]

    def fetch(gs):
        slot = gs % PAGE_RING
        for pg in range(pps):
            src = pl.ds(pl.multiple_of((first_page + pt_ref[gs * pps + pg]) * page_rows, page_rows), page_rows)
            dst = pl.ds(pg * page_rows, page_rows)
            pltpu.make_async_copy(ck_ref.at[src], kbuf.at[slot, dst], sem.at[slot, 0]).start()
            pltpu.make_async_copy(cv_ref.at[src], vbuf.at[slot, dst], sem.at[slot, 1]).start()

    @pl.when(g == 0)
    def _():
        for gs in range(PAGE_RING - 1):
            fetch(gs)

    @pl.when(g + PAGE_RING - 1 < total)
    def _():
        fetch(g + PAGE_RING - 1)

    @pl.when(step == 0)
    def _():
        m_sc[...] = jnp.full(m_sc.shape, -jnp.inf, F32)
        l_sc[...] = jnp.zeros(l_sc.shape, F32)
        acc_sc[...] = jnp.zeros(acc_sc.shape, F32)

    slot = g % PAGE_RING
    pltpu.make_async_copy(ck_ref.at[pl.ds(0, pps * page_rows)], kbuf.at[slot], sem.at[slot, 0]).wait()
    pltpu.make_async_copy(cv_ref.at[pl.ds(0, pps * page_rows)], vbuf.at[slot], sem.at[slot, 1]).wait()

    qn = q_ref[0]
    q_rows = jnp.concatenate([_split_branches(qn[:, cols]) for cols in heads], axis=0)
    bias = bias_ref[jnp.where(step == last, 1, 0)]
    page = lambda buf, pg: buf[slot, pg * page_rows:(pg + 1) * page_rows, :].astype(BF16)
    s = jnp.concatenate([_dot_nt(q_rows, page(kbuf, pg)) for pg in range(pps)], axis=1) + bias

    def past_values(p):
        acc = _dot(p[:, 0:page_rows], page(vbuf, 0))
        for pg in range(1, pps):
            acc = acc + _dot(p[:, pg * page_rows:(pg + 1) * page_rows], page(vbuf, pg))
        return acc

    _online_update(s, past_values, m_sc, l_sc, acc_sc)

    @pl.when(step == last)
    def _():
        sn = _dot_nt(q_rows, kn_ref[0].astype(BF16)) + bn_ref[...]
        _online_update(sn, lambda p: _dot(p, vn_ref[0].astype(BF16)), m_sc, l_sc, acc_sc)
        lam = lam_ref[0]
        for h, cols in enumerate(heads):
            rows = slice(h * 2 * n_new, (h + 1) * 2 * n_new)
            o = _diff_head_out(acc_sc[rows, :], l_sc[rows, :], n_new, lam, g_ref[...], lam_init)
            o_ref[0, :, cols] = o.astype(BF16)


def _paged_attention(page_table, lam, q, k_new, v_new, bias_past, bias_new, subln_g, cache_k, cache_v, *,
                     layer, lam_init):
    n_seq, n_pages = page_table.shape
    n_new, aw = q.shape[1], q.shape[2]
    n_phys, page = cache_k.shape[1], cache_k.shape[2]
    pps = math.gcd(PAGES_PER_STEP, n_pages)
    n_steps = n_pages // pps
    assert n_seq * n_steps >= PAGE_RING - 1
    n_rows = N_HEADS * 2 * n_new
    page_rows = page * N_HEADS
    ck = cache_k.reshape(-1, V_DIM)
    cv = cache_v.reshape(-1, V_DIM)
    per_seq = lambda n, s, pt: (n, 0, 0)
    hbm = pl.BlockSpec(memory_space=pl.ANY)
    return pl.pallas_call(
        functools.partial(_paged_attn_kernel, pps=pps, page_rows=page_rows, first_page=layer * n_phys,
                          n_new=n_new, lam_init=lam_init),
        grid_spec=pltpu.PrefetchScalarGridSpec(
            num_scalar_prefetch=1,
            grid=(n_seq, n_steps),
            in_specs=[pl.BlockSpec(memory_space=pltpu.SMEM),
                      pl.BlockSpec((1, n_new, aw), per_seq),
                      pl.BlockSpec((1,) + k_new.shape[1:], per_seq),
                      pl.BlockSpec((1,) + v_new.shape[1:], per_seq),
                      pl.BlockSpec(bias_past.shape, lambda n, s, pt: (0, 0, 0)),
                      pl.BlockSpec(bias_new.shape, lambda n, s, pt: (0, 0)),
                      pl.BlockSpec((1, V_DIM), lambda n, s, pt: (0, 0)), hbm, hbm],
            out_specs=pl.BlockSpec((1, n_new, aw), per_seq),
            scratch_shapes=[pltpu.VMEM((PAGE_RING, pps * page_rows, V_DIM), F32),
                            pltpu.VMEM((PAGE_RING, pps * page_rows, V_DIM), F32),
                            pltpu.SemaphoreType.DMA((PAGE_RING, 2)),
                            pltpu.VMEM((n_rows, LANES), F32), pltpu.VMEM((n_rows, LANES), F32),
                            pltpu.VMEM((n_rows, V_DIM), F32)]),
        out_shape=jax.ShapeDtypeStruct((n_seq, n_new, aw), BF16),
        compiler_params=_params("arbitrary", "arbitrary"),
        name="paged_attention",
    )(page_table.reshape(-1), lam, q, k_new, v_new, bias_past, bias_new, subln_g, ck, cv)


def _rel_bias_lookup(rel_bias, dist):
    n = jnp.maximum(dist, 0)
    max_exact = N_BUCKETS // 2
    nf = jnp.maximum(n, 1).astype(F32)
    large = max_exact + jnp.floor(jnp.log(nf / max_exact) / math.log(MAX_DISTANCE / max_exact)
                                  * (N_BUCKETS - max_exact)).astype(jnp.int32)
    large = jnp.minimum(large, N_BUCKETS - 1)
    bucket = jnp.where(n < max_exact, n, large)
    onehot = (bucket.reshape(1, -1) == jnp.arange(N_BUCKETS)[:, None]).astype(F32)
    table = jnp.dot(rel_bias.astype(F32).T, onehot, precision=lax.Precision.HIGHEST) * LOG2E
    return table.reshape((rel_bias.shape[1],) + dist.shape)


def _prompt_bias_tables(rel_bias, tq):
    i = jnp.arange(tq)[:, None]
    j = jnp.arange(tq)[None, :]
    diag = jnp.where((j <= i)[None], _rel_bias_lookup(rel_bias, i - j), -jnp.inf)
    sub = _rel_bias_lookup(rel_bias, tq + i - j)
    both = lambda b: jnp.concatenate([b, b], axis=1)
    return both(diag), both(sub), rel_bias[N_BUCKETS - 1].astype(F32) * LOG2E


def _head_interleaved_bias(rel_bias, base, n_new, n_keys, causal):
    lane = jnp.arange(n_keys * N_HEADS)
    key, lane_head = lane // N_HEADS, lane % N_HEADS
    t = jnp.arange(n_new)[:, None]
    table = _rel_bias_lookup(rel_bias, base + t - key[None, :])
    keep = lane_head[None, None, :] == jnp.arange(N_HEADS)[:, None, None]
    if causal:
        keep = keep & (key[None, :] <= t)[None]
    table = jnp.where(keep, table, -jnp.inf)
    return jnp.concatenate([table, table], axis=1).reshape(N_HEADS * 2 * n_new, n_keys * N_HEADS)


def _sample_bias_tables(rel_bias, past_len, n_new, step_keys):
    assert step_keys >= MAX_DISTANCE
    far = _head_interleaved_bias(rel_bias, past_len, n_new, step_keys, causal=False)
    near = _head_interleaved_bias(rel_bias, step_keys, n_new, step_keys, causal=False)
    new = _head_interleaved_bias(rel_bias, 0, n_new, LANES // N_HEADS, causal=True)
    return jnp.stack([far, near]), new


def _routing_plan(route_p, cnt_p, route_s, cnt_s, n_rows):
    tm = EXPERT_TILE
    bucket_p = route_p[..., 0].reshape(-1).astype(jnp.int32)
    rank_p = route_p[..., 1].reshape(-1).astype(jnp.int32)
    bucket_s = route_s[..., 0].reshape(-1).astype(jnp.int32)
    rank_s = route_s[..., 1].reshape(-1).astype(jnp.int32)
    tiles = jnp.ceil((cnt_p[0, :N_PAIR_BUCKETS] + cnt_s[0, :N_PAIR_BUCKETS]) / tm).astype(jnp.int32)
    cp = cnt_p[0, :N_PAIR_BUCKETS].astype(jnp.int32)
    tile_end = jnp.cumsum(tiles)
    off = (tile_end - tiles) * tm
    lookup = lambda table, idx: jnp.sum(
        jnp.where(idx[:, None] == jnp.arange(table.shape[0])[None, :], table[None, :], 0), axis=1)
    pos_p = lookup(off, bucket_p) + rank_p
    pos_s = lookup(off + cp, bucket_s) + rank_s
    n_tiles = n_rows // tm
    n_used = tile_end[-1]
    ti = jnp.minimum(jnp.arange(n_tiles), n_used - 1)
    tile_bucket = jnp.sum((ti[:, None] >= tile_end[None, :]).astype(jnp.int32), axis=1)
    pair = np.array(EXPERT_PAIRS, np.int32)
    base = np.arange(N_PAIR_BUCKETS) // len(EXPERT_PAIRS) * EXP_PER_GROUP
    tile_ea = lookup(jnp.asarray(base + pair[np.arange(N_PAIR_BUCKETS) % len(EXPERT_PAIRS), 0]), tile_bucket)
    tile_eb = lookup(jnp.asarray(base + pair[np.arange(N_PAIR_BUCKETS) % len(EXPERT_PAIRS), 1]), tile_bucket)
    tile_valid = (jnp.arange(n_tiles) < n_used).astype(jnp.int32)
    return pos_p, pos_s, tile_ea.astype(jnp.int32), tile_eb.astype(jnp.int32), tile_valid


def kernel(x_prompt, x_sample, c_prompt, c_sample, cache_k, cache_v, state_pool, page_table, rel_bias, w_ada, b_ada, w_in, lambda_q1, lambda_k1, lambda_q2, lambda_k2, subln_g, w_pool, pool_scale, w_o, ln1_g, ln1_b, w_router_group, b_router_group, w_router_expert, b_router_expert, w_gate, w_up, w_down, ln2_g, ln2_b):
    depth = w_in.shape[0]
    nb, seq, d = x_prompt.shape
    n_seq, n_new, _ = x_sample.shape
    n_pages = page_table.shape[1]
    page = cache_k.shape[2]
    past_len = n_pages * page
    aw = N_HEADS * V_DIM
    pw = pool_scale.shape[1]
    alpha = (2 * depth) ** 0.25
    tq = min(ATTN_TILE, seq)
    assert tq >= MAX_DISTANCE and seq % tq == 0 and seq % min(SEQ_TILE, seq) == 0
    assert past_len >= POOL_HIST and n_new <= 8
    n_tok_p = nb * seq
    n_tok_s = n_seq * n_new
    assert n_tok_p % min(SEQ_TILE, n_tok_p) == 0
    n_rows = ((n_tok_p + n_tok_s) // EXPERT_TILE + N_PAIR_BUCKETS) * EXPERT_TILE
    pps = math.gcd(PAGES_PER_STEP, n_pages)
    n_steps = n_pages // pps

    bias_diag, bias_sub, c_far = _prompt_bias_tables(rel_bias, tq)
    bias_past, bias_new = _sample_bias_tables(rel_bias, past_len, n_new, pps * page)
    c_all = jnp.concatenate([c_prompt, c_sample], axis=0)
    xs_tm = jnp.transpose(x_sample, (1, 0, 2)).reshape(n_tok_s, d)

    xp, xs_cur = x_prompt, xs_tm
    kp_l, vp_l, pp_l, ks_l, vs_l, ps_l = [], [], [], [], [], []
    for l in range(depth):
        lam_init = 0.8 - 0.6 * math.exp(-0.3 * l)
        lam = (jnp.exp(jnp.sum(lambda_q1[l].astype(F32) * lambda_k1[l].astype(F32)))
               - jnp.exp(jnp.sum(lambda_q2[l].astype(F32) * lambda_k2[l].astype(F32))) + lam_init).reshape(1)
        win_b = w_in[l].astype(BF16)
        wpool_b = w_pool[l].astype(BF16)
        wo_b = w_o[l].astype(BF16)
        wr = jnp.concatenate([w_router_group[l], jnp.transpose(w_router_expert[l], (1, 0, 2)).reshape(d, N_EXPERTS)], axis=1)
        wr_b = jnp.pad(wr, ((0, 0), (0, LANES - wr.shape[1]))).astype(BF16)
        br = jnp.pad(jnp.concatenate([b_router_group[l], b_router_expert[l].reshape(-1)]).astype(F32),
                     (0, LANES - N_EXP_GROUPS - N_EXPERTS)).reshape(1, LANES)
        wg_b = w_gate[l].reshape(N_EXPERTS, d, -1)
        wu_b = w_up[l].reshape(N_EXPERTS, d, -1)
        wd_b = w_down[l].reshape(N_EXPERTS, -1, d)
        pscale = pool_scale[l].reshape(1, pw)
        g_sub = subln_g[l].reshape(1, V_DIM)
        ln1g, ln1b = ln1_g[l].reshape(1, d), ln1_b[l].reshape(1, d)
        ln2g, ln2b = ln2_g[l].reshape(1, d), ln2_b[l].reshape(1, d)

        m_all = _modulation(c_all, w_ada[l], b_ada[l]).reshape(nb + n_seq, 6, d)
        mod_p = m_all[:nb]
        mod_s = jnp.tile(jnp.transpose(m_all[nb:], (1, 0, 2)), (1, n_new, 1))

        q_p, k_p, v_p, kb_p, vb_p, pm_p, ph_p = _prompt_inproj(xp, mod_p, win_b, wpool_b, pscale)
        o_p = _prompt_attention(q_p, kb_p, vb_p, bias_diag, bias_sub, c_far, lam, g_sub, lam_init)
        y1_p, tokx_p, route_p, cnt_p = _outproj_router(o_p, pm_p, xp, mod_p, wo_b, ln1g, ln1b, wr_b, br,
                                                       alpha=alpha, per_row=False)

        hist = state_pool[l]
        q_s, k_s, v_s, p_s, pm_s = _sample_inproj(xs_cur, mod_s[0], mod_s[1], win_b, jnp.transpose(hist, (1, 0, 2)),
                                                  wpool_b, pscale, n_seq=n_seq, n_new=n_new, past_len=past_len)
        seq_major = lambda a: jnp.transpose(a.reshape(n_new, n_seq, -1), (1, 0, 2))
        new_rows = lambda a: jnp.pad(seq_major(a).reshape(n_seq, n_new * N_HEADS, V_DIM),
                                     ((0, 0), (0, LANES - n_new * N_HEADS), (0, 0)))
        o_s = _paged_attention(page_table, lam, seq_major(q_s), new_rows(k_s), new_rows(v_s), bias_past, bias_new, g_sub,
                               cache_k, cache_v, layer=l, lam_init=lam_init)
        o_s_tm = jnp.transpose(o_s, (1, 0, 2)).reshape(1, n_tok_s, aw)
        y1_s, tokx_s, route_s, cnt_s = _outproj_router(o_s_tm, pm_s[None], xs_cur[None], mod_s, wo_b, ln1g, ln1b,
                                                       wr_b, br, alpha=alpha, per_row=True)

        pos_p, pos_s, tile_ea, tile_eb, tile_valid = _routing_plan(route_p, cnt_p, route_s, cnt_s, n_rows)
        xs_sorted = jnp.zeros((n_rows, d + ROUTE_EXTRA), F32)
        xs_sorted = _dispatch(pos_p, tokx_p.reshape(n_tok_p, -1), xs_sorted)
        xs_sorted = _dispatch(pos_s, tokx_s.reshape(n_tok_s, -1), xs_sorted)
        o_sorted = _experts(tile_ea, tile_eb, tile_valid, xs_sorted, wg_b, wu_b, wd_b)
        xp = _combine(pos_p, y1_p, mod_p, ln2g, ln2b, o_sorted, alpha=alpha, per_row=False)
        xs_cur = _combine(pos_s, y1_s, mod_s, ln2g, ln2b, o_sorted, alpha=alpha, per_row=True)[0]

        kp_l.append(k_p.reshape(nb, seq, N_HEADS, V_DIM))
        vp_l.append(v_p.reshape(nb, seq, N_HEADS, V_DIM))
        pp_l.append(ph_p)
        ks_l.append(seq_major(k_s).reshape(n_seq, n_new, N_HEADS, V_DIM))
        vs_l.append(seq_major(v_s).reshape(n_seq, n_new, N_HEADS, V_DIM))
        ps_l.append(jnp.concatenate([hist, seq_major(p_s)], axis=1)[:, -POOL_HIST:])

    y_sample = jnp.transpose(xs_cur.reshape(n_new, n_seq, d), (1, 0, 2))
    return (xp, y_sample, jnp.stack(kp_l), jnp.stack(vp_l), jnp.stack(pp_l),
            jnp.stack(ks_l), jnp.stack(vs_l), jnp.stack(ps_l))
```

```python
import functools
import math

import numpy as np
import jax
import jax.numpy as jnp
from jax import lax
from jax.experimental import pallas as pl
from jax.experimental.pallas import tpu as pltpu

F32 = jnp.float32
BF16 = jnp.bfloat16

N_HEADS = 4
HEAD_DIM = 64
V_DIM = 2 * HEAD_DIM
POOL_WINDOWS = (2, 4, 8, 16)
POOL_GROUP = 128
POOL_HIST = max(POOL_WINDOWS) - 1
N_BUCKETS = 32
MAX_DISTANCE = 128
N_EXP_GROUPS = 4
EXP_PER_GROUP = 4
N_EXPERTS = N_EXP_GROUPS * EXP_PER_GROUP
EXPERT_PAIRS = ((0, 1), (0, 2), (0, 3), (1, 2), (1, 3), (2, 3))
N_PAIR_BUCKETS = N_EXP_GROUPS * len(EXPERT_PAIRS)
LN_EPS = 1e-5
RMS_EPS = 1e-5

LANES = 128
POOL_HALO = 16
VMEM_LIMIT_BYTES = 48 * 1024 * 1024
SEQ_TILE = 512
ATTN_TILE = 256
EXPERT_TILE = 256
PAGES_PER_STEP = 16
PAGE_RING = 3
ROUTE_EXTRA = LANES
LOG2E = math.log2(math.e)
Q_SCALE = HEAD_DIM ** -0.5 * LOG2E


def _params(*sem):
    return pltpu.CompilerParams(dimension_semantics=sem, vmem_limit_bytes=VMEM_LIMIT_BYTES)


def _dot(a, b):
    return jnp.dot(a, b, preferred_element_type=F32)


def _dot_nt(a, b):
    return lax.dot_general(a, b, (((1,), (1,)), ((), ())), preferred_element_type=F32)


def _sigmoid(x):
    return 1.0 / (1.0 + jnp.exp(-x))


def _mod_kernel(c_ref, w_ref, b_ref, o_ref):
    c = c_ref[...]
    s = c * _sigmoid(c)
    s_hi = s.astype(BF16)
    s_lo = (s - s_hi.astype(F32)).astype(BF16)
    w = w_ref[...]
    w_hi = w.astype(BF16)
    w_lo = (w - w_hi.astype(F32)).astype(BF16)
    o_ref[...] = _dot(s_hi, w_hi) + _dot(s_lo, w_hi) + _dot(s_hi, w_lo) + b_ref[...]


def _modulation(c_all, w_ada, b_ada):
    n, d = c_all.shape
    e = w_ada.shape[1]
    bn = 1024
    return pl.pallas_call(
        _mod_kernel,
        grid=(e // bn,),
        in_specs=[pl.BlockSpec((n, d), lambda j: (0, 0)),
                  pl.BlockSpec((d, bn), lambda j: (0, j)),
                  pl.BlockSpec((1, bn), lambda j: (0, j))],
        out_specs=pl.BlockSpec((n, bn), lambda j: (0, j)),
        out_shape=jax.ShapeDtypeStruct((n, e), F32),
        compiler_params=_params("arbitrary"),
        name="modulation",
    )(c_all, w_ada, b_ada.reshape(1, e))


def _pool_mixed(ext, g, w, inv_cnt, rows):
    eg = ext[:, g * POOL_GROUP:(g + 1) * POOL_GROUP]
    s = eg
    step = 1
    while step < w:
        s = s + pltpu.roll(s, step, 0)
        step *= 2
    return s[POOL_HALO:POOL_HALO + rows] * inv_cnt - eg[POOL_HALO:POOL_HALO + rows]


def _inproj_kernel(x_ref, mod_ref, win_ref, wpool_ref, pscale_ref,
                   q_ref, k_ref, v_ref, kb_ref, vb_ref, pm_ref, ph_ref, ext_ref, *, ts, aw):
    i = pl.program_id(1)

    @pl.when(i == 0)
    def _():
        ext_ref[0:POOL_HALO, :] = jnp.zeros((POOL_HALO, ext_ref.shape[1]), F32)

    @pl.when(i > 0)
    def _():
        ext_ref[0:POOL_HALO, :] = ext_ref[ts:ts + POOL_HALO, :]

    shift = mod_ref[0, 0:1, :]
    scale = mod_ref[0, 1:2, :]
    u = (x_ref[0] * (1.0 + scale) + shift).astype(BF16)
    acc = _dot(u, win_ref[...])
    q_ref[0] = (acc[:, 0:aw] * Q_SCALE).astype(BF16)
    k = acc[:, aw:2 * aw]
    v = acc[:, 2 * aw:3 * aw]
    for h in range(N_HEADS):
        k_ref[0, pl.ds(h, ts, stride=N_HEADS), :] = k[:, h * V_DIM:(h + 1) * V_DIM]
        v_ref[0, pl.ds(h, ts, stride=N_HEADS), :] = v[:, h * V_DIM:(h + 1) * V_DIM]
    kb_ref[0] = k.astype(BF16)
    vb_ref[0] = v.astype(BF16)
    p = acc[:, 3 * aw:]
    ext_ref[POOL_HALO:POOL_HALO + ts, :] = p
    ext = ext_ref[...]
    pos = i * ts + lax.broadcasted_iota(jnp.int32, (ts, 1), 0)
    for g, w in enumerate(POOL_WINDOWS):
        inv_cnt = 1.0 / jnp.minimum(w, pos + 1).astype(F32)
        mixed = _pool_mixed(ext, g, w, inv_cnt, ts)
        cols = slice(g * POOL_GROUP, (g + 1) * POOL_GROUP)
        pm_ref[0, :, cols] = (_dot(mixed.astype(BF16), wpool_ref[g]) * pscale_ref[:, cols]).astype(BF16)

    @pl.when(i == pl.num_programs(1) - 1)
    def _():
        ph_ref[0] = p[ts - POOL_HIST:ts, :]


def _prompt_inproj(x, mod, win_b, wpool_b, pscale):
    nb, s, d = x.shape
    ew = win_b.shape[1]
    aw = (ew - 4 * POOL_GROUP) // 3
    pw = ew - 3 * aw
    ts = min(SEQ_TILE, s)
    nt = s // ts
    tile = lambda b, i: (b, i, 0)
    const2 = lambda b, i: (0, 0)
    return pl.pallas_call(
        functools.partial(_inproj_kernel, ts=ts, aw=aw),
        grid=(nb, nt),
        in_specs=[pl.BlockSpec((1, ts, d), tile),
                  pl.BlockSpec((1, 6, d), lambda b, i: (b, 0, 0)),
                  pl.BlockSpec((d, ew), const2),
                  pl.BlockSpec(wpool_b.shape, lambda b, i: (0, 0, 0)),
                  pl.BlockSpec((1, pw), const2)],
        out_specs=[pl.BlockSpec((1, ts, aw), tile),
                   pl.BlockSpec((1, ts * N_HEADS, V_DIM), tile),
                   pl.BlockSpec((1, ts * N_HEADS, V_DIM), tile),
                   pl.BlockSpec((1, ts, aw), tile), pl.BlockSpec((1, ts, aw), tile),
                   pl.BlockSpec((1, ts, pw), tile),
                   pl.BlockSpec((1, POOL_HIST, pw), lambda b, i: (b, 0, 0))],
        out_shape=[jax.ShapeDtypeStruct((nb, s, aw), BF16),
                   jax.ShapeDtypeStruct((nb, s * N_HEADS, V_DIM), F32),
                   jax.ShapeDtypeStruct((nb, s * N_HEADS, V_DIM), F32),
                   jax.ShapeDtypeStruct((nb, s, aw), BF16),
                   jax.ShapeDtypeStruct((nb, s, aw), BF16),
                   jax.ShapeDtypeStruct((nb, s, pw), BF16),
                   jax.ShapeDtypeStruct((nb, POOL_HIST, pw), F32)],
        scratch_shapes=[pltpu.VMEM((ts + POOL_HALO, pw), F32)],
        compiler_params=_params("arbitrary", "arbitrary"),
        name="prompt_inproj",
    )(x, mod, win_b, wpool_b, pscale)


def _split_branches(qh):
    lane = lax.broadcasted_iota(jnp.int32, qh.shape, 1)
    zero = jnp.zeros_like(qh)
    return jnp.concatenate([jnp.where(lane < HEAD_DIM, qh, zero), jnp.where(lane >= HEAD_DIM, qh, zero)], axis=0)


def _lane_tile(a, width):
    return jnp.concatenate([a] * (width // LANES), axis=1) if width > LANES else a


def _online_update(s, values, m_ref, l_ref, acc_ref, first=False):
    rows, width = s.shape
    m_new = jnp.broadcast_to(jnp.max(s, axis=1, keepdims=True), (rows, LANES))
    if not first:
        m_old = m_ref[...]
        m_new = jnp.maximum(m_old, m_new)
    p = jnp.exp2(s - _lane_tile(m_new, width))
    pv = values(p.astype(BF16))
    l_new = jnp.broadcast_to(jnp.sum(p, axis=1, keepdims=True), (rows, LANES))
    if first:
        acc_ref[...] = pv
        l_ref[...] = l_new
    else:
        alpha = jnp.exp2(m_old - m_new)
        acc_ref[...] = _lane_tile(alpha, pv.shape[1]) * acc_ref[...] + pv
        l_ref[...] = alpha * l_ref[...] + l_new
    m_ref[...] = m_new


def _diff_head_out(acc, l, t, lam, g, lam_init):
    o = acc[0:t] / l[0:t] - lam * (acc[t:2 * t] / l[t:2 * t])
    o = o * lax.rsqrt(jnp.mean(o * o, axis=-1, keepdims=True) + RMS_EPS) * g
    return o * (1.0 - lam_init)


def _attn_kernel(cfar_ref, lam_ref, q_ref, k_ref, v_ref, bd_ref, be_ref, g_ref, o_ref,
                 qs_sc, m_sc, l_sc, acc_sc, *, tq, lam_init):
    qi = pl.program_id(1)
    heads = [slice(h * V_DIM, (h + 1) * V_DIM) for h in range(N_HEADS)]
    for h, cols in enumerate(heads):
        qs_sc[h] = _split_branches(q_ref[0, :, cols])

    def step(j, bias, first):
        rows = pl.ds(pl.multiple_of(j * tq, tq), tq)
        for h, cols in enumerate(heads):
            s = _dot_nt(qs_sc[h], k_ref[0, rows, cols]) + bias(h)
            _online_update(s, lambda p, cols=cols: _dot(p, v_ref[0, rows, cols]),
                           m_sc.at[h], l_sc.at[h], acc_sc.at[h], first=first)

    step(qi, lambda h: bd_ref[h], True)

    @pl.when(qi >= 1)
    def _():
        step(qi - 1, lambda h: be_ref[h], False)

    def far(j, carry):
        step(j, lambda h: cfar_ref[h], False)
        return carry

    lax.fori_loop(0, jnp.maximum(qi - 1, 0), far, 0)
    lam = lam_ref[0]
    for h, cols in enumerate(heads):
        o = _diff_head_out(acc_sc[h], l_sc[h], tq, lam, g_ref[...], lam_init)
        o_ref[0, :, cols] = o.astype(BF16)


def _prompt_attention(q, kb, vb, bias_diag, bias_sub, c_far, lam, subln_g, lam_init):
    nb, s, aw = q.shape
    tq = bias_diag.shape[2]
    smem = pl.BlockSpec(memory_space=pltpu.SMEM)
    const3 = lambda b, i: (0, 0, 0)
    return pl.pallas_call(
        functools.partial(_attn_kernel, tq=tq, lam_init=lam_init),
        grid=(nb, s // tq),
        in_specs=[smem, smem,
                  pl.BlockSpec((1, tq, aw), lambda b, i: (b, i, 0)),
                  pl.BlockSpec((1, s, aw), lambda b, i: (b, 0, 0)),
                  pl.BlockSpec((1, s, aw), lambda b, i: (b, 0, 0)),
                  pl.BlockSpec(bias_diag.shape, const3),
                  pl.BlockSpec(bias_sub.shape, const3),
                  pl.BlockSpec((1, V_DIM), lambda b, i: (0, 0))],
        out_specs=pl.BlockSpec((1, tq, aw), lambda b, i: (b, i, 0)),
        out_shape=jax.ShapeDtypeStruct((nb, s, aw), BF16),
        scratch_shapes=[pltpu.VMEM((N_HEADS, 2 * tq, V_DIM), BF16),
                        pltpu.VMEM((N_HEADS, 2 * tq, LANES), F32), pltpu.VMEM((N_HEADS, 2 * tq, LANES), F32),
                        pltpu.VMEM((N_HEADS, 2 * tq, V_DIM), F32)],
        compiler_params=_params("arbitrary", "arbitrary"),
        name="prompt_attention",
    )(c_far, lam, q, kb, vb, bias_diag, bias_sub, subln_g)


def _layer_norm(z, g, b):
    mu = jnp.mean(z, axis=-1, keepdims=True)
    zc = z - mu
    var = jnp.mean(zc * zc, axis=-1, keepdims=True)
    return zc * lax.rsqrt(var + LN_EPS) * g + b


def _mod_get(mod_ref, j, per_row):
    return mod_ref[j] if per_row else mod_ref[0, j:j + 1, :]


def _outproj_router_kernel(o_ref, pm_ref, x_ref, mod_ref, wo_ref, lng_ref, lnb_ref, wr_ref, br_ref,
                           y_ref, tokx_ref, route_ref, cnt_ref, carry_sc, *, alpha, per_row):
    t = x_ref.shape[1]
    d = x_ref.shape[2]
    aw = o_ref.shape[2]

    @pl.when((pl.program_id(0) == 0) & (pl.program_id(1) == 0))
    def _():
        carry_sc[...] = jnp.zeros(carry_sc.shape, F32)

    h = _dot(o_ref[0], wo_ref[0:aw, :]) + _dot(pm_ref[0], wo_ref[aw:, :])
    gate1 = _mod_get(mod_ref, 2, per_row)
    y1 = _layer_norm(alpha * x_ref[0] + gate1 * h, lng_ref[...], lnb_ref[...])
    y_ref[0] = y1
    tok = y1 * (1.0 + _mod_get(mod_ref, 4, per_row)) + _mod_get(mod_ref, 3, per_row)
    tokx_ref[0, :, 0:d] = tok

    logits = _dot(tok.astype(BF16), wr_ref[...]) + br_ref[...]
    lane = lax.broadcasted_iota(jnp.int32, logits.shape, 1)
    lane_f = lane.astype(F32)
    neg = jnp.full_like(logits, -jnp.inf)

    def first_argmax(vals, vmax):
        return jnp.min(jnp.where(vals == vmax, lane_f, float(LANES)), axis=1, keepdims=True).astype(jnp.int32)

    gl = jnp.where(lane < N_EXP_GROUPS, logits, neg)
    gmax = jnp.max(gl, axis=1, keepdims=True)
    gidx = first_argmax(gl, gmax)
    g_w = 1.0 / jnp.sum(jnp.exp(gl - gmax), axis=1, keepdims=True)
    lo_lane = N_EXP_GROUPS + EXP_PER_GROUP * gidx
    el = jnp.where((lane >= lo_lane) & (lane < lo_lane + EXP_PER_GROUP), logits, neg)
    v1 = jnp.max(el, axis=1, keepdims=True)
    i1 = first_argmax(el, v1)
    el2 = jnp.where(lane == i1, neg, el)
    v2 = jnp.max(el2, axis=1, keepdims=True)
    i2 = first_argmax(el2, v2)
    e21 = jnp.exp(v2 - v1)
    w1 = g_w / (1.0 + e21)
    w2 = g_w * e21 / (1.0 + e21)
    first_lo = i1 < i2
    cw_lo = jnp.where(first_lo, w1, w2)
    cw_hi = jnp.where(first_lo, w2, w1)
    a = jnp.minimum(i1, i2) - lo_lane
    b = jnp.maximum(i1, i2) - lo_lane
    pair_base = jnp.where(a == 0, 0, jnp.where(a == 1, 3, 5))
    bucket = gidx * len(EXPERT_PAIRS) + pair_base + (b - a - 1)

    xlane = lax.broadcasted_iota(jnp.int32, (t, ROUTE_EXTRA), 1)
    tokx_ref[0, :, d:] = jnp.where(xlane == 0, cw_lo, jnp.where(xlane == 1, cw_hi, 0.0))

    onehot = lane == bucket
    row = lax.broadcasted_iota(jnp.int32, (t, t), 0)
    col = lax.broadcasted_iota(jnp.int32, (t, t), 1)
    ltri = jnp.where(col < row, 1.0, 0.0).astype(BF16)
    prefix = _dot(ltri, jnp.where(onehot, 1.0, 0.0).astype(BF16)) + carry_sc[...]
    rank = jnp.sum(jnp.where(onehot, prefix, 0.0), axis=1, keepdims=True)
    carry_sc[...] = carry_sc[...] + jnp.sum(jnp.where(onehot, 1.0, 0.0), axis=0, keepdims=True)
    cnt_ref[...] = carry_sc[...]
    route = jnp.where(lane == 0, bucket.astype(F32), jnp.where(lane == 1, rank, 0.0))
    route_ref[0] = route.T[0:8, :]


def _outproj_router(o, pm, x, mod, wo_b, ln_g, ln_b, wr_b, br, *, alpha, per_row):
    nb, s, d = x.shape
    aw = o.shape[2]
    pw = pm.shape[2]
    t = min(SEQ_TILE, s)
    tile = lambda b, i: (b, i, 0)
    const2 = lambda b, i: (0, 0)
    mod_spec = (pl.BlockSpec(mod.shape, lambda b, i: (0, 0, 0)) if per_row
                else pl.BlockSpec((1, 6, d), lambda b, i: (b, 0, 0)))
    return pl.pallas_call(
        functools.partial(_outproj_router_kernel, alpha=alpha, per_row=per_row),
        grid=(nb, s // t),
        in_specs=[pl.BlockSpec((1, t, aw), tile), pl.BlockSpec((1, t, pw), tile), pl.BlockSpec((1, t, d), tile),
                  mod_spec,
                  pl.BlockSpec(wo_b.shape, const2), pl.BlockSpec((1, d), const2), pl.BlockSpec((1, d), const2),
                  pl.BlockSpec(wr_b.shape, const2), pl.BlockSpec((1, LANES), const2)],
        out_specs=[pl.BlockSpec((1, t, d), tile), pl.BlockSpec((1, t, d + ROUTE_EXTRA), tile),
                   pl.BlockSpec((1, 8, t), lambda b, i: (b, 0, i)), pl.BlockSpec((1, LANES), const2)],
        out_shape=[jax.ShapeDtypeStruct((nb, s, d), F32), jax.ShapeDtypeStruct((nb, s, d + ROUTE_EXTRA), F32),
                   jax.ShapeDtypeStruct((nb, 8, s), F32), jax.ShapeDtypeStruct((1, LANES), F32)],
        scratch_shapes=[pltpu.VMEM((1, LANES), F32)],
        compiler_params=_params("arbitrary", "arbitrary"),
        name="outproj_router",
    )(o, pm, x, mod, wo_b, ln_g, ln_b, wr_b, br)


def _issue_row_copies(src_ref, dst_ref, sem, src_row, dst_row, n):
    for r in range(n):
        pltpu.make_async_copy(src_ref.at[pl.ds(src_row(r), 1)], dst_ref.at[pl.ds(dst_row(r), 1)],
                              sem).start(priority=r % 2)


def _wait_row_copies(src_ref, dst_ref, sem, n):
    pltpu.make_async_copy(src_ref.at[pl.ds(0, n)], dst_ref.at[pl.ds(0, n)], sem).wait()


def _dispatch_kernel(pos_ref, tok_ref, xs_in_ref, xs_ref, sem, *, t):
    del xs_in_ref
    base = pl.program_id(0) * t
    _issue_row_copies(tok_ref, xs_ref, sem, lambda r: r, lambda r: pos_ref[base + r], t)
    _wait_row_copies(tok_ref, xs_ref, sem, t)


def _dispatch(pos, tokx, xs):
    n, w = tokx.shape
    t = min(SEQ_TILE, n)
    return pl.pallas_call(
        functools.partial(_dispatch_kernel, t=t),
        grid_spec=pltpu.PrefetchScalarGridSpec(
            num_scalar_prefetch=1,
            grid=(n // t,),
            in_specs=[pl.BlockSpec((t, w), lambda i, pos: (i, 0)), pl.BlockSpec(memory_space=pl.ANY)],
            out_specs=pl.BlockSpec(memory_space=pl.ANY),
            scratch_shapes=[pltpu.SemaphoreType.DMA(())]),
        out_shape=jax.ShapeDtypeStruct(xs.shape, xs.dtype),
        input_output_aliases={2: 0},
        compiler_params=_params("arbitrary"),
        name="dispatch_rows",
    )(pos, tokx, xs)


def _expert_kernel(ea_ref, eb_ref, valid_ref, xs_ref, wga_ref, wua_ref, wda_ref, wgb_ref, wub_ref, wdb_ref,
                   o_ref, *, d):
    del ea_ref, eb_ref
    ti = pl.program_id(0)

    @pl.when(valid_ref[ti] == 1)
    def _():
        xs = xs_ref[...]
        x = xs[:, 0:d].astype(BF16)

        def ffn(wg_ref, wu_ref, wd_ref, cw):
            g = _dot(x, wg_ref[0].astype(BF16))
            u = _dot(x, wu_ref[0].astype(BF16))
            hid = (g * _sigmoid(g)) * u * cw
            return _dot(hid.astype(BF16), wd_ref[0].astype(BF16))

        o_ref[...] = (ffn(wga_ref, wua_ref, wda_ref, xs[:, d:d + 1])
                      + ffn(wgb_ref, wub_ref, wdb_ref, xs[:, d + 1:d + 2]))

    @pl.when(valid_ref[ti] == 0)
    def _():
        o_ref[...] = jnp.zeros(o_ref.shape, F32)


def _experts(tile_ea, tile_eb, tile_valid, xs, wg_b, wu_b, wd_b):
    r, w = xs.shape
    d = w - ROUTE_EXTRA
    f = wg_b.shape[2]
    tm = EXPERT_TILE
    wa = lambda blk: pl.BlockSpec(blk, lambda i, ea, eb, va: (ea[i], 0, 0))
    wb = lambda blk: pl.BlockSpec(blk, lambda i, ea, eb, va: (eb[i], 0, 0))
    return pl.pallas_call(
        functools.partial(_expert_kernel, d=d),
        grid_spec=pltpu.PrefetchScalarGridSpec(
            num_scalar_prefetch=3,
            grid=(r // tm,),
            in_specs=[pl.BlockSpec((tm, w), lambda i, ea, eb, va: (i, 0)),
                      wa((1, d, f)), wa((1, d, f)), wa((1, f, d)),
                      wb((1, d, f)), wb((1, d, f)), wb((1, f, d))],
            out_specs=pl.BlockSpec((tm, d), lambda i, ea, eb, va: (i, 0))),
        out_shape=jax.ShapeDtypeStruct((r, d), F32),
        compiler_params=_params("arbitrary"),
        name="expert_ffn",
    )(tile_ea, tile_eb, tile_valid, xs, wg_b, wu_b, wd_b, wg_b, wu_b, wd_b)


def _combine_kernel(pos_ref, y_ref, mod_ref, lng_ref, lnb_ref, os_ref, out_ref, buf, sem, *, alpha, per_row):
    t = y_ref.shape[1]
    k = pl.program_id(0) * pl.num_programs(1) + pl.program_id(1)
    n_tiles = pl.num_programs(0) * pl.num_programs(1)

    def fetch(tile):
        slot = tile % 2
        _issue_row_copies(os_ref, buf.at[slot], sem.at[slot], lambda r: pos_ref[tile * t + r], lambda r: r, t)

    @pl.when(k == 0)
    def _():
        fetch(k)

    @pl.when(k + 1 < n_tiles)
    def _():
        fetch(k + 1)

    slot = k % 2
    _wait_row_copies(os_ref, buf.at[slot], sem.at[slot], t)
    gate2 = _mod_get(mod_ref, 5, per_row)
    out_ref[0] = _layer_norm(alpha * y_ref[0] + gate2 * buf[slot], lng_ref[...], lnb_ref[...])


def _combine(pos, y1, mod, ln_g, ln_b, o_sorted, *, alpha, per_row):
    nb, s, d = y1.shape
    t = min(SEQ_TILE, s)
    tile = lambda b, i, pos: (b, i, 0)
    const2 = lambda b, i, pos: (0, 0)
    mod_spec = (pl.BlockSpec(mod.shape, lambda b, i, pos: (0, 0, 0)) if per_row
                else pl.BlockSpec((1, 6, d), lambda b, i, pos: (b, 0, 0)))
    return pl.pallas_call(
        functools.partial(_combine_kernel, alpha=alpha, per_row=per_row),
        grid_spec=pltpu.PrefetchScalarGridSpec(
            num_scalar_prefetch=1,
            grid=(nb, s // t),
            in_specs=[pl.BlockSpec((1, t, d), tile), mod_spec,
                      pl.BlockSpec((1, d), const2), pl.BlockSpec((1, d), const2),
                      pl.BlockSpec(memory_space=pl.ANY)],
            out_specs=pl.BlockSpec((1, t, d), tile),
            scratch_shapes=[pltpu.VMEM((2, t, d), F32), pltpu.SemaphoreType.DMA((2,))]),
        out_shape=jax.ShapeDtypeStruct((nb, s, d), F32),
        compiler_params=_params("arbitrary", "arbitrary"),
        name="combine_rows",
    )(pos, y1, mod, ln_g, ln_b, o_sorted)


def _sample_inproj_kernel(x_ref, shift_ref, scale_ref, win_ref, hist_ref, wpool_ref, pscale_ref,
                          q_ref, k_ref, v_ref, p_ref, pm_ref, *, aw, n_seq, n_new, past_len):
    u = (x_ref[...] * (1.0 + scale_ref[...]) + shift_ref[...]).astype(BF16)
    acc = _dot(u, win_ref[...])
    q_ref[...] = (acc[:, 0:aw] * Q_SCALE).astype(BF16)
    k_ref[...] = acc[:, aw:2 * aw]
    v_ref[...] = acc[:, 2 * aw:3 * aw]
    p = acc[:, 3 * aw:]
    p_ref[...] = p
    n_hist = hist_ref.shape[0]
    rows = [hist_ref[j] for j in range(n_hist)] + [p[t * n_seq:(t + 1) * n_seq] for t in range(n_new)]
    for g, w in enumerate(POOL_WINDOWS):
        cols = slice(g * POOL_GROUP, (g + 1) * POOL_GROUP)
        mixed = []
        for t in range(n_new):
            end = n_hist + t + 1
            start = max(end - w, 0)
            win_sum = rows[start][:, cols]
            for j in range(start + 1, end):
                win_sum = win_sum + rows[j][:, cols]
            count = float(min(w, past_len + t + 1))
            mixed.append(win_sum / count - rows[n_hist + t][:, cols])
        mixed = jnp.concatenate(mixed, axis=0)
        pm_ref[:, cols] = (_dot(mixed.astype(BF16), wpool_ref[g]) * pscale_ref[:, cols]).astype(BF16)


def _sample_inproj(x, shift, scale, win_b, hist_t, wpool_b, pscale, *, n_seq, n_new, past_len):
    n, d = x.shape
    ew = win_b.shape[1]
    aw = (ew - 4 * POOL_GROUP) // 3
    pw = ew - 3 * aw
    return pl.pallas_call(
        functools.partial(_sample_inproj_kernel, aw=aw, n_seq=n_seq, n_new=n_new, past_len=past_len),
        out_shape=[jax.ShapeDtypeStruct((n, aw), BF16), jax.ShapeDtypeStruct((n, aw), F32),
                   jax.ShapeDtypeStruct((n, aw), F32), jax.ShapeDtypeStruct((n, pw), F32),
                   jax.ShapeDtypeStruct((n, pw), BF16)],
        compiler_params=pltpu.CompilerParams(vmem_limit_bytes=VMEM_LIMIT_BYTES),
        name="sample_inproj",
    )(x, shift, scale, win_b, hist_t, wpool_b, pscale)


def _paged_attn_kernel(pt_ref, lam_ref, q_ref, kn_ref, vn_ref, bias_ref, bn_ref, g_ref, ck_ref, cv_ref, o_ref,
                       kbuf, vbuf, sem, m_sc, l_sc, acc_sc, *, pps, page_rows, first_page, n_new, lam_init):
    step = pl.program_id(1)
    n_steps = pl.num_programs(1)
    last = n_steps - 1
    g = pl.program_id(0) * n_steps + step
    total = pl.num_programs(0) * n_steps
    heads = [slice(h * V_DIM, (h + 1) * V_DIM) for h in range(N_HEADS)]

    def fetch(gs):
        slot = gs % PAGE_RING
        for pg in range(pps):
            src = pl.ds(pl.multiple_of((first_page + pt_ref[gs * pps + pg]) * page_rows, page_rows), page_rows)
            dst = pl.ds(pg * page_rows, page_rows)
            pltpu.make_async_copy(ck_ref.at[src], kbuf.at[slot, dst], sem.at[slot, 0]).start()
            pltpu.make_async_copy(cv_ref.at[src], vbuf.at[slot, dst], sem.at[slot, 1]).start()

    @pl.when(g == 0)
    def _():
        for gs in range(PAGE_RING - 1):
            fetch(gs)

    @pl.when(g + PAGE_RING - 1 < total)
    def _():
        fetch(g + PAGE_RING - 1)

    @pl.when(step == 0)
    def _():
        m_sc[...] = jnp.full(m_sc.shape, -jnp.inf, F32)
        l_sc[...] = jnp.zeros(l_sc.shape, F32)
        acc_sc[...] = jnp.zeros(acc_sc.shape, F32)

    slot = g % PAGE_RING
    pltpu.make_async_copy(ck_ref.at[pl.ds(0, pps * page_rows)], kbuf.at[slot], sem.at[slot, 0]).wait()
    pltpu.make_async_copy(cv_ref.at[pl.ds(0, pps * page_rows)], vbuf.at[slot], sem.at[slot, 1]).wait()

    qn = q_ref[0]
    q_rows = jnp.concatenate([_split_branches(qn[:, cols]) for cols in heads], axis=0)
    bias = bias_ref[jnp.where(step == last, 1, 0)]
    page = lambda buf, pg: buf[slot, pg * page_rows:(pg + 1) * page_rows, :].astype(BF16)
    s = jnp.concatenate([_dot_nt(q_rows, page(kbuf, pg)) for pg in range(pps)], axis=1) + bias

    def past_values(p):
        acc = _dot(p[:, 0:page_rows], page(vbuf, 0))
        for pg in range(1, pps):
            acc = acc + _dot(p[:, pg * page_rows:(pg + 1) * page_rows], page(vbuf, pg))
        return acc

    _online_update(s, past_values, m_sc, l_sc, acc_sc)

    @pl.when(step == last)
    def _():
        sn = _dot_nt(q_rows, kn_ref[0].astype(BF16)) + bn_ref[...]
        _online_update(sn, lambda p: _dot(p, vn_ref[0].astype(BF16)), m_sc, l_sc, acc_sc)
        lam = lam_ref[0]
        for h, cols in enumerate(heads):
            rows = slice(h * 2 * n_new, (h + 1) * 2 * n_new)
            o = _diff_head_out(acc_sc[rows, :], l_sc[rows, :], n_new, lam, g_ref[...], lam_init)
            o_ref[0, :, cols] = o.astype(BF16)


def _paged_attention(page_table, lam, q, k_new, v_new, bias_past, bias_new, subln_g, cache_k, cache_v, *,
                     layer, lam_init):
    n_seq, n_pages = page_table.shape
    n_new, aw = q.shape[1], q.shape[2]
    n_phys, page = cache_k.shape[1], cache_k.shape[2]
    pps = math.gcd(PAGES_PER_STEP, n_pages)
    n_steps = n_pages // pps
    assert n_seq * n_steps >= PAGE_RING - 1
    n_rows = N_HEADS * 2 * n_new
    page_rows = page * N_HEADS
    ck = cache_k.reshape(-1, V_DIM)
    cv = cache_v.reshape(-1, V_DIM)
    per_seq = lambda n, s, pt: (n, 0, 0)
    hbm = pl.BlockSpec(memory_space=pl.ANY)
    return pl.pallas_call(
        functools.partial(_paged_attn_kernel, pps=pps, page_rows=page_rows, first_page=layer * n_phys,
                          n_new=n_new, lam_init=lam_init),
        grid_spec=pltpu.PrefetchScalarGridSpec(
            num_scalar_prefetch=1,
            grid=(n_seq, n_steps),
            in_specs=[pl.BlockSpec(memory_space=pltpu.SMEM),
                      pl.BlockSpec((1, n_new, aw), per_seq),
                      pl.BlockSpec((1,) + k_new.shape[1:], per_seq),
                      pl.BlockSpec((1,) + v_new.shape[1:], per_seq),
                      pl.BlockSpec(bias_past.shape, lambda n, s, pt: (0, 0, 0)),
                      pl.BlockSpec(bias_new.shape, lambda n, s, pt: (0, 0)),
                      pl.BlockSpec((1, V_DIM), lambda n, s, pt: (0, 0)), hbm, hbm],
            out_specs=pl.BlockSpec((1, n_new, aw), per_seq),
            scratch_shapes=[pltpu.VMEM((PAGE_RING, pps * page_rows, V_DIM), F32),
                            pltpu.VMEM((PAGE_RING, pps * page_rows, V_DIM), F32),
                            pltpu.SemaphoreType.DMA((PAGE_RING, 2)),
                            pltpu.VMEM((n_rows, LANES), F32), pltpu.VMEM((n_rows, LANES), F32),
                            pltpu.VMEM((n_rows, V_DIM), F32)]),
        out_shape=jax.ShapeDtypeStruct((n_seq, n_new, aw), BF16),
        compiler_params=_params("arbitrary", "arbitrary"),
        name="paged_attention",
    )(page_table.reshape(-1), lam, q, k_new, v_new, bias_past, bias_new, subln_g, ck, cv)


def _rel_bias_lookup(rel_bias, dist):
    n = jnp.maximum(dist, 0)
    max_exact = N_BUCKETS // 2
    nf = jnp.maximum(n, 1).astype(F32)
    large = max_exact + jnp.floor(jnp.log(nf / max_exact) / math.log(MAX_DISTANCE / max_exact)
                                  * (N_BUCKETS - max_exact)).astype(jnp.int32)
    large = jnp.minimum(large, N_BUCKETS - 1)
    bucket = jnp.where(n < max_exact, n, large)
    onehot = (bucket.reshape(1, -1) == jnp.arange(N_BUCKETS)[:, None]).astype(F32)
    table = jnp.dot(rel_bias.astype(F32).T, onehot, precision=lax.Precision.HIGHEST) * LOG2E
    return table.reshape((rel_bias.shape[1],) + dist.shape)


def _prompt_bias_tables(rel_bias, tq):
    i = jnp.arange(tq)[:, None]
    j = jnp.arange(tq)[None, :]
    diag = jnp.where((j <= i)[None], _rel_bias_lookup(rel_bias, i - j), -jnp.inf)
    sub = _rel_bias_lookup(rel_bias, tq + i - j)
    both = lambda b: jnp.concatenate([b, b], axis=1)
    return both(diag), both(sub), rel_bias[N_BUCKETS - 1].astype(F32) * LOG2E


def _head_interleaved_bias(rel_bias, base, n_new, n_keys, causal):
    lane = jnp.arange(n_keys * N_HEADS)
    key, lane_head = lane // N_HEADS, lane % N_HEADS
    t = jnp.arange(n_new)[:, None]
    table = _rel_bias_lookup(rel_bias, base + t - key[None, :])
    keep = lane_head[None, None, :] == jnp.arange(N_HEADS)[:, None, None]
    if causal:
        keep = keep & (key[None, :] <= t)[None]
    table = jnp.where(keep, table, -jnp.inf)
    return jnp.concatenate([table, table], axis=1).reshape(N_HEADS * 2 * n_new, n_keys * N_HEADS)


def _sample_bias_tables(rel_bias, past_len, n_new, step_keys):
    assert step_keys >= MAX_DISTANCE
    far = _head_interleaved_bias(rel_bias, past_len, n_new, step_keys, causal=False)
    near = _head_interleaved_bias(rel_bias, step_keys, n_new, step_keys, causal=False)
    new = _head_interleaved_bias(rel_bias, 0, n_new, LANES // N_HEADS, causal=True)
    return jnp.stack([far, near]), new


def _routing_plan(route_p, cnt_p, route_s, cnt_s, n_rows):
    tm = EXPERT_TILE
    bucket_p = route_p[:, 0, :].reshape(-1).astype(jnp.int32)
    rank_p = route_p[:, 1, :].reshape(-1).astype(jnp.int32)
    bucket_s = route_s[:, 0, :].reshape(-1).astype(jnp.int32)
    rank_s = route_s[:, 1, :].reshape(-1).astype(jnp.int32)
    tiles = jnp.ceil((cnt_p[0, :N_PAIR_BUCKETS] + cnt_s[0, :N_PAIR_BUCKETS]) / tm).astype(jnp.int32)
    cp = cnt_p[0, :N_PAIR_BUCKETS].astype(jnp.int32)
    tile_end = jnp.cumsum(tiles)
    off = (tile_end - tiles) * tm
    lookup = lambda table, idx: jnp.sum(
        jnp.where(idx[:, None] == jnp.arange(table.shape[0])[None, :], table[None, :], 0), axis=1)
    pos_p = lookup(off, bucket_p) + rank_p
    pos_s = lookup(off + cp, bucket_s) + rank_s
    n_tiles = n_rows // tm
    n_used = tile_end[-1]
    ti = jnp.minimum(jnp.arange(n_tiles), n_used - 1)
    tile_bucket = jnp.sum((ti[:, None] >= tile_end[None, :]).astype(jnp.int32), axis=1)
    pair = np.array(EXPERT_PAIRS, np.int32)
    base = np.arange(N_PAIR_BUCKETS) // len(EXPERT_PAIRS) * EXP_PER_GROUP
    tile_ea = lookup(jnp.asarray(base + pair[np.arange(N_PAIR_BUCKETS) % len(EXPERT_PAIRS), 0]), tile_bucket)
    tile_eb = lookup(jnp.asarray(base + pair[np.arange(N_PAIR_BUCKETS) % len(EXPERT_PAIRS), 1]), tile_bucket)
    tile_valid = (jnp.arange(n_tiles) < n_used).astype(jnp.int32)
    return pos_p, pos_s, tile_ea.astype(jnp.int32), tile_eb.astype(jnp.int32), tile_valid


def kernel(x_prompt, x_sample, c_prompt, c_sample, cache_k, cache_v, state_pool, page_table, rel_bias, w_ada, b_ada, w_in, lambda_q1, lambda_k1, lambda_q2, lambda_k2, subln_g, w_pool, pool_scale, w_o, ln1_g, ln1_b, w_router_group, b_router_group, w_router_expert, b_router_expert, w_gate, w_up, w_down, ln2_g, ln2_b):
    depth = w_in.shape[0]
    nb, seq, d = x_prompt.shape
    n_seq, n_new, _ = x_sample.shape
    n_pages = page_table.shape[1]
    page = cache_k.shape[2]
    past_len = n_pages * page
    aw = N_HEADS * V_DIM
    pw = pool_scale.shape[1]
    alpha = (2 * depth) ** 0.25
    tq = min(ATTN_TILE, seq)
    assert tq >= MAX_DISTANCE and seq % tq == 0 and seq % min(SEQ_TILE, seq) == 0
    assert past_len >= POOL_HIST and n_new <= 8
    n_tok_p = nb * seq
    n_tok_s = n_seq * n_new
    assert n_tok_p % min(SEQ_TILE, n_tok_p) == 0
    n_rows = ((n_tok_p + n_tok_s) // EXPERT_TILE + N_PAIR_BUCKETS) * EXPERT_TILE
    pps = math.gcd(PAGES_PER_STEP, n_pages)
    n_steps = n_pages // pps

    bias_diag, bias_sub, c_far = _prompt_bias_tables(rel_bias, tq)
    bias_past, bias_new = _sample_bias_tables(rel_bias, past_len, n_new, pps * page)
    c_all = jnp.concatenate([c_prompt, c_sample], axis=0)
    xs_tm = jnp.transpose(x_sample, (1, 0, 2)).reshape(n_tok_s, d)

    xp, xs_cur = x_prompt, xs_tm
    kp_l, vp_l, pp_l, ks_l, vs_l, ps_l = [], [], [], [], [], []
    for l in range(depth):
        lam_init = 0.8 - 0.6 * math.exp(-0.3 * l)
        lam = (jnp.exp(jnp.sum(lambda_q1[l].astype(F32) * lambda_k1[l].astype(F32)))
               - jnp.exp(jnp.sum(lambda_q2[l].astype(F32) * lambda_k2[l].astype(F32))) + lam_init).reshape(1)
        win_b = w_in[l].astype(BF16)
        wpool_b = w_pool[l].astype(BF16)
        wo_b = w_o[l].astype(BF16)
        wr = jnp.concatenate([w_router_group[l], jnp.transpose(w_router_expert[l], (1, 0, 2)).reshape(d, N_EXPERTS)], axis=1)
        wr_b = jnp.pad(wr, ((0, 0), (0, LANES - wr.shape[1]))).astype(BF16)
        br = jnp.pad(jnp.concatenate([b_router_group[l], b_router_expert[l].reshape(-1)]).astype(F32),
                     (0, LANES - N_EXP_GROUPS - N_EXPERTS)).reshape(1, LANES)
        wg_b = w_gate[l].reshape(N_EXPERTS, d, -1)
        wu_b = w_up[l].reshape(N_EXPERTS, d, -1)
        wd_b = w_down[l].reshape(N_EXPERTS, -1, d)
        pscale = pool_scale[l].reshape(1, pw)
        g_sub = subln_g[l].reshape(1, V_DIM)
        ln1g, ln1b = ln1_g[l].reshape(1, d), ln1_b[l].reshape(1, d)
        ln2g, ln2b = ln2_g[l].reshape(1, d), ln2_b[l].reshape(1, d)

        m_all = _modulation(c_all, w_ada[l], b_ada[l]).reshape(nb + n_seq, 6, d)
        mod_p = m_all[:nb]
        mod_s = jnp.tile(jnp.transpose(m_all[nb:], (1, 0, 2)), (1, n_new, 1))

        q_p, k_p, v_p, kb_p, vb_p, pm_p, ph_p = _prompt_inproj(xp, mod_p, win_b, wpool_b, pscale)
        o_p = _prompt_attention(q_p, kb_p, vb_p, bias_diag, bias_sub, c_far, lam, g_sub, lam_init)
        y1_p, tokx_p, route_p, cnt_p = _outproj_router(o_p, pm_p, xp, mod_p, wo_b, ln1g, ln1b, wr_b, br,
                                                       alpha=alpha, per_row=False)

        hist = state_pool[l]
        q_s, k_s, v_s, p_s, pm_s = _sample_inproj(xs_cur, mod_s[0], mod_s[1], win_b, jnp.transpose(hist, (1, 0, 2)),
                                                  wpool_b, pscale, n_seq=n_seq, n_new=n_new, past_len=past_len)
        seq_major = lambda a: jnp.transpose(a.reshape(n_new, n_seq, -1), (1, 0, 2))
        new_rows = lambda a: jnp.pad(seq_major(a).reshape(n_seq, n_new * N_HEADS, V_DIM),
                                     ((0, 0), (0, LANES - n_new * N_HEADS), (0, 0)))
        o_s = _paged_attention(page_table, lam, seq_major(q_s), new_rows(k_s), new_rows(v_s), bias_past, bias_new, g_sub,
                               cache_k, cache_v, layer=l, lam_init=lam_init)
        o_s_tm = jnp.transpose(o_s, (1, 0, 2)).reshape(1, n_tok_s, aw)
        y1_s, tokx_s, route_s, cnt_s = _outproj_router(o_s_tm, pm_s[None], xs_cur[None], mod_s, wo_b, ln1g, ln1b,
                                                       wr_b, br, alpha=alpha, per_row=True)

        pos_p, pos_s, tile_ea, tile_eb, tile_valid = _routing_plan(route_p, cnt_p, route_s, cnt_s, n_rows)
        xs_sorted = jnp.zeros((n_rows, d + ROUTE_EXTRA), F32)
        xs_sorted = _dispatch(pos_p, tokx_p.reshape(n_tok_p, -1), xs_sorted)
        xs_sorted = _dispatch(pos_s, tokx_s.reshape(n_tok_s, -1), xs_sorted)
        o_sorted = _experts(tile_ea, tile_eb, tile_valid, xs_sorted, wg_b, wu_b, wd_b)
        xp = _combine(pos_p, y1_p, mod_p, ln2g, ln2b, o_sorted, alpha=alpha, per_row=False)
        xs_cur = _combine(pos_s, y1_s, mod_s, ln2g, ln2b, o_sorted, alpha=alpha, per_row=True)[0]

        kp_l.append(k_p.reshape(nb, seq, N_HEADS, V_DIM))
        vp_l.append(v_p.reshape(nb, seq, N_HEADS, V_DIM))
        pp_l.append(ph_p)
        ks_l.append(seq_major(k_s).reshape(n_seq, n_new, N_HEADS, V_DIM))
        vs_l.append(seq_major(v_s).reshape(n_seq, n_new, N_HEADS, V_DIM))
        ps_l.append(jnp.concatenate([hist, seq_major(p_s)], axis=1)[:, -POOL_HIST:])

    y_sample = jnp.transpose(xs_cur.reshape(n_new, n_seq, d), (1, 0, 2))
    return (xp, y_sample, jnp.stack(kp_l), jnp.stack(vp_l), jnp.stack(pp_l),
            jnp.stack(ks_l), jnp.stack(vs_l), jnp.stack(ps_l))
```

```python
import functools
import math

import numpy as np
import jax
import jax.numpy as jnp
from jax import lax
from jax.experimental import pallas as pl
from jax.experimental.pallas import tpu as pltpu

F32 = jnp.float32
BF16 = jnp.bfloat16

N_HEADS = 4
HEAD_DIM = 64
V_DIM = 2 * HEAD_DIM
POOL_WINDOWS = (2, 4, 8, 16)
POOL_GROUP = 128
POOL_HIST = max(POOL_WINDOWS) - 1
N_BUCKETS = 32
MAX_DISTANCE = 128
N_EXP_GROUPS = 4
EXP_PER_GROUP = 4
N_EXPERTS = N_EXP_GROUPS * EXP_PER_GROUP
EXPERT_PAIRS = ((0, 1), (0, 2), (0, 3), (1, 2), (1, 3), (2, 3))
N_PAIR_BUCKETS = N_EXP_GROUPS * len(EXPERT_PAIRS)
LN_EPS = 1e-5
RMS_EPS = 1e-5

LANES = 128
POOL_HALO = 16
VMEM_LIMIT_BYTES = 48 * 1024 * 1024
SEQ_TILE = 1024
ROW_COPY_TILE = 512
ATTN_TILE = 256
EXPERT_TILE = 256
PAGES_PER_STEP = 16
PAGE_RING = 3
ROUTE_EXTRA = LANES
LOG2E = math.log2(math.e)
Q_SCALE = HEAD_DIM ** -0.5 * LOG2E


def _params(*sem):
    return pltpu.CompilerParams(dimension_semantics=sem, vmem_limit_bytes=VMEM_LIMIT_BYTES)


def _dot(a, b):
    return jnp.dot(a, b, preferred_element_type=F32)


def _dot_nt(a, b):
    return lax.dot_general(a, b, (((1,), (1,)), ((), ())), preferred_element_type=F32)


def _sigmoid(x):
    return 1.0 / (1.0 + jnp.exp(-x))


def _mod_kernel(c_ref, w_ref, b_ref, o_ref):
    c = c_ref[...]
    s = c * _sigmoid(c)
    s_hi = s.astype(BF16)
    s_lo = (s - s_hi.astype(F32)).astype(BF16)
    w = w_ref[...]
    w_hi = w.astype(BF16)
    w_lo = (w - w_hi.astype(F32)).astype(BF16)
    o_ref[...] = _dot(s_hi, w_hi) + _dot(s_lo, w_hi) + _dot(s_hi, w_lo) + b_ref[...]


def _modulation(c_all, w_ada, b_ada):
    n, d = c_all.shape
    e = w_ada.shape[1]
    bn = 1024
    return pl.pallas_call(
        _mod_kernel,
        grid=(e // bn,),
        in_specs=[pl.BlockSpec((n, d), lambda j: (0, 0)),
                  pl.BlockSpec((d, bn), lambda j: (0, j)),
                  pl.BlockSpec((1, bn), lambda j: (0, j))],
        out_specs=pl.BlockSpec((n, bn), lambda j: (0, j)),
        out_shape=jax.ShapeDtypeStruct((n, e), F32),
        compiler_params=_params("arbitrary"),
        name="modulation",
    )(c_all, w_ada, b_ada.reshape(1, e))


def _pool_mixed(ext, g, w, inv_cnt, rows):
    eg = ext[:, g * POOL_GROUP:(g + 1) * POOL_GROUP]
    s = eg
    step = 1
    while step < w:
        s = s + pltpu.roll(s, step, 0)
        step *= 2
    return s[POOL_HALO:POOL_HALO + rows] * inv_cnt - eg[POOL_HALO:POOL_HALO + rows]


def _inproj_kernel(x_ref, mod_ref, win_ref, wpool_ref, pscale_ref,
                   q_ref, k_ref, v_ref, kb_ref, vb_ref, pm_ref, ph_ref, ext_ref, *, ts, aw):
    i = pl.program_id(1)

    @pl.when(i == 0)
    def _():
        ext_ref[0:POOL_HALO, :] = jnp.zeros((POOL_HALO, ext_ref.shape[1]), F32)

    @pl.when(i > 0)
    def _():
        ext_ref[0:POOL_HALO, :] = ext_ref[ts:ts + POOL_HALO, :]

    shift = mod_ref[0, 0:1, :]
    scale = mod_ref[0, 1:2, :]
    u = (x_ref[0] * (1.0 + scale) + shift).astype(BF16)
    acc = _dot(u, win_ref[...])
    q_ref[0] = (acc[:, 0:aw] * Q_SCALE).astype(BF16)
    k = acc[:, aw:2 * aw]
    v = acc[:, 2 * aw:3 * aw]
    for h in range(N_HEADS):
        k_ref[0, pl.ds(h, ts, stride=N_HEADS), :] = k[:, h * V_DIM:(h + 1) * V_DIM]
        v_ref[0, pl.ds(h, ts, stride=N_HEADS), :] = v[:, h * V_DIM:(h + 1) * V_DIM]
    kb_ref[0] = k.astype(BF16)
    vb_ref[0] = v.astype(BF16)
    p = acc[:, 3 * aw:]
    ext_ref[POOL_HALO:POOL_HALO + ts, :] = p
    ext = ext_ref[...]
    pos = i * ts + lax.broadcasted_iota(jnp.int32, (ts, 1), 0)
    for g, w in enumerate(POOL_WINDOWS):
        inv_cnt = 1.0 / jnp.minimum(w, pos + 1).astype(F32)
        mixed = _pool_mixed(ext, g, w, inv_cnt, ts)
        cols = slice(g * POOL_GROUP, (g + 1) * POOL_GROUP)
        pm_ref[0, :, cols] = (_dot(mixed.astype(BF16), wpool_ref[g]) * pscale_ref[:, cols]).astype(BF16)

    @pl.when(i == pl.num_programs(1) - 1)
    def _():
        ph_ref[0] = p[ts - POOL_HIST:ts, :]


def _prompt_inproj(x, mod, win_b, wpool_b, pscale):
    nb, s, d = x.shape
    ew = win_b.shape[1]
    aw = (ew - 4 * POOL_GROUP) // 3
    pw = ew - 3 * aw
    ts = min(SEQ_TILE, s)
    nt = s // ts
    tile = lambda b, i: (b, i, 0)
    const2 = lambda b, i: (0, 0)
    return pl.pallas_call(
        functools.partial(_inproj_kernel, ts=ts, aw=aw),
        grid=(nb, nt),
        in_specs=[pl.BlockSpec((1, ts, d), tile),
                  pl.BlockSpec((1, 6, d), lambda b, i: (b, 0, 0)),
                  pl.BlockSpec((d, ew), const2),
                  pl.BlockSpec(wpool_b.shape, lambda b, i: (0, 0, 0)),
                  pl.BlockSpec((1, pw), const2)],
        out_specs=[pl.BlockSpec((1, ts, aw), tile),
                   pl.BlockSpec((1, ts * N_HEADS, V_DIM), tile),
                   pl.BlockSpec((1, ts * N_HEADS, V_DIM), tile),
                   pl.BlockSpec((1, ts, aw), tile), pl.BlockSpec((1, ts, aw), tile),
                   pl.BlockSpec((1, ts, pw), tile),
                   pl.BlockSpec((1, POOL_HIST, pw), lambda b, i: (b, 0, 0))],
        out_shape=[jax.ShapeDtypeStruct((nb, s, aw), BF16),
                   jax.ShapeDtypeStruct((nb, s * N_HEADS, V_DIM), F32),
                   jax.ShapeDtypeStruct((nb, s * N_HEADS, V_DIM), F32),
                   jax.ShapeDtypeStruct((nb, s, aw), BF16),
                   jax.ShapeDtypeStruct((nb, s, aw), BF16),
                   jax.ShapeDtypeStruct((nb, s, pw), BF16),
                   jax.ShapeDtypeStruct((nb, POOL_HIST, pw), F32)],
        scratch_shapes=[pltpu.VMEM((ts + POOL_HALO, pw), F32)],
        compiler_params=_params("arbitrary", "arbitrary"),
        name="prompt_inproj",
    )(x, mod, win_b, wpool_b, pscale)


def _split_branches(qh):
    lane = lax.broadcasted_iota(jnp.int32, qh.shape, 1)
    zero = jnp.zeros_like(qh)
    return jnp.concatenate([jnp.where(lane < HEAD_DIM, qh, zero), jnp.where(lane >= HEAD_DIM, qh, zero)], axis=0)


def _lane_tile(a, width):
    return jnp.concatenate([a] * (width // LANES), axis=1) if width > LANES else a


def _online_update(s, values, m_ref, l_ref, acc_ref, first=False):
    rows, width = s.shape
    m_new = jnp.broadcast_to(jnp.max(s, axis=1, keepdims=True), (rows, LANES))
    if not first:
        m_old = m_ref[...]
        m_new = jnp.maximum(m_old, m_new)
    p = jnp.exp2(s - _lane_tile(m_new, width))
    pv = values(p.astype(BF16))
    l_new = jnp.broadcast_to(jnp.sum(p, axis=1, keepdims=True), (rows, LANES))
    if first:
        acc_ref[...] = pv
        l_ref[...] = l_new
    else:
        alpha = jnp.exp2(m_old - m_new)
        acc_ref[...] = _lane_tile(alpha, pv.shape[1]) * acc_ref[...] + pv
        l_ref[...] = alpha * l_ref[...] + l_new
    m_ref[...] = m_new


def _diff_head_out(acc, l, t, lam, g, lam_init):
    o = acc[0:t] / l[0:t] - lam * (acc[t:2 * t] / l[t:2 * t])
    o = o * lax.rsqrt(jnp.mean(o * o, axis=-1, keepdims=True) + RMS_EPS) * g
    return o * (1.0 - lam_init)


def _attn_kernel(cfar_ref, lam_ref, q_ref, k_ref, v_ref, bd_ref, be_ref, g_ref, o_ref,
                 qs_sc, m_sc, l_sc, acc_sc, *, tq, lam_init):
    qi = pl.program_id(1)
    heads = [slice(h * V_DIM, (h + 1) * V_DIM) for h in range(N_HEADS)]
    for h, cols in enumerate(heads):
        qs_sc[h] = _split_branches(q_ref[0, :, cols])

    def step(j, bias, first):
        rows = pl.ds(pl.multiple_of(j * tq, tq), tq)
        for h, cols in enumerate(heads):
            s = _dot_nt(qs_sc[h], k_ref[0, rows, cols]) + bias(h)
            _online_update(s, lambda p, cols=cols: _dot(p, v_ref[0, rows, cols]),
                           m_sc.at[h], l_sc.at[h], acc_sc.at[h], first=first)

    step(qi, lambda h: bd_ref[h], True)

    @pl.when(qi >= 1)
    def _():
        step(qi - 1, lambda h: be_ref[h], False)

    def far(j, carry):
        step(j, lambda h: cfar_ref[h], False)
        return carry

    lax.fori_loop(0, jnp.maximum(qi - 1, 0), far, 0)
    lam = lam_ref[0]
    for h, cols in enumerate(heads):
        o = _diff_head_out(acc_sc[h], l_sc[h], tq, lam, g_ref[...], lam_init)
        o_ref[0, :, cols] = o.astype(BF16)


def _prompt_attention(q, kb, vb, bias_diag, bias_sub, c_far, lam, subln_g, lam_init):
    nb, s, aw = q.shape
    tq = bias_diag.shape[2]
    smem = pl.BlockSpec(memory_space=pltpu.SMEM)
    const3 = lambda b, i: (0, 0, 0)
    return pl.pallas_call(
        functools.partial(_attn_kernel, tq=tq, lam_init=lam_init),
        grid=(nb, s // tq),
        in_specs=[smem, smem,
                  pl.BlockSpec((1, tq, aw), lambda b, i: (b, i, 0)),
                  pl.BlockSpec((1, s, aw), lambda b, i: (b, 0, 0)),
                  pl.BlockSpec((1, s, aw), lambda b, i: (b, 0, 0)),
                  pl.BlockSpec(bias_diag.shape, const3),
                  pl.BlockSpec(bias_sub.shape, const3),
                  pl.BlockSpec((1, V_DIM), lambda b, i: (0, 0))],
        out_specs=pl.BlockSpec((1, tq, aw), lambda b, i: (b, i, 0)),
        out_shape=jax.ShapeDtypeStruct((nb, s, aw), BF16),
        scratch_shapes=[pltpu.VMEM((N_HEADS, 2 * tq, V_DIM), BF16),
                        pltpu.VMEM((N_HEADS, 2 * tq, LANES), F32), pltpu.VMEM((N_HEADS, 2 * tq, LANES), F32),
                        pltpu.VMEM((N_HEADS, 2 * tq, V_DIM), F32)],
        compiler_params=_params("arbitrary", "arbitrary"),
        name="prompt_attention",
    )(c_far, lam, q, kb, vb, bias_diag, bias_sub, subln_g)


def _layer_norm(z, g, b):
    mu = jnp.mean(z, axis=-1, keepdims=True)
    zc = z - mu
    var = jnp.mean(zc * zc, axis=-1, keepdims=True)
    return zc * lax.rsqrt(var + LN_EPS) * g + b


def _mod_get(mod_ref, j, per_row):
    return mod_ref[j] if per_row else mod_ref[0, j:j + 1, :]


def _outproj_router_kernel(o_ref, pm_ref, x_ref, mod_ref, wo_ref, lng_ref, lnb_ref, wr_ref, br_ref,
                           y_ref, tokx_ref, route_ref, cnt_ref, carry_sc, *, alpha, per_row):
    t = x_ref.shape[1]
    d = x_ref.shape[2]
    aw = o_ref.shape[2]

    @pl.when((pl.program_id(0) == 0) & (pl.program_id(1) == 0))
    def _():
        carry_sc[...] = jnp.zeros(carry_sc.shape, F32)

    h = _dot(o_ref[0], wo_ref[0:aw, :]) + _dot(pm_ref[0], wo_ref[aw:, :])
    gate1 = _mod_get(mod_ref, 2, per_row)
    y1 = _layer_norm(alpha * x_ref[0] + gate1 * h, lng_ref[...], lnb_ref[...])
    y_ref[0] = y1
    tok = y1 * (1.0 + _mod_get(mod_ref, 4, per_row)) + _mod_get(mod_ref, 3, per_row)
    tokx_ref[0, :, 0:d] = tok

    logits = _dot(tok.astype(BF16), wr_ref[...]) + br_ref[...]
    lane = lax.broadcasted_iota(jnp.int32, logits.shape, 1)
    lane_f = lane.astype(F32)
    neg = jnp.full_like(logits, -jnp.inf)

    def first_argmax(vals, vmax):
        return jnp.min(jnp.where(vals == vmax, lane_f, float(LANES)), axis=1, keepdims=True).astype(jnp.int32)

    gl = jnp.where(lane < N_EXP_GROUPS, logits, neg)
    gmax = jnp.max(gl, axis=1, keepdims=True)
    gidx = first_argmax(gl, gmax)
    g_w = 1.0 / jnp.sum(jnp.exp(gl - gmax), axis=1, keepdims=True)
    lo_lane = N_EXP_GROUPS + EXP_PER_GROUP * gidx
    el = jnp.where((lane >= lo_lane) & (lane < lo_lane + EXP_PER_GROUP), logits, neg)
    v1 = jnp.max(el, axis=1, keepdims=True)
    i1 = first_argmax(el, v1)
    el2 = jnp.where(lane == i1, neg, el)
    v2 = jnp.max(el2, axis=1, keepdims=True)
    i2 = first_argmax(el2, v2)
    e21 = jnp.exp(v2 - v1)
    w1 = g_w / (1.0 + e21)
    w2 = g_w * e21 / (1.0 + e21)
    first_lo = i1 < i2
    cw_lo = jnp.where(first_lo, w1, w2)
    cw_hi = jnp.where(first_lo, w2, w1)
    a = jnp.minimum(i1, i2) - lo_lane
    b = jnp.maximum(i1, i2) - lo_lane
    pair_base = jnp.where(a == 0, 0, jnp.where(a == 1, 3, 5))
    bucket = gidx * len(EXPERT_PAIRS) + pair_base + (b - a - 1)

    xlane = lax.broadcasted_iota(jnp.int32, (t, ROUTE_EXTRA), 1)
    tokx_ref[0, :, d:] = jnp.where(xlane == 0, cw_lo, jnp.where(xlane == 1, cw_hi, 0.0))

    onehot = lane == bucket
    row = lax.broadcasted_iota(jnp.int32, (t, t), 0)
    col = lax.broadcasted_iota(jnp.int32, (t, t), 1)
    ltri = jnp.where(col < row, 1.0, 0.0).astype(BF16)
    prefix = _dot(ltri, jnp.where(onehot, 1.0, 0.0).astype(BF16)) + carry_sc[...]
    rank = jnp.sum(jnp.where(onehot, prefix, 0.0), axis=1, keepdims=True)
    carry_sc[...] = carry_sc[...] + jnp.sum(jnp.where(onehot, 1.0, 0.0), axis=0, keepdims=True)
    cnt_ref[...] = carry_sc[...]
    route = jnp.where(lane == 0, bucket.astype(F32), jnp.where(lane == 1, rank, 0.0))
    route_ref[0] = route.T[0:8, :]


def _outproj_router(o, pm, x, mod, wo_b, ln_g, ln_b, wr_b, br, *, alpha, per_row):
    nb, s, d = x.shape
    aw = o.shape[2]
    pw = pm.shape[2]
    t = min(SEQ_TILE, s)
    tile = lambda b, i: (b, i, 0)
    const2 = lambda b, i: (0, 0)
    mod_spec = (pl.BlockSpec(mod.shape, lambda b, i: (0, 0, 0)) if per_row
                else pl.BlockSpec((1, 6, d), lambda b, i: (b, 0, 0)))
    return pl.pallas_call(
        functools.partial(_outproj_router_kernel, alpha=alpha, per_row=per_row),
        grid=(nb, s // t),
        in_specs=[pl.BlockSpec((1, t, aw), tile), pl.BlockSpec((1, t, pw), tile), pl.BlockSpec((1, t, d), tile),
                  mod_spec,
                  pl.BlockSpec(wo_b.shape, const2), pl.BlockSpec((1, d), const2), pl.BlockSpec((1, d), const2),
                  pl.BlockSpec(wr_b.shape, const2), pl.BlockSpec((1, LANES), const2)],
        out_specs=[pl.BlockSpec((1, t, d), tile), pl.BlockSpec((1, t, d + ROUTE_EXTRA), tile),
                   pl.BlockSpec((1, 8, t), lambda b, i: (b, 0, i)), pl.BlockSpec((1, LANES), const2)],
        out_shape=[jax.ShapeDtypeStruct((nb, s, d), F32), jax.ShapeDtypeStruct((nb, s, d + ROUTE_EXTRA), F32),
                   jax.ShapeDtypeStruct((nb, 8, s), F32), jax.ShapeDtypeStruct((1, LANES), F32)],
        scratch_shapes=[pltpu.VMEM((1, LANES), F32)],
        compiler_params=_params("arbitrary", "arbitrary"),
        name="outproj_router",
    )(o, pm, x, mod, wo_b, ln_g, ln_b, wr_b, br)


def _issue_row_copies(src_ref, dst_ref, sem, src_row, dst_row, n):
    for r in range(n):
        pltpu.make_async_copy(src_ref.at[pl.ds(src_row(r), 1)], dst_ref.at[pl.ds(dst_row(r), 1)],
                              sem).start(priority=r % 2)


def _wait_row_copies(src_ref, dst_ref, sem, n):
    pltpu.make_async_copy(src_ref.at[pl.ds(0, n)], dst_ref.at[pl.ds(0, n)], sem).wait()


def _dispatch_kernel(pos_ref, tok_ref, xs_in_ref, xs_ref, sem, *, t):
    del xs_in_ref
    base = pl.program_id(0) * t
    _issue_row_copies(tok_ref, xs_ref, sem, lambda r: r, lambda r: pos_ref[base + r], t)
    _wait_row_copies(tok_ref, xs_ref, sem, t)


def _dispatch(pos, tokx, xs):
    n, w = tokx.shape
    t = min(ROW_COPY_TILE, n)
    return pl.pallas_call(
        functools.partial(_dispatch_kernel, t=t),
        grid_spec=pltpu.PrefetchScalarGridSpec(
            num_scalar_prefetch=1,
            grid=(n // t,),
            in_specs=[pl.BlockSpec((t, w), lambda i, pos: (i, 0)), pl.BlockSpec(memory_space=pl.ANY)],
            out_specs=pl.BlockSpec(memory_space=pl.ANY),
            scratch_shapes=[pltpu.SemaphoreType.DMA(())]),
        out_shape=jax.ShapeDtypeStruct(xs.shape, xs.dtype),
        input_output_aliases={2: 0},
        compiler_params=_params("arbitrary"),
        name="dispatch_rows",
    )(pos, tokx, xs)


def _expert_kernel(ea_ref, eb_ref, valid_ref, xs_ref, wga_ref, wua_ref, wda_ref, wgb_ref, wub_ref, wdb_ref,
                   o_ref, *, d):
    del ea_ref, eb_ref
    ti = pl.program_id(0)

    @pl.when(valid_ref[ti] == 1)
    def _():
        xs = xs_ref[...]
        x = xs[:, 0:d].astype(BF16)

        def ffn(wg_ref, wu_ref, wd_ref, cw):
            g = _dot(x, wg_ref[0].astype(BF16))
            u = _dot(x, wu_ref[0].astype(BF16))
            hid = (g * _sigmoid(g)) * u * cw
            return _dot(hid.astype(BF16), wd_ref[0].astype(BF16))

        o_ref[...] = (ffn(wga_ref, wua_ref, wda_ref, xs[:, d:d + 1])
                      + ffn(wgb_ref, wub_ref, wdb_ref, xs[:, d + 1:d + 2]))

    @pl.when(valid_ref[ti] == 0)
    def _():
        o_ref[...] = jnp.zeros(o_ref.shape, F32)


def _experts(tile_ea, tile_eb, tile_valid, xs, wg_b, wu_b, wd_b):
    r, w = xs.shape
    d = w - ROUTE_EXTRA
    f = wg_b.shape[2]
    tm = EXPERT_TILE
    wa = lambda blk: pl.BlockSpec(blk, lambda i, ea, eb, va: (ea[i], 0, 0))
    wb = lambda blk: pl.BlockSpec(blk, lambda i, ea, eb, va: (eb[i], 0, 0))
    return pl.pallas_call(
        functools.partial(_expert_kernel, d=d),
        grid_spec=pltpu.PrefetchScalarGridSpec(
            num_scalar_prefetch=3,
            grid=(r // tm,),
            in_specs=[pl.BlockSpec((tm, w), lambda i, ea, eb, va: (i, 0)),
                      wa((1, d, f)), wa((1, d, f)), wa((1, f, d)),
                      wb((1, d, f)), wb((1, d, f)), wb((1, f, d))],
            out_specs=pl.BlockSpec((tm, d), lambda i, ea, eb, va: (i, 0))),
        out_shape=jax.ShapeDtypeStruct((r, d), F32),
        compiler_params=_params("arbitrary"),
        name="expert_ffn",
    )(tile_ea, tile_eb, tile_valid, xs, wg_b, wu_b, wd_b, wg_b, wu_b, wd_b)


def _combine_kernel(pos_ref, y_ref, mod_ref, lng_ref, lnb_ref, os_ref, out_ref, buf, sem, *, alpha, per_row):
    t = y_ref.shape[1]
    k = pl.program_id(0) * pl.num_programs(1) + pl.program_id(1)
    n_tiles = pl.num_programs(0) * pl.num_programs(1)

    def fetch(tile):
        slot = tile % 2
        _issue_row_copies(os_ref, buf.at[slot], sem.at[slot], lambda r: pos_ref[tile * t + r], lambda r: r, t)

    @pl.when(k == 0)
    def _():
        fetch(k)

    @pl.when(k + 1 < n_tiles)
    def _():
        fetch(k + 1)

    slot = k % 2
    _wait_row_copies(os_ref, buf.at[slot], sem.at[slot], t)
    gate2 = _mod_get(mod_ref, 5, per_row)
    out_ref[0] = _layer_norm(alpha * y_ref[0] + gate2 * buf[slot], lng_ref[...], lnb_ref[...])


def _combine(pos, y1, mod, ln_g, ln_b, o_sorted, *, alpha, per_row):
    nb, s, d = y1.shape
    t = min(ROW_COPY_TILE, s)
    tile = lambda b, i, pos: (b, i, 0)
    const2 = lambda b, i, pos: (0, 0)
    mod_spec = (pl.BlockSpec(mod.shape, lambda b, i, pos: (0, 0, 0)) if per_row
                else pl.BlockSpec((1, 6, d), lambda b, i, pos: (b, 0, 0)))
    return pl.pallas_call(
        functools.partial(_combine_kernel, alpha=alpha, per_row=per_row),
        grid_spec=pltpu.PrefetchScalarGridSpec(
            num_scalar_prefetch=1,
            grid=(nb, s // t),
            in_specs=[pl.BlockSpec((1, t, d), tile), mod_spec,
                      pl.BlockSpec((1, d), const2), pl.BlockSpec((1, d), const2),
                      pl.BlockSpec(memory_space=pl.ANY)],
            out_specs=pl.BlockSpec((1, t, d), tile),
            scratch_shapes=[pltpu.VMEM((2, t, d), F32), pltpu.SemaphoreType.DMA((2,))]),
        out_shape=jax.ShapeDtypeStruct((nb, s, d), F32),
        compiler_params=_params("arbitrary", "arbitrary"),
        name="combine_rows",
    )(pos, y1, mod, ln_g, ln_b, o_sorted)


def _sample_inproj_kernel(x_ref, shift_ref, scale_ref, win_ref, hist_ref, wpool_ref, pscale_ref,
                          q_ref, k_ref, v_ref, p_ref, pm_ref, *, aw, n_seq, n_new, past_len):
    u = (x_ref[...] * (1.0 + scale_ref[...]) + shift_ref[...]).astype(BF16)
    acc = _dot(u, win_ref[...])
    q_ref[...] = (acc[:, 0:aw] * Q_SCALE).astype(BF16)
    k_ref[...] = acc[:, aw:2 * aw]
    v_ref[...] = acc[:, 2 * aw:3 * aw]
    p = acc[:, 3 * aw:]
    p_ref[...] = p
    n_hist = hist_ref.shape[0]
    rows = [hist_ref[j] for j in range(n_hist)] + [p[t * n_seq:(t + 1) * n_seq] for t in range(n_new)]
    for g, w in enumerate(POOL_WINDOWS):
        cols = slice(g * POOL_GROUP, (g + 1) * POOL_GROUP)
        mixed = []
        for t in range(n_new):
            end = n_hist + t + 1
            start = max(end - w, 0)
            win_sum = rows[start][:, cols]
            for j in range(start + 1, end):
                win_sum = win_sum + rows[j][:, cols]
            count = float(min(w, past_len + t + 1))
            mixed.append(win_sum / count - rows[n_hist + t][:, cols])
        mixed = jnp.concatenate(mixed, axis=0)
        pm_ref[:, cols] = (_dot(mixed.astype(BF16), wpool_ref[g]) * pscale_ref[:, cols]).astype(BF16)


def _sample_inproj(x, shift, scale, win_b, hist_t, wpool_b, pscale, *, n_seq, n_new, past_len):
    n, d = x.shape
    ew = win_b.shape[1]
    aw = (ew - 4 * POOL_GROUP) // 3
    pw = ew - 3 * aw
    return pl.pallas_call(
        functools.partial(_sample_inproj_kernel, aw=aw, n_seq=n_seq, n_new=n_new, past_len=past_len),
        out_shape=[jax.ShapeDtypeStruct((n, aw), BF16), jax.ShapeDtypeStruct((n, aw), F32),
                   jax.ShapeDtypeStruct((n, aw), F32), jax.ShapeDtypeStruct((n, pw), F32),
                   jax.ShapeDtypeStruct((n, pw), BF16)],
        compiler_params=pltpu.CompilerParams(vmem_limit_bytes=VMEM_LIMIT_BYTES),
        name="sample_inproj",
    )(x, shift, scale, win_b, hist_t, wpool_b, pscale)


def _paged_attn_kernel(pt_ref, lam_ref, q_ref, kn_ref, vn_ref, bias_ref, bn_ref, g_ref, ck_ref, cv_ref, o_ref,
                       kbuf, vbuf, sem, m_sc, l_sc, acc_sc, *, pps, page_rows, first_page, n_new, lam_init):
    step = pl.program_id(1)
    n_steps = pl.num_programs(1)
    last = n_steps - 1
    g = pl.program_id(0) * n_steps + step
    total = pl.num_programs(0) * n_steps
    heads = [slice(h * V_DIM, (h + 1) * V_DIM) for h in range(N_HEADS)]

    def fetch(gs):
        slot = gs % PAGE_RING
        for pg in range(pps):
            src = pl.ds(pl.multiple_of((first_page + pt_ref[gs * pps + pg]) * page_rows, page_rows), page_rows)
            dst = pl.ds(pg * page_rows, page_rows)
            pltpu.make_async_copy(ck_ref.at[src], kbuf.at[slot, dst], sem.at[slot, 0]).start()
            pltpu.make_async_copy(cv_ref.at[src], vbuf.at[slot, dst], sem.at[slot, 1]).start()

    @pl.when(g == 0)
    def _():
        for gs in range(PAGE_RING - 1):
            fetch(gs)

    @pl.when(g + PAGE_RING - 1 < total)
    def _():
        fetch(g + PAGE_RING - 1)

    @pl.when(step == 0)
    def _():
        m_sc[...] = jnp.full(m_sc.shape, -jnp.inf, F32)
        l_sc[...] = jnp.zeros(l_sc.shape, F32)
        acc_sc[...] = jnp.zeros(acc_sc.shape, F32)

    slot = g % PAGE_RING
    pltpu.make_async_copy(ck_ref.at[pl.ds(0, pps * page_rows)], kbuf.at[slot], sem.at[slot, 0]).wait()
    pltpu.make_async_copy(cv_ref.at[pl.ds(0, pps * page_rows)], vbuf.at[slot], sem.at[slot, 1]).wait()

    qn = q_ref[0]
    q_rows = jnp.concatenate([_split_branches(qn[:, cols]) for cols in heads], axis=0)
    bias = bias_ref[jnp.where(step == last, 1, 0)]
    page = lambda buf, pg: buf[slot, pg * page_rows:(pg + 1) * page_rows, :].astype(BF16)
    s = jnp.concatenate([_dot_nt(q_rows, page(kbuf, pg)) for pg in range(pps)], axis=1) + bias

    def past_values(p):
        acc = _dot(p[:, 0:page_rows], page(vbuf, 0))
        for pg in range(1, pps):
            acc = acc + _dot(p[:, pg * page_rows:(pg + 1) * page_rows], page(vbuf, pg))
        return acc

    _online_update(s, past_values, m_sc, l_sc, acc_sc)

    @pl.when(step == last)
    def _():
        sn = _dot_nt(q_rows, kn_ref[0].astype(BF16)) + bn_ref[...]
        _online_update(sn, lambda p: _dot(p, vn_ref[0].astype(BF16)), m_sc, l_sc, acc_sc)
        lam = lam_ref[0]
        for h, cols in enumerate(heads):
            rows = slice(h * 2 * n_new, (h + 1) * 2 * n_new)
            o = _diff_head_out(acc_sc[rows, :], l_sc[rows, :], n_new, lam, g_ref[...], lam_init)
            o_ref[0, :, cols] = o.astype(BF16)


def _paged_attention(page_table, lam, q, k_new, v_new, bias_past, bias_new, subln_g, cache_k, cache_v, *,
                     layer, lam_init):
    n_seq, n_pages = page_table.shape
    n_new, aw = q.shape[1], q.shape[2]
    n_phys, page = cache_k.shape[1], cache_k.shape[2]
    pps = math.gcd(PAGES_PER_STEP, n_pages)
    n_steps = n_pages // pps
    assert n_seq * n_steps >= PAGE_RING - 1
    n_rows = N_HEADS * 2 * n_new
    page_rows = page * N_HEADS
    ck = cache_k.reshape(-1, V_DIM)
    cv = cache_v.reshape(-1, V_DIM)
    per_seq = lambda n, s, pt: (n, 0, 0)
    hbm = pl.BlockSpec(memory_space=pl.ANY)
    return pl.pallas_call(
        functools.partial(_paged_attn_kernel, pps=pps, page_rows=page_rows, first_page=layer * n_phys,
                          n_new=n_new, lam_init=lam_init),
        grid_spec=pltpu.PrefetchScalarGridSpec(
            num_scalar_prefetch=1,
            grid=(n_seq, n_steps),
            in_specs=[pl.BlockSpec(memory_space=pltpu.SMEM),
                      pl.BlockSpec((1, n_new, aw), per_seq),
                      pl.BlockSpec((1,) + k_new.shape[1:], per_seq),
                      pl.BlockSpec((1,) + v_new.shape[1:], per_seq),
                      pl.BlockSpec(bias_past.shape, lambda n, s, pt: (0, 0, 0)),
                      pl.BlockSpec(bias_new.shape, lambda n, s, pt: (0, 0)),
                      pl.BlockSpec((1, V_DIM), lambda n, s, pt: (0, 0)), hbm, hbm],
            out_specs=pl.BlockSpec((1, n_new, aw), per_seq),
            scratch_shapes=[pltpu.VMEM((PAGE_RING, pps * page_rows, V_DIM), F32),
                            pltpu.VMEM((PAGE_RING, pps * page_rows, V_DIM), F32),
                            pltpu.SemaphoreType.DMA((PAGE_RING, 2)),
                            pltpu.VMEM((n_rows, LANES), F32), pltpu.VMEM((n_rows, LANES), F32),
                            pltpu.VMEM((n_rows, V_DIM), F32)]),
        out_shape=jax.ShapeDtypeStruct((n_seq, n_new, aw), BF16),
        compiler_params=_params("arbitrary", "arbitrary"),
        name="paged_attention",
    )(page_table.reshape(-1), lam, q, k_new, v_new, bias_past, bias_new, subln_g, ck, cv)


def _rel_bias_lookup(rel_bias, dist):
    n = jnp.maximum(dist, 0)
    max_exact = N_BUCKETS // 2
    nf = jnp.maximum(n, 1).astype(F32)
    large = max_exact + jnp.floor(jnp.log(nf / max_exact) / math.log(MAX_DISTANCE / max_exact)
                                  * (N_BUCKETS - max_exact)).astype(jnp.int32)
    large = jnp.minimum(large, N_BUCKETS - 1)
    bucket = jnp.where(n < max_exact, n, large)
    onehot = (bucket.reshape(1, -1) == jnp.arange(N_BUCKETS)[:, None]).astype(F32)
    table = jnp.dot(rel_bias.astype(F32).T, onehot, precision=lax.Precision.HIGHEST) * LOG2E
    return table.reshape((rel_bias.shape[1],) + dist.shape)


def _prompt_bias_tables(rel_bias, tq):
    i = jnp.arange(tq)[:, None]
    j = jnp.arange(tq)[None, :]
    diag = jnp.where((j <= i)[None], _rel_bias_lookup(rel_bias, i - j), -jnp.inf)
    sub = _rel_bias_lookup(rel_bias, tq + i - j)
    both = lambda b: jnp.concatenate([b, b], axis=1)
    return both(diag), both(sub), rel_bias[N_BUCKETS - 1].astype(F32) * LOG2E


def _head_interleaved_bias(rel_bias, base, n_new, n_keys, causal):
    lane = jnp.arange(n_keys * N_HEADS)
    key, lane_head = lane // N_HEADS, lane % N_HEADS
    t = jnp.arange(n_new)[:, None]
    table = _rel_bias_lookup(rel_bias, base + t - key[None, :])
    keep = lane_head[None, None, :] == jnp.arange(N_HEADS)[:, None, None]
    if causal:
        keep = keep & (key[None, :] <= t)[None]
    table = jnp.where(keep, table, -jnp.inf)
    return jnp.concatenate([table, table], axis=1).reshape(N_HEADS * 2 * n_new, n_keys * N_HEADS)


def _sample_bias_tables(rel_bias, past_len, n_new, step_keys):
    assert step_keys >= MAX_DISTANCE
    far = _head_interleaved_bias(rel_bias, past_len, n_new, step_keys, causal=False)
    near = _head_interleaved_bias(rel_bias, step_keys, n_new, step_keys, causal=False)
    new = _head_interleaved_bias(rel_bias, 0, n_new, LANES // N_HEADS, causal=True)
    return jnp.stack([far, near]), new


def _routing_plan(route_p, cnt_p, route_s, cnt_s, n_rows):
    tm = EXPERT_TILE
    bucket_p = route_p[:, 0, :].reshape(-1).astype(jnp.int32)
    rank_p = route_p[:, 1, :].reshape(-1).astype(jnp.int32)
    bucket_s = route_s[:, 0, :].reshape(-1).astype(jnp.int32)
    rank_s = route_s[:, 1, :].reshape(-1).astype(jnp.int32)
    tiles = jnp.ceil((cnt_p[0, :N_PAIR_BUCKETS] + cnt_s[0, :N_PAIR_BUCKETS]) / tm).astype(jnp.int32)
    cp = cnt_p[0, :N_PAIR_BUCKETS].astype(jnp.int32)
    tile_end = jnp.cumsum(tiles)
    off = (tile_end - tiles) * tm
    lookup = lambda table, idx: jnp.sum(
        jnp.where(idx[:, None] == jnp.arange(table.shape[0])[None, :], table[None, :], 0), axis=1)
    pos_p = lookup(off, bucket_p) + rank_p
    pos_s = lookup(off + cp, bucket_s) + rank_s
    n_tiles = n_rows // tm
    n_used = tile_end[-1]
    ti = jnp.minimum(jnp.arange(n_tiles), n_used - 1)
    tile_bucket = jnp.sum((ti[:, None] >= tile_end[None, :]).astype(jnp.int32), axis=1)
    pair = np.array(EXPERT_PAIRS, np.int32)
    base = np.arange(N_PAIR_BUCKETS) // len(EXPERT_PAIRS) * EXP_PER_GROUP
    tile_ea = lookup(jnp.asarray(base + pair[np.arange(N_PAIR_BUCKETS) % len(EXPERT_PAIRS), 0]), tile_bucket)
    tile_eb = lookup(jnp.asarray(base + pair[np.arange(N_PAIR_BUCKETS) % len(EXPERT_PAIRS), 1]), tile_bucket)
    tile_valid = (jnp.arange(n_tiles) < n_used).astype(jnp.int32)
    return pos_p, pos_s, tile_ea.astype(jnp.int32), tile_eb.astype(jnp.int32), tile_valid


def kernel(x_prompt, x_sample, c_prompt, c_sample, cache_k, cache_v, state_pool, page_table, rel_bias, w_ada, b_ada, w_in, lambda_q1, lambda_k1, lambda_q2, lambda_k2, subln_g, w_pool, pool_scale, w_o, ln1_g, ln1_b, w_router_group, b_router_group, w_router_expert, b_router_expert, w_gate, w_up, w_down, ln2_g, ln2_b):
    depth = w_in.shape[0]
    nb, seq, d = x_prompt.shape
    n_seq, n_new, _ = x_sample.shape
    n_pages = page_table.shape[1]
    page = cache_k.shape[2]
    past_len = n_pages * page
    aw = N_HEADS * V_DIM
    pw = pool_scale.shape[1]
    alpha = (2 * depth) ** 0.25
    tq = min(ATTN_TILE, seq)
    assert tq >= MAX_DISTANCE and seq % tq == 0 and seq % min(SEQ_TILE, seq) == 0
    assert past_len >= POOL_HIST and n_new <= 8
    n_tok_p = nb * seq
    n_tok_s = n_seq * n_new
    assert n_tok_p % min(SEQ_TILE, n_tok_p) == 0
    n_rows = ((n_tok_p + n_tok_s) // EXPERT_TILE + N_PAIR_BUCKETS) * EXPERT_TILE
    pps = math.gcd(PAGES_PER_STEP, n_pages)
    n_steps = n_pages // pps

    bias_diag, bias_sub, c_far = _prompt_bias_tables(rel_bias, tq)
    bias_past, bias_new = _sample_bias_tables(rel_bias, past_len, n_new, pps * page)
    c_all = jnp.concatenate([c_prompt, c_sample], axis=0)
    xs_tm = jnp.transpose(x_sample, (1, 0, 2)).reshape(n_tok_s, d)

    xp, xs_cur = x_prompt, xs_tm
    kp_l, vp_l, pp_l, ks_l, vs_l, ps_l = [], [], [], [], [], []
    for l in range(depth):
        lam_init = 0.8 - 0.6 * math.exp(-0.3 * l)
        lam = (jnp.exp(jnp.sum(lambda_q1[l].astype(F32) * lambda_k1[l].astype(F32)))
               - jnp.exp(jnp.sum(lambda_q2[l].astype(F32) * lambda_k2[l].astype(F32))) + lam_init).reshape(1)
        win_b = w_in[l].astype(BF16)
        wpool_b = w_pool[l].astype(BF16)
        wo_b = w_o[l].astype(BF16)
        wr = jnp.concatenate([w_router_group[l], jnp.transpose(w_router_expert[l], (1, 0, 2)).reshape(d, N_EXPERTS)], axis=1)
        wr_b = jnp.pad(wr, ((0, 0), (0, LANES - wr.shape[1]))).astype(BF16)
        br = jnp.pad(jnp.concatenate([b_router_group[l], b_router_expert[l].reshape(-1)]).astype(F32),
                     (0, LANES - N_EXP_GROUPS - N_EXPERTS)).reshape(1, LANES)
        wg_b = w_gate[l].reshape(N_EXPERTS, d, -1)
        wu_b = w_up[l].reshape(N_EXPERTS, d, -1)
        wd_b = w_down[l].reshape(N_EXPERTS, -1, d)
        pscale = pool_scale[l].reshape(1, pw)
        g_sub = subln_g[l].reshape(1, V_DIM)
        ln1g, ln1b = ln1_g[l].reshape(1, d), ln1_b[l].reshape(1, d)
        ln2g, ln2b = ln2_g[l].reshape(1, d), ln2_b[l].reshape(1, d)

        m_all = _modulation(c_all, w_ada[l], b_ada[l]).reshape(nb + n_seq, 6, d)
        mod_p = m_all[:nb]
        mod_s = jnp.tile(jnp.transpose(m_all[nb:], (1, 0, 2)), (1, n_new, 1))

        q_p, k_p, v_p, kb_p, vb_p, pm_p, ph_p = _prompt_inproj(xp, mod_p, win_b, wpool_b, pscale)
        o_p = _prompt_attention(q_p, kb_p, vb_p, bias_diag, bias_sub, c_far, lam, g_sub, lam_init)
        y1_p, tokx_p, route_p, cnt_p = _outproj_router(o_p, pm_p, xp, mod_p, wo_b, ln1g, ln1b, wr_b, br,
                                                       alpha=alpha, per_row=False)

        hist = state_pool[l]
        q_s, k_s, v_s, p_s, pm_s = _sample_inproj(xs_cur, mod_s[0], mod_s[1], win_b, jnp.transpose(hist, (1, 0, 2)),
                                                  wpool_b, pscale, n_seq=n_seq, n_new=n_new, past_len=past_len)
        seq_major = lambda a: jnp.transpose(a.reshape(n_new, n_seq, -1), (1, 0, 2))
        new_rows = lambda a: jnp.pad(seq_major(a).reshape(n_seq, n_new * N_HEADS, V_DIM),
                                     ((0, 0), (0, LANES - n_new * N_HEADS), (0, 0)))
        o_s = _paged_attention(page_table, lam, seq_major(q_s), new_rows(k_s), new_rows(v_s), bias_past, bias_new, g_sub,
                               cache_k, cache_v, layer=l, lam_init=lam_init)
        o_s_tm = jnp.transpose(o_s, (1, 0, 2)).reshape(1, n_tok_s, aw)
        y1_s, tokx_s, route_s, cnt_s = _outproj_router(o_s_tm, pm_s[None], xs_cur[None], mod_s, wo_b, ln1g, ln1b,
                                                       wr_b, br, alpha=alpha, per_row=True)

        pos_p, pos_s, tile_ea, tile_eb, tile_valid = _routing_plan(route_p, cnt_p, route_s, cnt_s, n_rows)
        xs_sorted = jnp.zeros((n_rows, d + ROUTE_EXTRA), F32)
        xs_sorted = _dispatch(pos_p, tokx_p.reshape(n_tok_p, -1), xs_sorted)
        xs_sorted = _dispatch(pos_s, tokx_s.reshape(n_tok_s, -1), xs_sorted)
        o_sorted = _experts(tile_ea, tile_eb, tile_valid, xs_sorted, wg_b, wu_b, wd_b)
        xp = _combine(pos_p, y1_p, mod_p, ln2g, ln2b, o_sorted, alpha=alpha, per_row=False)
        xs_cur = _combine(pos_s, y1_s, mod_s, ln2g, ln2b, o_sorted, alpha=alpha, per_row=True)[0]

        kp_l.append(k_p.reshape(nb, seq, N_HEADS, V_DIM))
        vp_l.append(v_p.reshape(nb, seq, N_HEADS, V_DIM))
        pp_l.append(ph_p)
        ks_l.append(seq_major(k_s).reshape(n_seq, n_new, N_HEADS, V_DIM))
        vs_l.append(seq_major(v_s).reshape(n_seq, n_new, N_HEADS, V_DIM))
        ps_l.append(jnp.concatenate([hist, seq_major(p_s)], axis=1)[:, -POOL_HIST:])

    y_sample = jnp.transpose(xs_cur.reshape(n_new, n_seq, d), (1, 0, 2))
    return (xp, y_sample, jnp.stack(kp_l), jnp.stack(vp_l), jnp.stack(pp_l),
            jnp.stack(ks_l), jnp.stack(vs_l), jnp.stack(ps_l))
```
